```python
import math
import jax
import jax.numpy as jnp
from jax import lax
import numpy as np

D_MODEL = 1024
BATCH = 1
SEQ = 16384
DEPTH = 2
DEC_BATCH = 4
DEC_SEQ = 8192
PAST_LEN = 128

GRID_W = 64
HGRN_HEADS = 4
HGRN_KEY_DIM = 128
HGRN_VAL_DIM = 128
HGRN_WIDTH = HGRN_HEADS * HGRN_VAL_DIM
HGRN_CHUNK = 32
GQA_HEADS = 4
GQA_KV_HEADS = 2
GQA_GROUP = GQA_HEADS // GQA_KV_HEADS
GQA_HEAD_DIM = 64
GQA_WIDTH = GQA_HEADS * GQA_HEAD_DIM
DIFF_HEADS = 4
DIFF_HEAD_DIM = 32
DIFF_WIDTH = DIFF_HEADS * 2 * DIFF_HEAD_DIM
MIX_WIDTH = HGRN_WIDTH + GQA_WIDTH + DIFF_WIDTH
Q_BLOCK = 128
ROPE_THETA = 10000.0
NORM_EPS = 1e-6
SPLIT_SIZES = (
    HGRN_HEADS * HGRN_KEY_DIM,
    HGRN_HEADS * HGRN_KEY_DIM,
    HGRN_HEADS * HGRN_KEY_DIM,
    HGRN_WIDTH,
    HGRN_WIDTH,
    GQA_HEADS * GQA_HEAD_DIM,
    GQA_KV_HEADS * GQA_HEAD_DIM,
    GQA_KV_HEADS * GQA_HEAD_DIM,
    GQA_WIDTH,
    DIFF_HEADS * 2 * DIFF_HEAD_DIM,
    DIFF_HEADS * 2 * DIFF_HEAD_DIM,
    DIFF_WIDTH,
    DIFF_WIDTH,
)
IN_COLS = 4352

kernel_name = "hymba_hgrn2_axialgqa_diffattn_encoder"


def _rms(x, w):
    xf = x.astype(jnp.float32)
    y = xf * lax.rsqrt(jnp.mean(xf * xf, axis=-1, keepdims=True) + NORM_EPS)
    return (y * w.astype(jnp.float32)).astype(x.dtype)


def _rope_tables(pos, dim):
    inv = jnp.power(ROPE_THETA, -jnp.arange(0, dim, 2, dtype=jnp.float32) / dim)
    ang = pos[:, None] * inv[None, :]
    ang = jnp.concatenate([ang, ang], axis=-1)
    return jnp.cos(ang), jnp.sin(ang)


def _apply_rope(x, cos, sin):
    shape = (1, x.shape[1]) + (1,) * (x.ndim - 3) + (x.shape[-1],)
    c = cos.reshape(shape)
    s = sin.reshape(shape)
    xf = x.astype(jnp.float32)
    half = x.shape[-1] // 2
    rot = jnp.concatenate([-xf[..., half:], xf[..., :half]], axis=-1)
    return (xf * c + rot * s).astype(x.dtype)


def _gla_chunked(q, k, v, g):
    B, H, L, dk = q.shape
    dv = v.shape[-1]
    n = L // HGRN_CHUNK
    q = q.reshape(B, H, n, HGRN_CHUNK, dk)
    k = k.reshape(B, H, n, HGRN_CHUNK, dk)
    v = v.reshape(B, H, n, HGRN_CHUNK, dv)
    g = g.reshape(B, H, n, HGRN_CHUNK, dk)
    b = jnp.cumsum(g, axis=3)
    mid = HGRN_CHUNK // 2
    b_mid = b[:, :, :, mid:mid + 1]
    b_last = b[:, :, :, -1:]
    qm = q * jnp.exp(b - b_mid)
    km = k * jnp.exp(b_mid - b)
    a = jnp.einsum('bhntd,bhnsd->bhnts', qm, km)
    causal_in_chunk = jnp.tril(jnp.ones((HGRN_CHUNK, HGRN_CHUNK), dtype=bool))
    a = jnp.where(causal_in_chunk, a, 0.0)
    o_intra = jnp.einsum('bhnts,bhnse->bhnte', a, v)
    d_state = jnp.einsum('bhnsd,bhnse->bhnde', k * jnp.exp(b_last - b), v)
    chunk_decay = jnp.exp(b_last[:, :, :, 0, :])

    def step(S, inp):
        dec, ds = inp
        return dec[..., None] * S + ds, S

    S0 = jnp.zeros((B, H, dk, dv), dtype=jnp.float32)
    _, S_prev = lax.scan(step, S0, (jnp.moveaxis(chunk_decay, 2, 0), jnp.moveaxis(d_state, 2, 0)))
    S_prev = jnp.moveaxis(S_prev, 0, 2)
    o_inter = jnp.einsum('bhntd,bhnde->bhnte', q * jnp.exp(b), S_prev)
    return (o_intra + o_inter).reshape(B, H, L, dv)


def _hgrn2_branch(aq, af, afb, ai, lb, norm_w):
    B, L, _ = aq.shape

    def heads(t):
        return t.astype(jnp.float32).reshape(B, L, HGRN_HEADS, -1).transpose(0, 2, 1, 3)

    q = jax.nn.silu(heads(aq))
    v = heads(ai)
    out = None
    for direction, fx in enumerate((af, afb)):
        lbd = lb[direction].reshape(HGRN_HEADS, 1, HGRN_KEY_DIM)
        xf = heads(fx)
        g = jnp.logaddexp(jnp.log(lbd), jnp.log1p(-lbd) + jax.nn.log_sigmoid(xf))
        k = (1.0 - lbd) * jax.nn.sigmoid(-xf)
        if direction == 0:
            o = _gla_chunked(q, k, v, g)
        else:
            o = jnp.flip(_gla_chunked(jnp.flip(q, 2), jnp.flip(k, 2), jnp.flip(v, 2), jnp.flip(g, 2)), 2)
        out = o if out is None else out + o
    out = out.transpose(0, 2, 1, 3)
    out = _rms(out, norm_w.reshape(HGRN_HEADS, HGRN_VAL_DIM))
    return out.reshape(B, L, HGRN_WIDTH).astype(aq.dtype)


def _query_blocks(q):
    B, L = q.shape[:2]
    nb = L // Q_BLOCK
    return jnp.moveaxis(q.reshape((B, nb, Q_BLOCK) + q.shape[2:]), 1, 0)


def _unblock(o, B, L):
    o = jnp.moveaxis(o, 0, 1)
    return o.reshape((B, L) + o.shape[3:])


def _gqa_attention(q, k, v):
    B, L = q.shape[:2]
    scale = 1.0 / math.sqrt(GQA_HEAD_DIM)

    def blk(qi):
        s = jnp.einsum('bqkgd,bskd->bkgqs', qi, k).astype(jnp.float32) * scale
        p = jax.nn.softmax(s, axis=-1).astype(v.dtype)
        return jnp.einsum('bkgqs,bskd->bqkgd', p, v)

    o = lax.map(blk, _query_blocks(q))
    return _unblock(o, B, L)


def _diff_attention(q, k, v, lam):
    B, L = q.shape[:2]
    scale = 1.0 / math.sqrt(DIFF_HEAD_DIM)

    def blk(qi):
        s = jnp.einsum('bqhcd,bshcd->bhcqs', qi, k).astype(jnp.float32) * scale
        p = jax.nn.softmax(s, axis=-1)
        w = (p[:, :, 0] - lam * p[:, :, 1]).astype(v.dtype)
        return jnp.einsum('bhqs,bshe->bqhe', w, v)

    o = lax.map(blk, _query_blocks(q))
    return _unblock(o, B, L)


def _trunk(x, pre_norm_w, w_in, hgrn_lb, hgrn_norm_w, gqa_q_norm_w, gqa_k_norm_w,
           diff_lambda, diff_norm_w, w_out, post_norm_w):
    B, L, _ = x.shape
    rows = L // GRID_W
    pos = jnp.arange(L, dtype=jnp.float32)
    row_pos = jnp.repeat(jnp.arange(rows, dtype=jnp.float32), GRID_W)
    col_pos = jnp.tile(jnp.arange(GRID_W, dtype=jnp.float32), rows)
    half = GQA_HEAD_DIM // 2
    cos_r, sin_r = _rope_tables(row_pos, half)
    cos_c, sin_c = _rope_tables(col_pos, half)
    cos_1, sin_1 = _rope_tables(pos, DIFF_HEAD_DIM)

    lb_all = jnp.cumsum(jax.nn.softmax(hgrn_lb.astype(jnp.float32), axis=0), axis=0)
    lb_all = lb_all - lb_all[:1]

    split_idx = []
    acc = 0
    for s in SPLIT_SIZES[:-1]:
        acc += s
        split_idx.append(acc)

    def axial(t):
        return jnp.concatenate([_apply_rope(t[..., :half], cos_r, sin_r),
                                _apply_rope(t[..., half:], cos_c, sin_c)], axis=-1)

    for layer in range(DEPTH):
        h = _rms(x, pre_norm_w[layer])
        proj = jnp.einsum('bld,dc->blc', h, w_in[layer])
        (aq, af, afb, ai, ag, bq, bk, bv, bg, cq, ck, cv, cg) = jnp.split(proj, split_idx, axis=-1)

        a_out = _hgrn2_branch(aq, af, afb, ai, lb_all[layer], hgrn_norm_w[layer]) * jax.nn.silu(ag)

        q = _rms(bq.reshape(B, L, GQA_KV_HEADS, GQA_GROUP, GQA_HEAD_DIM), gqa_q_norm_w[layer])
        k = _rms(bk.reshape(B, L, GQA_KV_HEADS, GQA_HEAD_DIM), gqa_k_norm_w[layer])
        v = bv.reshape(B, L, GQA_KV_HEADS, GQA_HEAD_DIM)
        b_out = _gqa_attention(axial(q), axial(k), v).reshape(B, L, GQA_WIDTH) * jax.nn.silu(bg)

        lam_init = 0.8 - 0.6 * math.exp(-0.3 * layer)
        lp = diff_lambda[layer].astype(jnp.float32)
        lam = jnp.exp(jnp.sum(lp[0] * lp[1])) - jnp.exp(jnp.sum(lp[2] * lp[3])) + lam_init
        q = _apply_rope(cq.reshape(B, L, DIFF_HEADS, 2, DIFF_HEAD_DIM), cos_1, sin_1)
        k = _apply_rope(ck.reshape(B, L, DIFF_HEADS, 2, DIFF_HEAD_DIM), cos_1, sin_1)
        v = cv.reshape(B, L, DIFF_HEADS, 2 * DIFF_HEAD_DIM)
        o = _rms(_diff_attention(q, k, v, lam), diff_norm_w[layer]) * (1.0 - lam_init)
        c_out = o.reshape(B, L, DIFF_WIDTH) * jax.nn.silu(cg)

        mix = jnp.concatenate([a_out, b_out, c_out], axis=-1)
        y = jnp.einsum('blc,cd->bld', mix, w_out[layer])
        x = x + _rms(y, post_norm_w[layer])
    return x


def setup_inputs(seed: int = 0) -> dict:
    key = jax.random.key(seed)
    ks = jax.random.split(key, 13)
    f32 = jnp.float32
    return {
        "x_prompt": jax.random.normal(ks[0], (BATCH, SEQ, D_MODEL), f32),
        "x_sample": jax.random.normal(ks[1], (DEC_BATCH, DEC_SEQ, D_MODEL), f32),
        "pre_norm_w": 1.0 + 0.05 * jax.random.normal(ks[2], (DEPTH, D_MODEL), f32),
        "w_in": jax.random.normal(ks[3], (DEPTH, D_MODEL, IN_COLS), f32) * D_MODEL ** -0.5,
        "hgrn_lb": 0.1 * jax.random.normal(ks[4], (DEPTH, 2, HGRN_HEADS * HGRN_KEY_DIM), f32),
        "hgrn_norm_w": 1.0 + 0.05 * jax.random.normal(ks[5], (DEPTH, HGRN_WIDTH), f32),
        "gqa_q_norm_w": 1.0 + 0.05 * jax.random.normal(ks[6], (DEPTH, GQA_HEAD_DIM), f32),
        "gqa_k_norm_w": 1.0 + 0.05 * jax.random.normal(ks[7], (DEPTH, GQA_HEAD_DIM), f32),
        "diff_lambda": 0.1 * jax.random.normal(ks[8], (DEPTH, 4, DIFF_HEAD_DIM), f32),
        "diff_norm_w": 1.0 + 0.05 * jax.random.normal(ks[9], (DEPTH, 2 * DIFF_HEAD_DIM), f32),
        "w_out": jax.random.normal(ks[10], (DEPTH, MIX_WIDTH, D_MODEL), f32) * MIX_WIDTH ** -0.5,
        "post_norm_w": 1.0 + 0.05 * jax.random.normal(ks[11], (DEPTH, D_MODEL), f32),
    }


def reference(x_prompt, x_sample, pre_norm_w, w_in, hgrn_lb, hgrn_norm_w, gqa_q_norm_w,
              gqa_k_norm_w, diff_lambda, diff_norm_w, w_out, post_norm_w):
    y_prompt = _trunk(x_prompt, pre_norm_w, w_in, hgrn_lb, hgrn_norm_w, gqa_q_norm_w, gqa_k_norm_w,
                      diff_lambda, diff_norm_w, w_out, post_norm_w)
    y_sample = _trunk(x_sample, pre_norm_w, w_in, hgrn_lb, hgrn_norm_w, gqa_q_norm_w, gqa_k_norm_w,
                      diff_lambda, diff_norm_w, w_out, post_norm_w)
    return (y_prompt, y_sample)
```

```python
import functools
import math

import jax
import jax.numpy as jnp
from jax import lax
from jax.experimental import pallas as pl
from jax.experimental.pallas import tpu as pltpu

F32 = jnp.float32
BF16 = jnp.bfloat16

D_MODEL = 1024
DEPTH = 2
GRID_W = 64
HGRN_HEADS = 4
HGRN_DIM = 128
HGRN_WIDTH = HGRN_HEADS * HGRN_DIM
HGRN_CHUNK = 32
GQA_HEADS = 4
GQA_KV_HEADS = 2
GQA_HEAD_DIM = 64
GQA_WIDTH = GQA_HEADS * GQA_HEAD_DIM
GQA_KV_WIDTH = GQA_KV_HEADS * GQA_HEAD_DIM
DIFF_HEADS = 4
DIFF_HEAD_DIM = 32
DIFF_WIDTH = DIFF_HEADS * 2 * DIFF_HEAD_DIM
MIX_WIDTH = HGRN_WIDTH + GQA_WIDTH + DIFF_WIDTH
ROPE_THETA = 10000.0
ROPE_DIM = 32
NORM_EPS = 1e-6

A_COLS = 4 * HGRN_WIDTH
OFF_AG = A_COLS
OFF_B = OFF_AG + HGRN_WIDTH
OFF_BG = OFF_B + GQA_WIDTH + 2 * GQA_KV_WIDTH
OFF_C = OFF_BG + GQA_WIDTH
IN_COLS = OFF_C + 4 * DIFF_WIDTH

LANES = 128
NEG_BIG = -1e30
VMEM_LIMIT = 56 * 1024 * 1024

NT_DIMS = (((1,), (1,)), ((), ()))
TN_DIMS = (((0,), (0,)), ((), ()))


def _params(sem):
    return pltpu.CompilerParams(dimension_semantics=sem, vmem_limit_bytes=VMEM_LIMIT)


def _const_spec(shape):
    nd = len(shape)
    return pl.BlockSpec(shape, lambda *_: (0,) * nd)


def _rot_half16(x):
    outs = []
    for s in range(x.shape[1] // LANES):
        xs = x[:, s * LANES:(s + 1) * LANES]
        up = pltpu.roll(xs, ROPE_DIM // 2, 1)
        dn = pltpu.roll(xs, LANES - ROPE_DIM // 2, 1)
        lane = lax.broadcasted_iota(jnp.int32, xs.shape, 1)
        outs.append(jnp.where((lane & (ROPE_DIM - 1)) < ROPE_DIM // 2, dn, up))
    return outs[0] if len(outs) == 1 else jnp.concatenate(outs, axis=1)


def _silu(x):
    return x * (1.0 / (1.0 + jnp.exp(-x)))


def _inproj_kernel(x_ref, pw_ref, w_ref, cosb_ref, sinb_ref, cosc_ref, sinc_ref,
                   qnw_ref, knw_ref, g64_ref,
                   pa_ref, qb_ref, kb_ref, vb_ref, qc_ref, kc_ref, vc_ref, sg_ref):
    x = x_ref[...]
    h = x * lax.rsqrt(jnp.mean(x * x, axis=-1, keepdims=True) + NORM_EPS) * pw_ref[...]
    hb = h.astype(BF16)

    def proj(lo, hi):
        return jnp.dot(hb, w_ref[:, lo:hi], preferred_element_type=F32)

    pa_ref[...] = proj(0, A_COLS)
    sg_ref[:, 0:HGRN_WIDTH] = _silu(proj(OFF_AG, OFF_B))

    def head_rms(t, w, g):
        ms = jnp.dot((t * t).astype(BF16), g, preferred_element_type=F32)
        return t * lax.rsqrt(ms + NORM_EPS) * w

    def rope(t, cos, sin):
        return t * cos + _rot_half16(t) * sin

    g64 = g64_ref[...]
    cosb = cosb_ref[...]
    sinb = sinb_ref[...]
    bq = head_rms(proj(OFF_B, OFF_B + GQA_WIDTH), qnw_ref[...], g64)
    qb_ref[...] = (rope(bq, cosb, sinb) * (1.0 / math.sqrt(GQA_HEAD_DIM))).astype(BF16)
    off_k = OFF_B + GQA_WIDTH
    bk = head_rms(proj(off_k, off_k + GQA_KV_WIDTH), knw_ref[...],
                  g64[:GQA_KV_WIDTH, :GQA_KV_WIDTH])
    kb_ref[...] = rope(bk, cosb[:, :GQA_KV_WIDTH], sinb[:, :GQA_KV_WIDTH]).astype(BF16)
    off_v = off_k + GQA_KV_WIDTH
    vb_ref[...] = proj(off_v, OFF_BG).astype(BF16)
    sg_ref[:, HGRN_WIDTH:HGRN_WIDTH + GQA_WIDTH] = _silu(proj(OFF_BG, OFF_C))

    cosc = cosc_ref[...]
    sinc = sinc_ref[...]
    cq = proj(OFF_C, OFF_C + DIFF_WIDTH)
    qc_ref[...] = (rope(cq, cosc, sinc) * (1.0 / math.sqrt(DIFF_HEAD_DIM))).astype(BF16)
    ck = proj(OFF_C + DIFF_WIDTH, OFF_C + 2 * DIFF_WIDTH)
    kc_ref[...] = rope(ck, cosc, sinc).astype(BF16)
    vc_ref[...] = proj(OFF_C + 2 * DIFF_WIDTH, OFF_C + 3 * DIFF_WIDTH).astype(BF16)
    sg_ref[:, HGRN_WIDTH + GQA_WIDTH:] = _silu(proj(OFF_C + 3 * DIFF_WIDTH, IN_COLS))


def _inproj(x2, seq_len, pre_w, w_in_bf, tabs, qnw, knw, g64, tm):
    t = x2.shape[0]
    tiles_per_seq = seq_len // tm
    tok = lambda w: pl.BlockSpec((tm, w), lambda i: (i, 0))
    tab = pl.BlockSpec((tm, GQA_WIDTH), lambda i: (i % tiles_per_seq, 0))
    out_w = (A_COLS, GQA_WIDTH, GQA_KV_WIDTH, GQA_KV_WIDTH, DIFF_WIDTH, DIFF_WIDTH, DIFF_WIDTH, MIX_WIDTH)
    out_dt = (F32, BF16, BF16, BF16, BF16, BF16, BF16, F32)
    return pl.pallas_call(
        _inproj_kernel,
        grid=(t // tm,),
        in_specs=[tok(D_MODEL), _const_spec((1, D_MODEL)), _const_spec((D_MODEL, IN_COLS)),
                  tab, tab, tab, tab,
                  _const_spec((1, GQA_WIDTH)), _const_spec((1, GQA_KV_WIDTH)),
                  _const_spec((GQA_WIDTH, GQA_WIDTH))],
        out_specs=[tok(w) for w in out_w],
        out_shape=[jax.ShapeDtypeStruct((t, w), d) for w, d in zip(out_w, out_dt)],
        compiler_params=_params(("parallel",)),
        name="inproj",
    )(x2, pre_w, w_in_bf, *tabs, qnw, knw, g64)


def _log1p(x):
    return jnp.log(1.0 + x)


def _hgrn_direction(q, xf, v, lb, tri, state_ref, reverse):
    tb = q.shape[0]
    nc = tb // HGRN_CHUNK
    log_sig = jnp.minimum(xf, 0.0) - _log1p(jnp.exp(-jnp.abs(xf)))
    c = _log1p(-lb) + log_sig
    a = jnp.log(lb)
    g = jnp.maximum(a, c) + _log1p(jnp.exp(-jnp.abs(a - c)))
    k = (1.0 - lb) * (1.0 / (1.0 + jnp.exp(xf)))
    qs = _silu(q)

    g_hi = g.astype(BF16)
    g_lo = (g - g_hi.astype(F32)).astype(BF16)
    b = (jnp.dot(tri, g_hi, preferred_element_type=F32)
         + jnp.dot(tri, g_lo, preferred_element_type=F32))

    mid = HGRN_CHUNK // 2 - 1 if reverse else HGRN_CHUNK // 2
    last = 0 if reverse else HGRN_CHUNK - 1
    b_mid, b_last = [], []
    for ci in range(nc):
        r0 = ci * HGRN_CHUNK
        b_mid.append(jnp.broadcast_to(b[r0 + mid:r0 + mid + 1, :], (HGRN_CHUNK, HGRN_DIM)))
        b_last.append(jnp.broadcast_to(b[r0 + last:r0 + last + 1, :], (HGRN_CHUNK, HGRN_DIM)))
    b_mid = jnp.concatenate(b_mid, axis=0)
    b_last = jnp.concatenate(b_last, axis=0)

    qm = (qs * jnp.exp(b - b_mid)).astype(BF16)
    km = (k * jnp.exp(b_mid - b)).astype(BF16)
    kp = (k * jnp.exp(b_last - b)).astype(BF16)
    qd = (qs * jnp.exp(b)).astype(BF16)
    vb = v.astype(BF16)

    scores = lax.dot_general(qm, km, NT_DIMS, preferred_element_type=F32)
    scores = jnp.where(tri > 0, scores, 0.0).astype(BF16)
    o_intra = jnp.dot(scores, vb, preferred_element_type=F32)

    state = state_ref[...]
    o_inter = [None] * nc
    order = range(nc - 1, -1, -1) if reverse else range(nc)
    for ci in order:
        r0 = ci * HGRN_CHUNK
        sl = slice(r0, r0 + HGRN_CHUNK)
        o_inter[ci] = lax.dot_general(qd[sl], state.astype(BF16), NT_DIMS,
                                      preferred_element_type=F32)
        dec = jnp.exp(b[r0 + last:r0 + last + 1, :])
        state = state * dec + lax.dot_general(vb[sl], kp[sl], TN_DIMS,
                                              preferred_element_type=F32)
    state_ref[...] = state
    return o_intra + jnp.concatenate(o_inter, axis=0)


def _hgrn_kernel(qf_ref, xf_ref, vf_ref, qr_ref, xr_ref, vr_ref, lbp_ref, trif_ref, trir_ref,
                 of_ref, or_ref, sf_ref, sr_ref, *, layer):
    @pl.when(pl.program_id(2) == 0)
    def _():
        sf_ref[...] = jnp.zeros_like(sf_ref)
        sr_ref[...] = jnp.zeros_like(sr_ref)

    rows = [lbp_ref[l] for l in range(DEPTH)]
    top = functools.reduce(jnp.maximum, rows)
    e = [jnp.exp(r - top) for r in rows]
    den = functools.reduce(lambda u, w: u + w, e)
    lb = jnp.zeros((2, HGRN_DIM), F32)
    for l in range(1, layer + 1):
        lb = lb + e[l] / den

    of_ref[...] = _hgrn_direction(qf_ref[...], xf_ref[...], vf_ref[...], lb[0:1], trif_ref[...],
                                  sf_ref, False)
    or_ref[...] = _hgrn_direction(qr_ref[...], xr_ref[...], vr_ref[...], lb[1:2], trir_ref[...],
                                  sr_ref, True)


def _hgrn(pa3, hgrn_lb, trif, trir, layer, tb):
    bsz, seq, _ = pa3.shape
    nb = seq // tb
    hw = HGRN_HEADS

    def fwd(col):
        return pl.BlockSpec((None, tb, HGRN_DIM), lambda b, h, i: (b, i, col * hw + h))

    def rev(col):
        return pl.BlockSpec((None, tb, HGRN_DIM), lambda b, h, i: (b, nb - 1 - i, col * hw + h))

    out_f = pl.BlockSpec((None, tb, HGRN_DIM), lambda b, h, i: (b, i, h))
    out_r = pl.BlockSpec((None, tb, HGRN_DIM), lambda b, h, i: (b, nb - 1 - i, h))
    shape = jax.ShapeDtypeStruct((bsz, seq, HGRN_WIDTH), F32)
    return pl.pallas_call(
        functools.partial(_hgrn_kernel, layer=layer),
        grid=(bsz, hw, nb),
        in_specs=[fwd(0), fwd(1), fwd(3), rev(0), rev(2), rev(3),
                  pl.BlockSpec((DEPTH, 2, HGRN_DIM), lambda b, h, i: (0, 0, h)),
                  _const_spec((tb, tb)), _const_spec((tb, tb))],
        out_specs=[out_f, out_r],
        out_shape=[shape, shape],
        scratch_shapes=[pltpu.VMEM((HGRN_DIM, HGRN_DIM), F32),
                        pltpu.VMEM((HGRN_DIM, HGRN_DIM), F32)],
        compiler_params=_params(("parallel", "parallel", "arbitrary")),
        name="hgrn2",
    )(pa3, pa3, pa3, pa3, pa3, pa3, hgrn_lb, trif, trir)


def _flash_kernel(*refs, n_maps, slots, tk):
    q_ref = refs[0]
    k_refs = refs[1:1 + n_maps]
    v_refs = refs[1 + n_maps:1 + 2 * n_maps]
    out_refs = refs[1 + 2 * n_maps:1 + 2 * n_maps + len(slots)]
    acc_ref, m_ref = refs[1 + 2 * n_maps + len(slots):]
    kv_step = pl.program_id(3)

    @pl.when(kv_step == 0)
    def _():
        acc_ref[...] = jnp.zeros_like(acc_ref)
        m_ref[...] = jnp.full_like(m_ref, NEG_BIG)

    q = q_ref[...]
    n_chunks = k_refs[0].shape[0] // tk
    for m in range(n_maps):
        def body(ci, carry, m=m):
            off = pl.multiple_of(ci * tk, tk)
            kc = k_refs[m][pl.ds(off, tk), :]
            vc = v_refs[m][pl.ds(off, tk), :]
            s = lax.dot_general(q, kc, NT_DIMS, preferred_element_type=F32)
            m_prev = m_ref[m]
            m_next = jnp.maximum(m_prev, jnp.max(s, axis=1, keepdims=True))
            p = jnp.exp(s - m_next[:, :1])
            alpha = jnp.exp(m_prev - m_next)
            acc_ref[m] = acc_ref[m] * alpha + jnp.dot(p.astype(BF16), vc,
                                                      preferred_element_type=F32)
            m_ref[m] = m_next
            return carry
        lax.fori_loop(0, n_chunks, body, 0)

    @pl.when(kv_step == pl.num_programs(3) - 1)
    def _():
        lane = lax.broadcasted_iota(jnp.int32, acc_ref.shape[1:], 1)
        for out_ref, members in zip(out_refs, slots):
            out = jnp.zeros(acc_ref.shape[1:], F32)
            for m, half in members:
                acc = acc_ref[m]
                den_lane = GQA_HEAD_DIM if half == 0 else 0
                val = acc * (1.0 / acc[:, den_lane:den_lane + 1])
                out = jnp.where((lane < GQA_HEAD_DIM) == (half == 0), val, out)
            out_ref[...] = out


def _flash(q3, kexp, vexp, n_maps, v_of_map, slots, tq, tks, tk):
    bsz, seq, qw = q3.shape
    nqb = qw // LANES
    tq = min(tq, seq)
    tks = min(tks, seq)
    tk = min(tk, tks)

    def kspec(m):
        return pl.BlockSpec((None, None, tks, LANES),
                            lambda b, j, qi, ki: (n_maps * j + m, b, ki, 0))

    def vspec(m):
        return pl.BlockSpec((None, None, tks, LANES),
                            lambda b, j, qi, ki: (v_of_map(j, m), b, ki, 0))

    qspec = pl.BlockSpec((None, tq, LANES), lambda b, j, qi, ki: (b, qi, j))
    return pl.pallas_call(
        functools.partial(_flash_kernel, n_maps=n_maps, slots=slots, tk=tk),
        grid=(bsz, nqb, seq // tq, seq // tks),
        in_specs=[qspec] + [kspec(m) for m in range(n_maps)] + [vspec(m) for m in range(n_maps)],
        out_specs=[qspec for _ in slots],
        out_shape=[jax.ShapeDtypeStruct((bsz, seq, qw), F32) for _ in slots],
        scratch_shapes=[pltpu.VMEM((n_maps, tq, LANES), F32), pltpu.VMEM((n_maps, tq, LANES), F32)],
        compiler_params=_params(("parallel", "parallel", "parallel", "arbitrary")),
        name="flash_%dmaps" % n_maps,
    )(q3, *([kexp] * n_maps), *([vexp] * n_maps))


def _value_variants(v):
    ones = jnp.ones(v.shape[:-1] + (1,), v.dtype)
    zeros = jnp.zeros(v.shape[:-1] + (LANES - GQA_HEAD_DIM - 1,), v.dtype)
    return (jnp.concatenate([v, ones, zeros], axis=-1), jnp.concatenate([ones, zeros, v], axis=-1))


def _gqa_operands(kb3, vb3):
    z = jnp.zeros(kb3.shape[:-1] + (GQA_HEAD_DIM,), kb3.dtype)
    ks, vs = [], []
    for j in range(GQA_KV_HEADS):
        sl = slice(j * GQA_HEAD_DIM, (j + 1) * GQA_HEAD_DIM)
        ks += [jnp.concatenate([kb3[..., sl], z], axis=-1), jnp.concatenate([z, kb3[..., sl]], axis=-1)]
        vs += list(_value_variants(vb3[..., sl]))
    return jnp.stack(ks), jnp.stack(vs)


def _diff_operands(kc3, vc3):
    quarter = lax.broadcasted_iota(jnp.int32, (LANES,), 0) // DIFF_HEAD_DIM
    ks, vs = [], []
    for j in range(DIFF_HEADS // 2):
        kblk = kc3[..., j * LANES:(j + 1) * LANES]
        for m in range(4):
            ks.append(jnp.where(quarter == m, kblk, jnp.zeros_like(kblk)))
        lo, _ = _value_variants(vc3[..., (2 * j) * GQA_HEAD_DIM:(2 * j + 1) * GQA_HEAD_DIM])
        _, hi = _value_variants(vc3[..., (2 * j + 1) * GQA_HEAD_DIM:(2 * j + 2) * GQA_HEAD_DIM])
        vs += [lo, hi]
    return jnp.stack(ks), jnp.stack(vs)


def _outproj_kernel(x_ref, of_ref, or_ref, ob_ref, oc1_ref, oc2_ref, sg_ref, wo_ref,
                    hnw_ref, dnw_ref, lam_ref, postw_ref, g128_ref, g64_ref, out_ref, *, lam_init):
    sg = sg_ref[...]
    a = of_ref[...] + or_ref[...]
    ms = jnp.dot((a * a).astype(BF16), g128_ref[...], preferred_element_type=F32)
    mix_a = a * lax.rsqrt(ms + NORM_EPS) * hnw_ref[...] * sg[:, :HGRN_WIDTH]
    mix_b = ob_ref[...] * sg[:, HGRN_WIDTH:HGRN_WIDTH + GQA_WIDTH]

    lp = lam_ref[...]
    lam = (jnp.exp(jnp.sum(lp[0:1] * lp[1:2], axis=-1, keepdims=True))
           - jnp.exp(jnp.sum(lp[2:3] * lp[3:4], axis=-1, keepdims=True)) + lam_init)
    c = oc1_ref[...] - lam * oc2_ref[...]
    ms = jnp.dot((c * c).astype(BF16), g64_ref[...], preferred_element_type=F32)
    mix_c = (c * lax.rsqrt(ms + NORM_EPS) * dnw_ref[...] * (1.0 - lam_init)
             * sg[:, HGRN_WIDTH + GQA_WIDTH:])

    y = (jnp.dot(mix_a.astype(BF16), wo_ref[0:HGRN_WIDTH, :], preferred_element_type=F32)
         + jnp.dot(mix_b.astype(BF16), wo_ref[HGRN_WIDTH:HGRN_WIDTH + GQA_WIDTH, :],
                   preferred_element_type=F32)
         + jnp.dot(mix_c.astype(BF16), wo_ref[HGRN_WIDTH + GQA_WIDTH:, :],
                   preferred_element_type=F32))
    out_ref[...] = x_ref[...] + (y * lax.rsqrt(jnp.mean(y * y, axis=-1, keepdims=True) + NORM_EPS)
                                 * postw_ref[...])


def _outproj(x2, of2, or2, ob2, oc1, oc2, sg, wo_bf, hnw, dnw, lam_p, postw, g128, g64, lam_init, tm):
    t = x2.shape[0]
    tok = lambda w: pl.BlockSpec((tm, w), lambda i: (i, 0))
    return pl.pallas_call(
        functools.partial(_outproj_kernel, lam_init=lam_init),
        grid=(t // tm,),
        in_specs=[tok(D_MODEL), tok(HGRN_WIDTH), tok(HGRN_WIDTH), tok(GQA_WIDTH), tok(DIFF_WIDTH),
                  tok(DIFF_WIDTH), tok(MIX_WIDTH), _const_spec((MIX_WIDTH, D_MODEL)),
                  _const_spec((1, HGRN_WIDTH)), _const_spec((1, DIFF_WIDTH)),
                  _const_spec((4, DIFF_HEAD_DIM)), _const_spec((1, D_MODEL)),
                  _const_spec((HGRN_WIDTH, HGRN_WIDTH)), _const_spec((DIFF_WIDTH, DIFF_WIDTH))],
        out_specs=tok(D_MODEL),
        out_shape=jax.ShapeDtypeStruct((t, D_MODEL), F32),
        compiler_params=_params(("parallel",)),
        name="outproj",
    )(x2, of2, or2, ob2, oc1, oc2, sg, wo_bf, hnw, dnw, lam_p, postw, g128, g64)


def _group_mean_matrix(width, group):
    idx = jnp.arange(width) // group
    return ((idx[:, None] == idx[None, :]).astype(F32) / group).astype(BF16)


def _rope_tables(seq_len):
    half = ROPE_DIM // 2
    inv = jnp.power(ROPE_THETA, -jnp.arange(0, ROPE_DIM, 2, dtype=F32) / ROPE_DIM)
    pos = jnp.arange(seq_len, dtype=F32)
    rows = seq_len // GRID_W
    row_pos = jnp.repeat(jnp.arange(rows, dtype=F32), GRID_W)
    col_pos = jnp.tile(jnp.arange(GRID_W, dtype=F32), rows)
    sign = jnp.concatenate([-jnp.ones((half,), F32), jnp.ones((half,), F32)])

    def tab(p):
        ang = p[:, None] * inv[None, :]
        ang = jnp.concatenate([ang, ang], axis=-1)
        return jnp.cos(ang), jnp.sin(ang) * sign

    cr, sr = tab(row_pos)
    cc, sc = tab(col_pos)
    c1, s1 = tab(pos)
    rep_b = GQA_WIDTH // (2 * ROPE_DIM)
    rep_c = DIFF_WIDTH // ROPE_DIM
    return (jnp.tile(jnp.concatenate([cr, cc], axis=-1), (1, rep_b)),
            jnp.tile(jnp.concatenate([sr, sc], axis=-1), (1, rep_b)),
            jnp.tile(c1, (1, rep_c)), jnp.tile(s1, (1, rep_c)))


def _chunk_tri(tb, reverse):
    r = jnp.arange(tb)
    same = (r[:, None] // HGRN_CHUNK) == (r[None, :] // HGRN_CHUNK)
    tri = (r[None, :] >= r[:, None]) if reverse else (r[None, :] <= r[:, None])
    return (same & tri).astype(BF16)


def _trunk(x, pre_norm_w, w_in_bf, hgrn_lb, hgrn_norm_w, gqa_q_norm_w, gqa_k_norm_w,
           diff_lambda, diff_norm_w, w_out_bf, post_norm_w, *, tm, tb, tq, tks, tk):
    bsz, seq, _ = x.shape
    t = bsz * seq
    tabs = _rope_tables(seq)
    g64 = _group_mean_matrix(GQA_WIDTH, GQA_HEAD_DIM)
    g128 = _group_mean_matrix(HGRN_WIDTH, HGRN_DIM)
    trif = _chunk_tri(tb, False)
    trir = _chunk_tri(tb, True)
    x2 = x.reshape(t, D_MODEL)
    for layer in range(DEPTH):
        qnw = jnp.tile(gqa_q_norm_w[layer], GQA_HEADS)[None, :]
        knw = jnp.tile(gqa_k_norm_w[layer], GQA_KV_HEADS)[None, :]
        pa, qb, kb, vb, qc, kc, vc, sg = _inproj(
            x2, seq, pre_norm_w[layer][None, :], w_in_bf[layer], tabs, qnw, knw, g64, tm)

        o_f, o_r = _hgrn(pa.reshape(bsz, seq, A_COLS), hgrn_lb, trif, trir, layer, tb)

        kexp, vexp = _gqa_operands(kb.reshape(bsz, seq, -1), vb.reshape(bsz, seq, -1))
        (ob,) = _flash(qb.reshape(bsz, seq, -1), kexp, vexp, 2, lambda j, m: 2 * j + m,
                       (((0, 0), (1, 1)),), tq, tks, tk)

        kexp, vexp = _diff_operands(kc.reshape(bsz, seq, -1), vc.reshape(bsz, seq, -1))
        oc1, oc2 = _flash(qc.reshape(bsz, seq, -1), kexp, vexp, 4, lambda j, m: 2 * j + m // 2,
                          (((0, 0), (2, 1)), ((1, 0), (3, 1))), tq, tks, tk)

        lam_init = 0.8 - 0.6 * math.exp(-0.3 * layer)
        x2 = _outproj(x2, o_f.reshape(t, -1), o_r.reshape(t, -1), ob.reshape(t, -1),
                      oc1.reshape(t, -1), oc2.reshape(t, -1), sg, w_out_bf[layer],
                      hgrn_norm_w[layer][None, :],
                      jnp.tile(diff_norm_w[layer], DIFF_HEADS)[None, :],
                      diff_lambda[layer], post_norm_w[layer][None, :], g128, g64, lam_init, tm)
    return x2.reshape(bsz, seq, D_MODEL)


def kernel(x_prompt, x_sample, pre_norm_w, w_in, hgrn_lb, hgrn_norm_w, gqa_q_norm_w, gqa_k_norm_w,
           diff_lambda, diff_norm_w, w_out, post_norm_w):
    w_in_bf = w_in.astype(BF16)
    w_out_bf = w_out.astype(BF16)
    run = functools.partial(_trunk, pre_norm_w=pre_norm_w, w_in_bf=w_in_bf, hgrn_lb=hgrn_lb,
                            hgrn_norm_w=hgrn_norm_w, gqa_q_norm_w=gqa_q_norm_w,
                            gqa_k_norm_w=gqa_k_norm_w, diff_lambda=diff_lambda,
                            diff_norm_w=diff_norm_w, w_out_bf=w_out_bf, post_norm_w=post_norm_w,
                            tm=256, tb=256, tq=512, tks=2048, tk=1024)
    return (run(x_prompt), run(x_sample))
```

```python
import functools
import math

import jax
import jax.numpy as jnp
from jax import lax
from jax.experimental import pallas as pl
from jax.experimental.pallas import tpu as pltpu

F32 = jnp.float32
BF16 = jnp.bfloat16

D_MODEL = 1024
DEPTH = 2
GRID_W = 64
HGRN_HEADS = 4
HGRN_DIM = 128
HGRN_WIDTH = HGRN_HEADS * HGRN_DIM
HGRN_CHUNK = 32
GQA_HEADS = 4
GQA_KV_HEADS = 2
GQA_HEAD_DIM = 64
GQA_WIDTH = GQA_HEADS * GQA_HEAD_DIM
GQA_KV_WIDTH = GQA_KV_HEADS * GQA_HEAD_DIM
DIFF_HEADS = 4
DIFF_HEAD_DIM = 32
DIFF_WIDTH = DIFF_HEADS * 2 * DIFF_HEAD_DIM
MIX_WIDTH = HGRN_WIDTH + GQA_WIDTH + DIFF_WIDTH
ROPE_THETA = 10000.0
ROPE_DIM = 32
NORM_EPS = 1e-6

A_COLS = 4 * HGRN_WIDTH
OFF_AG = A_COLS
OFF_B = OFF_AG + HGRN_WIDTH
OFF_BG = OFF_B + GQA_WIDTH + 2 * GQA_KV_WIDTH
OFF_C = OFF_BG + GQA_WIDTH
IN_COLS = OFF_C + 4 * DIFF_WIDTH

LANES = 128
NEG_BIG = -1e30
LOG2E = math.log2(math.e)
VMEM_LIMIT = 56 * 1024 * 1024

NT_DIMS = (((1,), (1,)), ((), ()))
TN_DIMS = (((0,), (0,)), ((), ()))


def _params(sem):
    return pltpu.CompilerParams(dimension_semantics=sem, vmem_limit_bytes=VMEM_LIMIT)


def _const_spec(shape):
    nd = len(shape)
    return pl.BlockSpec(shape, lambda *_: (0,) * nd)


def _rot_half16(x):
    outs = []
    for s in range(x.shape[1] // LANES):
        xs = x[:, s * LANES:(s + 1) * LANES]
        up = pltpu.roll(xs, ROPE_DIM // 2, 1)
        dn = pltpu.roll(xs, LANES - ROPE_DIM // 2, 1)
        lane = lax.broadcasted_iota(jnp.int32, xs.shape, 1)
        outs.append(jnp.where((lane & (ROPE_DIM - 1)) < ROPE_DIM // 2, dn, up))
    return outs[0] if len(outs) == 1 else jnp.concatenate(outs, axis=1)


def _silu(x):
    return x * (1.0 / (1.0 + jnp.exp(-x)))


def _inproj_kernel(x_ref, pw_ref, w_ref, cosb_ref, sinb_ref, cosc_ref, sinc_ref,
                   qnw_ref, knw_ref, g64_ref,
                   pa_ref, qb_ref, kb_ref, vb_ref, qc_ref, kc_ref, vc_ref, sg_ref):
    x = x_ref[...]
    h = x * lax.rsqrt(jnp.mean(x * x, axis=-1, keepdims=True) + NORM_EPS) * pw_ref[...]
    hb = h.astype(BF16)

    def proj(lo, hi):
        return jnp.dot(hb, w_ref[:, lo:hi], preferred_element_type=F32)

    pa_ref[...] = proj(0, A_COLS)
    sg_ref[:, 0:HGRN_WIDTH] = _silu(proj(OFF_AG, OFF_B))

    def head_rms(t, w, g):
        ms = jnp.dot((t * t).astype(BF16), g, preferred_element_type=F32)
        return t * lax.rsqrt(ms + NORM_EPS) * w

    def rope(t, cos, sin):
        return t * cos + _rot_half16(t) * sin

    g64 = g64_ref[...]
    cosb = cosb_ref[...]
    sinb = sinb_ref[...]
    bq = head_rms(proj(OFF_B, OFF_B + GQA_WIDTH), qnw_ref[...], g64)
    qb_ref[...] = (rope(bq, cosb, sinb) * (LOG2E / math.sqrt(GQA_HEAD_DIM))).astype(BF16)
    off_k = OFF_B + GQA_WIDTH
    bk = head_rms(proj(off_k, off_k + GQA_KV_WIDTH), knw_ref[...],
                  g64[:GQA_KV_WIDTH, :GQA_KV_WIDTH])
    kb_ref[...] = rope(bk, cosb[:, :GQA_KV_WIDTH], sinb[:, :GQA_KV_WIDTH]).astype(BF16)
    off_v = off_k + GQA_KV_WIDTH
    vb_ref[...] = proj(off_v, OFF_BG).astype(BF16)
    sg_ref[:, HGRN_WIDTH:HGRN_WIDTH + GQA_WIDTH] = _silu(proj(OFF_BG, OFF_C))

    cosc = cosc_ref[...]
    sinc = sinc_ref[...]
    cq = proj(OFF_C, OFF_C + DIFF_WIDTH)
    qc_ref[...] = (rope(cq, cosc, sinc) * (LOG2E / math.sqrt(DIFF_HEAD_DIM))).astype(BF16)
    ck = proj(OFF_C + DIFF_WIDTH, OFF_C + 2 * DIFF_WIDTH)
    kc_ref[...] = rope(ck, cosc, sinc).astype(BF16)
    vc_ref[...] = proj(OFF_C + 2 * DIFF_WIDTH, OFF_C + 3 * DIFF_WIDTH).astype(BF16)
    sg_ref[:, HGRN_WIDTH + GQA_WIDTH:] = _silu(proj(OFF_C + 3 * DIFF_WIDTH, IN_COLS))


def _inproj(x2, seq_len, pre_w, w_in_bf, tabs, qnw, knw, g64, tm):
    t = x2.shape[0]
    tiles_per_seq = seq_len // tm
    tok = lambda w: pl.BlockSpec((tm, w), lambda i: (i, 0))
    tab = pl.BlockSpec((tm, GQA_WIDTH), lambda i: (i % tiles_per_seq, 0))
    out_w = (A_COLS, GQA_WIDTH, GQA_KV_WIDTH, GQA_KV_WIDTH, DIFF_WIDTH, DIFF_WIDTH, DIFF_WIDTH, MIX_WIDTH)
    out_dt = (F32, BF16, BF16, BF16, BF16, BF16, BF16, F32)
    return pl.pallas_call(
        _inproj_kernel,
        grid=(t // tm,),
        in_specs=[tok(D_MODEL), _const_spec((1, D_MODEL)), _const_spec((D_MODEL, IN_COLS)),
                  tab, tab, tab, tab,
                  _const_spec((1, GQA_WIDTH)), _const_spec((1, GQA_KV_WIDTH)),
                  _const_spec((GQA_WIDTH, GQA_WIDTH))],
        out_specs=[tok(w) for w in out_w],
        out_shape=[jax.ShapeDtypeStruct((t, w), d) for w, d in zip(out_w, out_dt)],
        compiler_params=_params(("parallel",)),
        name="inproj",
    )(x2, pre_w, w_in_bf, *tabs, qnw, knw, g64)


def _log1p(x):
    return jnp.log(1.0 + x)


def _hgrn_direction(q, xf, v, lb, tri, state_ref, reverse):
    tb = q.shape[0]
    nc = tb // HGRN_CHUNK
    log_sig = jnp.minimum(xf, 0.0) - _log1p(jnp.exp(-jnp.abs(xf)))
    c = _log1p(-lb) + log_sig
    a = jnp.log(lb)
    g = jnp.maximum(a, c) + _log1p(jnp.exp(-jnp.abs(a - c)))
    k = (1.0 - lb) * (1.0 / (1.0 + jnp.exp(xf)))
    qs = _silu(q)

    g_hi = g.astype(BF16)
    g_lo = (g - g_hi.astype(F32)).astype(BF16)
    b = (jnp.dot(tri, g_hi, preferred_element_type=F32)
         + jnp.dot(tri, g_lo, preferred_element_type=F32))

    mid = HGRN_CHUNK // 2 - 1 if reverse else HGRN_CHUNK // 2
    last = 0 if reverse else HGRN_CHUNK - 1
    b_mid, b_last = [], []
    for ci in range(nc):
        r0 = ci * HGRN_CHUNK
        b_mid.append(jnp.broadcast_to(b[r0 + mid:r0 + mid + 1, :], (HGRN_CHUNK, HGRN_DIM)))
        b_last.append(jnp.broadcast_to(b[r0 + last:r0 + last + 1, :], (HGRN_CHUNK, HGRN_DIM)))
    b_mid = jnp.concatenate(b_mid, axis=0)
    b_last = jnp.concatenate(b_last, axis=0)

    qm = (qs * jnp.exp(b - b_mid)).astype(BF16)
    km = (k * jnp.exp(b_mid - b)).astype(BF16)
    kp = (k * jnp.exp(b_last - b)).astype(BF16)
    qd = (qs * jnp.exp(b)).astype(BF16)
    vb = v.astype(BF16)

    scores = lax.dot_general(qm, km, NT_DIMS, preferred_element_type=F32)
    scores = jnp.where(tri > 0, scores, 0.0).astype(BF16)
    o_intra = jnp.dot(scores, vb, preferred_element_type=F32)

    state = state_ref[...]
    o_inter = [None] * nc
    order = range(nc - 1, -1, -1) if reverse else range(nc)
    for ci in order:
        r0 = ci * HGRN_CHUNK
        sl = slice(r0, r0 + HGRN_CHUNK)
        o_inter[ci] = lax.dot_general(qd[sl], state.astype(BF16), NT_DIMS,
                                      preferred_element_type=F32)
        dec = jnp.exp(b[r0 + last:r0 + last + 1, :])
        state = state * dec + lax.dot_general(vb[sl], kp[sl], TN_DIMS,
                                              preferred_element_type=F32)
    state_ref[...] = state
    return o_intra + jnp.concatenate(o_inter, axis=0)


def _hgrn_kernel(qf_ref, xf_ref, vf_ref, qr_ref, xr_ref, vr_ref, lbp_ref, trif_ref, trir_ref,
                 of_ref, or_ref, sf_ref, sr_ref, *, layer):
    @pl.when(pl.program_id(2) == 0)
    def _():
        sf_ref[...] = jnp.zeros_like(sf_ref)
        sr_ref[...] = jnp.zeros_like(sr_ref)

    rows = [lbp_ref[l] for l in range(DEPTH)]
    top = functools.reduce(jnp.maximum, rows)
    e = [jnp.exp(r - top) for r in rows]
    den = functools.reduce(lambda u, w: u + w, e)
    lb = jnp.zeros((2, HGRN_DIM), F32)
    for l in range(1, layer + 1):
        lb = lb + e[l] / den

    of_ref[...] = _hgrn_direction(qf_ref[...], xf_ref[...], vf_ref[...], lb[0:1], trif_ref[...],
                                  sf_ref, False)
    or_ref[...] = _hgrn_direction(qr_ref[...], xr_ref[...], vr_ref[...], lb[1:2], trir_ref[...],
                                  sr_ref, True)


def _hgrn(pa3, hgrn_lb, trif, trir, layer, tb):
    bsz, seq, _ = pa3.shape
    nb = seq // tb
    hw = HGRN_HEADS

    def fwd(col):
        return pl.BlockSpec((None, tb, HGRN_DIM), lambda b, h, i: (b, i, col * hw + h))

    def rev(col):
        return pl.BlockSpec((None, tb, HGRN_DIM), lambda b, h, i: (b, nb - 1 - i, col * hw + h))

    out_f = pl.BlockSpec((None, tb, HGRN_DIM), lambda b, h, i: (b, i, h))
    out_r = pl.BlockSpec((None, tb, HGRN_DIM), lambda b, h, i: (b, nb - 1 - i, h))
    shape = jax.ShapeDtypeStruct((bsz, seq, HGRN_WIDTH), F32)
    return pl.pallas_call(
        functools.partial(_hgrn_kernel, layer=layer),
        grid=(bsz, hw, nb),
        in_specs=[fwd(0), fwd(1), fwd(3), rev(0), rev(2), rev(3),
                  pl.BlockSpec((DEPTH, 2, HGRN_DIM), lambda b, h, i: (0, 0, h)),
                  _const_spec((tb, tb)), _const_spec((tb, tb))],
        out_specs=[out_f, out_r],
        out_shape=[shape, shape],
        scratch_shapes=[pltpu.VMEM((HGRN_DIM, HGRN_DIM), F32),
                        pltpu.VMEM((HGRN_DIM, HGRN_DIM), F32)],
        compiler_params=_params(("parallel", "parallel", "arbitrary")),
        name="hgrn2",
    )(pa3, pa3, pa3, pa3, pa3, pa3, hgrn_lb, trif, trir)


def _flash_kernel(*refs, maps, n_v, n_out, tk):
    n_maps = len(maps)
    q_ref, k_ref = refs[0], refs[1]
    v_refs = refs[2:2 + n_v]
    out_refs = refs[2 + n_v:2 + n_v + n_out]
    acc_ref, m_ref, qv_ref, s_ref, rm_ref, p_ref, al_ref = refs[2 + n_v + n_out:]
    tq = q_ref.shape[0]
    n_chunks = k_ref.shape[0] // tk
    n_slabs = tk // LANES
    lane = lax.broadcasted_iota(jnp.int32, (tq, LANES), 1)

    def stage_a(ci, slot):
        off = pl.multiple_of(jnp.minimum(ci, n_chunks - 1) * tk, tk)
        kc = k_ref[pl.ds(off, tk), :]
        for m in range(n_maps):
            s = lax.dot_general(qv_ref[m], kc, NT_DIMS, preferred_element_type=F32)
            s_ref[slot, m] = s
            top = s[:, 0:LANES]
            for c in range(1, n_slabs):
                top = jnp.maximum(top, s[:, c * LANES:(c + 1) * LANES])
            rm_ref[slot, m] = jnp.broadcast_to(jnp.max(top, axis=1, keepdims=True), (tq, LANES))

    def stage_b(slot):
        for m in range(n_maps):
            m_prev = m_ref[m]
            m_next = jnp.maximum(m_prev, rm_ref[slot, m])
            al_ref[slot, m] = jnp.exp2(m_prev - m_next)
            m_ref[m] = m_next
            for c in range(n_slabs):
                sl = slice(c * LANES, (c + 1) * LANES)
                p_ref[slot, m, :, sl] = jnp.exp2(s_ref[slot, m, :, sl] - m_next).astype(BF16)

    def stage_c(ci, slot):
        off = pl.multiple_of(jnp.maximum(ci, 0) * tk, tk)
        for m, (_, _, vi, _, _) in enumerate(maps):
            vc = v_refs[vi][pl.ds(off, tk), :]
            acc_ref[m] = acc_ref[m] * al_ref[slot, m] + jnp.dot(
                p_ref[slot, m], vc, preferred_element_type=F32)

    q = q_ref[...]
    for m, (lo, hi, _, _, _) in enumerate(maps):
        qv_ref[m] = jnp.where((lane >= lo) & (lane < hi), q, jnp.zeros_like(q))
    acc_ref[...] = jnp.zeros_like(acc_ref)
    m_ref[...] = jnp.full_like(m_ref, NEG_BIG)
    p_ref[1] = jnp.zeros(p_ref.shape[1:], BF16)
    al_ref[1] = jnp.ones(al_ref.shape[1:], F32)
    stage_a(0, 0)

    def body(t, carry):
        i = 2 * t
        stage_c(i - 1, 1)
        stage_a(i + 1, 1)
        stage_b(0)
        stage_c(i, 0)
        stage_a(i + 2, 0)
        stage_b(1)
        return carry
    lax.fori_loop(0, n_chunks // 2, body, 0)
    stage_c(n_chunks - 1, 1)

    for oi, out_ref in enumerate(out_refs):
        halves = [None, None]
        for m, (_, _, _, mo, half) in enumerate(maps):
            if mo != oi:
                continue
            acc = acc_ref[m]
            val = acc * (1.0 / acc[:, GQA_HEAD_DIM:GQA_HEAD_DIM + 1])
            halves[half] = val if half == 0 else pltpu.roll(val, GQA_HEAD_DIM, 1)
        out_ref[...] = jnp.where(lane < GQA_HEAD_DIM, halves[0], halves[1])


def _flash(q3, k4, k_index, vaug, v_indices, maps, n_out, tq, tk, name):
    bsz, seq, qw = q3.shape
    nqb = qw // LANES
    tq = min(tq, seq)
    tk = min(tk, seq // 2)
    n_maps = len(maps)
    n_v = len(v_indices(0))
    once = pl.Buffered(1)

    def kmap(b, j, qi):
        n, lb = k_index(b, j)
        return (n, b, 0, lb)

    kspec = pl.BlockSpec((None, None, seq, LANES), kmap, pipeline_mode=once)
    vspecs = [pl.BlockSpec((None, None, seq, LANES),
                           lambda b, j, qi, n=n: (v_indices(j)[n], b, 0, 0), pipeline_mode=once)
              for n in range(n_v)]
    qspec = pl.BlockSpec((None, tq, LANES), lambda b, j, qi: (b, qi, j))
    return pl.pallas_call(
        functools.partial(_flash_kernel, maps=maps, n_v=n_v, n_out=n_out, tk=tk),
        grid=(bsz, nqb, seq // tq),
        in_specs=[qspec, kspec] + vspecs,
        out_specs=[qspec] * n_out,
        out_shape=[jax.ShapeDtypeStruct((bsz, seq, qw), F32)] * n_out,
        scratch_shapes=[pltpu.VMEM((n_maps, tq, LANES), F32),
                        pltpu.VMEM((n_maps, tq, LANES), F32),
                        pltpu.VMEM((n_maps, tq, LANES), BF16),
                        pltpu.VMEM((2, n_maps, tq, tk), F32),
                        pltpu.VMEM((2, n_maps, tq, LANES), F32),
                        pltpu.VMEM((2, n_maps, tq, tk), BF16),
                        pltpu.VMEM((2, n_maps, tq, LANES), F32)],
        compiler_params=_params(("parallel", "parallel", "parallel")),
        name=name,
    )(q3, k4, *([vaug] * n_v))


def _augment_values(v3, n_heads):
    ones = jnp.ones(v3.shape[:-1] + (1,), v3.dtype)
    zeros = jnp.zeros(v3.shape[:-1] + (LANES - GQA_HEAD_DIM - 1,), v3.dtype)
    return jnp.stack([jnp.concatenate([v3[..., h * GQA_HEAD_DIM:(h + 1) * GQA_HEAD_DIM], ones, zeros],
                                      axis=-1) for h in range(n_heads)])


GQA_MAPS = ((0, GQA_HEAD_DIM, 0, 0, 0), (GQA_HEAD_DIM, LANES, 0, 0, 1))
DIFF_MAPS = tuple((m * DIFF_HEAD_DIM, (m + 1) * DIFF_HEAD_DIM, m // 2, m % 2, m // 2)
                  for m in range(4))


def _outproj_kernel(x_ref, of_ref, or_ref, ob_ref, oc1_ref, oc2_ref, sg_ref, wo_ref,
                    hnw_ref, dnw_ref, lam_ref, postw_ref, g128_ref, g64_ref, out_ref, *, lam_init):
    sg = sg_ref[...]
    a = of_ref[...] + or_ref[...]
    ms = jnp.dot((a * a).astype(BF16), g128_ref[...], preferred_element_type=F32)
    mix_a = a * lax.rsqrt(ms + NORM_EPS) * hnw_ref[...] * sg[:, :HGRN_WIDTH]
    mix_b = ob_ref[...] * sg[:, HGRN_WIDTH:HGRN_WIDTH + GQA_WIDTH]

    lp = lam_ref[...]
    lam = (jnp.exp(jnp.sum(lp[0:1] * lp[1:2], axis=-1, keepdims=True))
           - jnp.exp(jnp.sum(lp[2:3] * lp[3:4], axis=-1, keepdims=True)) + lam_init)
    c = oc1_ref[...] - lam * oc2_ref[...]
    ms = jnp.dot((c * c).astype(BF16), g64_ref[...], preferred_element_type=F32)
    mix_c = (c * lax.rsqrt(ms + NORM_EPS) * dnw_ref[...] * (1.0 - lam_init)
             * sg[:, HGRN_WIDTH + GQA_WIDTH:])

    y = (jnp.dot(mix_a.astype(BF16), wo_ref[0:HGRN_WIDTH, :], preferred_element_type=F32)
         + jnp.dot(mix_b.astype(BF16), wo_ref[HGRN_WIDTH:HGRN_WIDTH + GQA_WIDTH, :],
                   preferred_element_type=F32)
         + jnp.dot(mix_c.astype(BF16), wo_ref[HGRN_WIDTH + GQA_WIDTH:, :],
                   preferred_element_type=F32))
    out_ref[...] = x_ref[...] + (y * lax.rsqrt(jnp.mean(y * y, axis=-1, keepdims=True) + NORM_EPS)
                                 * postw_ref[...])


def _outproj(x2, of2, or2, ob2, oc1, oc2, sg, wo_bf, hnw, dnw, lam_p, postw, g128, g64, lam_init, tm):
    t = x2.shape[0]
    tok = lambda w: pl.BlockSpec((tm, w), lambda i: (i, 0))
    return pl.pallas_call(
        functools.partial(_outproj_kernel, lam_init=lam_init),
        grid=(t // tm,),
        in_specs=[tok(D_MODEL), tok(HGRN_WIDTH), tok(HGRN_WIDTH), tok(GQA_WIDTH), tok(DIFF_WIDTH),
                  tok(DIFF_WIDTH), tok(MIX_WIDTH), _const_spec((MIX_WIDTH, D_MODEL)),
                  _const_spec((1, HGRN_WIDTH)), _const_spec((1, DIFF_WIDTH)),
                  _const_spec((4, DIFF_HEAD_DIM)), _const_spec((1, D_MODEL)),
                  _const_spec((HGRN_WIDTH, HGRN_WIDTH)), _const_spec((DIFF_WIDTH, DIFF_WIDTH))],
        out_specs=tok(D_MODEL),
        out_shape=jax.ShapeDtypeStruct((t, D_MODEL), F32),
        compiler_params=_params(("parallel",)),
        name="outproj",
    )(x2, of2, or2, ob2, oc1, oc2, sg, wo_bf, hnw, dnw, lam_p, postw, g128, g64)


def _group_mean_matrix(width, group):
    idx = jnp.arange(width) // group
    return ((idx[:, None] == idx[None, :]).astype(F32) / group).astype(BF16)


def _rope_tables(seq_len):
    half = ROPE_DIM // 2
    inv = jnp.power(ROPE_THETA, -jnp.arange(0, ROPE_DIM, 2, dtype=F32) / ROPE_DIM)
    pos = jnp.arange(seq_len, dtype=F32)
    rows = seq_len // GRID_W
    row_pos = jnp.repeat(jnp.arange(rows, dtype=F32), GRID_W)
    col_pos = jnp.tile(jnp.arange(GRID_W, dtype=F32), rows)
    sign = jnp.concatenate([-jnp.ones((half,), F32), jnp.ones((half,), F32)])

    def tab(p):
        ang = p[:, None] * inv[None, :]
        ang = jnp.concatenate([ang, ang], axis=-1)
        return jnp.cos(ang), jnp.sin(ang) * sign

    cr, sr = tab(row_pos)
    cc, sc = tab(col_pos)
    c1, s1 = tab(pos)
    rep_b = GQA_WIDTH // (2 * ROPE_DIM)
    rep_c = DIFF_WIDTH // ROPE_DIM
    return (jnp.tile(jnp.concatenate([cr, cc], axis=-1), (1, rep_b)),
            jnp.tile(jnp.concatenate([sr, sc], axis=-1), (1, rep_b)),
            jnp.tile(c1, (1, rep_c)), jnp.tile(s1, (1, rep_c)))


def _chunk_tri(tb, reverse):
    r = jnp.arange(tb)
    same = (r[:, None] // HGRN_CHUNK) == (r[None, :] // HGRN_CHUNK)
    tri = (r[None, :] >= r[:, None]) if reverse else (r[None, :] <= r[:, None])
    return (same & tri).astype(BF16)


def _trunk(x, pre_norm_w, w_in_bf, hgrn_lb, hgrn_norm_w, gqa_q_norm_w, gqa_k_norm_w,
           diff_lambda, diff_norm_w, w_out_bf, post_norm_w, *, tm, tb, tq, tk):
    bsz, seq, _ = x.shape
    t = bsz * seq
    tabs = _rope_tables(seq)
    g64 = _group_mean_matrix(GQA_WIDTH, GQA_HEAD_DIM)
    g128 = _group_mean_matrix(HGRN_WIDTH, HGRN_DIM)
    trif = _chunk_tri(tb, False)
    trir = _chunk_tri(tb, True)
    x2 = x.reshape(t, D_MODEL)
    for layer in range(DEPTH):
        qnw = jnp.tile(gqa_q_norm_w[layer], GQA_HEADS)[None, :]
        knw = jnp.tile(gqa_k_norm_w[layer], GQA_KV_HEADS)[None, :]
        pa, qb, kb, vb, qc, kc, vc, sg = _inproj(
            x2, seq, pre_norm_w[layer][None, :], w_in_bf[layer], tabs, qnw, knw, g64, tm)

        o_f, o_r = _hgrn(pa.reshape(bsz, seq, A_COLS), hgrn_lb, trif, trir, layer, tb)

        kb3 = kb.reshape(bsz, seq, -1)
        kdup = jnp.stack([jnp.concatenate([kb3[..., h * GQA_HEAD_DIM:(h + 1) * GQA_HEAD_DIM]] * 2, axis=-1)
                          for h in range(GQA_KV_HEADS)])
        (ob,) = _flash(qb.reshape(bsz, seq, -1), kdup, lambda b, j: (j, 0),
                       _augment_values(vb.reshape(bsz, seq, -1), GQA_KV_HEADS), lambda j: (j,),
                       GQA_MAPS, 1, tq, tk, "flash_gqa")

        oc1, oc2 = _flash(qc.reshape(bsz, seq, -1), kc.reshape(1, bsz, seq, -1), lambda b, j: (0, j),
                          _augment_values(vc.reshape(bsz, seq, -1), DIFF_HEADS),
                          lambda j: (2 * j, 2 * j + 1), DIFF_MAPS, 2, tq, tk, "flash_diff")

        lam_init = 0.8 - 0.6 * math.exp(-0.3 * layer)
        x2 = _outproj(x2, o_f.reshape(t, -1), o_r.reshape(t, -1), ob.reshape(t, -1),
                      oc1.reshape(t, -1), oc2.reshape(t, -1), sg, w_out_bf[layer],
                      hgrn_norm_w[layer][None, :],
                      jnp.tile(diff_norm_w[layer], DIFF_HEADS)[None, :],
                      diff_lambda[layer], post_norm_w[layer][None, :], g128, g64, lam_init, tm)
    return x2.reshape(bsz, seq, D_MODEL)


def kernel(x_prompt, x_sample, pre_norm_w, w_in, hgrn_lb, hgrn_norm_w, gqa_q_norm_w, gqa_k_norm_w,
           diff_lambda, diff_norm_w, w_out, post_norm_w):
    w_in_bf = w_in.astype(BF16)
    w_out_bf = w_out.astype(BF16)
    run = functools.partial(_trunk, pre_norm_w=pre_norm_w, w_in_bf=w_in_bf, hgrn_lb=hgrn_lb,
                            hgrn_norm_w=hgrn_norm_w, gqa_q_norm_w=gqa_q_norm_w,
                            gqa_k_norm_w=gqa_k_norm_w, diff_lambda=diff_lambda,
                            diff_norm_w=diff_norm_w, w_out_bf=w_out_bf, post_norm_w=post_norm_w,
                            tm=256, tb=256, tq=512, tk=512)
    return (run(x_prompt), run(x_sample))
```

```python
import functools
import math

import jax
import jax.numpy as jnp
from jax import lax
from jax.experimental import pallas as pl
from jax.experimental.pallas import tpu as pltpu

F32 = jnp.float32
BF16 = jnp.bfloat16

D_MODEL = 1024
DEPTH = 2
GRID_W = 64
HGRN_HEADS = 4
HGRN_DIM = 128
HGRN_WIDTH = HGRN_HEADS * HGRN_DIM
HGRN_CHUNK = 32
GQA_HEADS = 4
GQA_KV_HEADS = 2
GQA_HEAD_DIM = 64
GQA_WIDTH = GQA_HEADS * GQA_HEAD_DIM
GQA_KV_WIDTH = GQA_KV_HEADS * GQA_HEAD_DIM
DIFF_HEADS = 4
DIFF_HEAD_DIM = 32
DIFF_WIDTH = DIFF_HEADS * 2 * DIFF_HEAD_DIM
MIX_WIDTH = HGRN_WIDTH + GQA_WIDTH + DIFF_WIDTH
ROPE_THETA = 10000.0
ROPE_DIM = 32
NORM_EPS = 1e-6

A_COLS = 4 * HGRN_WIDTH
OFF_AG = A_COLS
OFF_B = OFF_AG + HGRN_WIDTH
OFF_BG = OFF_B + GQA_WIDTH + 2 * GQA_KV_WIDTH
OFF_C = OFF_BG + GQA_WIDTH
IN_COLS = OFF_C + 4 * DIFF_WIDTH

LANES = 128
NEG_BIG = -1e30
LOG2E = math.log2(math.e)
VMEM_LIMIT = 56 * 1024 * 1024

FLASH_FLAGS = None
ROW_BLOCK = 64

NT_DIMS = (((1,), (1,)), ((), ()))
TN_DIMS = (((0,), (0,)), ((), ()))


def _params(sem, flags=None):
    return pltpu.CompilerParams(dimension_semantics=sem, vmem_limit_bytes=VMEM_LIMIT, flags=flags)


def _const_spec(shape):
    nd = len(shape)
    return pl.BlockSpec(shape, lambda *_: (0,) * nd)


def _rot_half16(x):
    outs = []
    for s in range(x.shape[1] // LANES):
        xs = x[:, s * LANES:(s + 1) * LANES]
        up = pltpu.roll(xs, ROPE_DIM // 2, 1)
        dn = pltpu.roll(xs, LANES - ROPE_DIM // 2, 1)
        lane = lax.broadcasted_iota(jnp.int32, xs.shape, 1)
        outs.append(jnp.where((lane & (ROPE_DIM - 1)) < ROPE_DIM // 2, dn, up))
    return outs[0] if len(outs) == 1 else jnp.concatenate(outs, axis=1)


def _silu(x):
    return x * (1.0 / (1.0 + jnp.exp(-x)))


def _inproj_kernel(x_ref, pw_ref, w_ref, cosb_ref, sinb_ref, cosc_ref, sinc_ref,
                   qnw_ref, knw_ref, g64_ref,
                   pa_ref, qb_ref, kb_ref, vb_ref, qc_ref, kc_ref, vc_ref, sg_ref):
    x = x_ref[...]
    h = x * lax.rsqrt(jnp.mean(x * x, axis=-1, keepdims=True) + NORM_EPS) * pw_ref[...]
    hb = h.astype(BF16)

    def proj(lo, hi):
        return jnp.dot(hb, w_ref[:, lo:hi], preferred_element_type=F32)

    pa_ref[...] = proj(0, A_COLS)
    sg_ref[:, 0:HGRN_WIDTH] = _silu(proj(OFF_AG, OFF_B))

    def head_rms(t, w, g):
        ms = jnp.dot((t * t).astype(BF16), g, preferred_element_type=F32)
        return t * lax.rsqrt(ms + NORM_EPS) * w

    def rope(t, cos, sin):
        return t * cos + _rot_half16(t) * sin

    g64 = g64_ref[...]
    cosb = cosb_ref[...]
    sinb = sinb_ref[...]
    bq = head_rms(proj(OFF_B, OFF_B + GQA_WIDTH), qnw_ref[...], g64)
    qb_ref[...] = (rope(bq, cosb, sinb) * (LOG2E / math.sqrt(GQA_HEAD_DIM))).astype(BF16)
    off_k = OFF_B + GQA_WIDTH
    bk = head_rms(proj(off_k, off_k + GQA_KV_WIDTH), knw_ref[...],
                  g64[:GQA_KV_WIDTH, :GQA_KV_WIDTH])
    kb_ref[...] = rope(bk, cosb[:, :GQA_KV_WIDTH], sinb[:, :GQA_KV_WIDTH]).astype(BF16)
    off_v = off_k + GQA_KV_WIDTH
    vb_ref[...] = proj(off_v, OFF_BG).astype(BF16)
    sg_ref[:, HGRN_WIDTH:HGRN_WIDTH + GQA_WIDTH] = _silu(proj(OFF_BG, OFF_C))

    cosc = cosc_ref[...]
    sinc = sinc_ref[...]
    cq = proj(OFF_C, OFF_C + DIFF_WIDTH)
    qc_ref[...] = (rope(cq, cosc, sinc) * (LOG2E / math.sqrt(DIFF_HEAD_DIM))).astype(BF16)
    ck = proj(OFF_C + DIFF_WIDTH, OFF_C + 2 * DIFF_WIDTH)
    kc_ref[...] = rope(ck, cosc, sinc).astype(BF16)
    vc_ref[...] = proj(OFF_C + 2 * DIFF_WIDTH, OFF_C + 3 * DIFF_WIDTH).astype(BF16)
    sg_ref[:, HGRN_WIDTH + GQA_WIDTH:] = _silu(proj(OFF_C + 3 * DIFF_WIDTH, IN_COLS))


def _inproj(x2, seq_len, pre_w, w_in_bf, tabs, qnw, knw, g64, tm):
    t = x2.shape[0]
    tiles_per_seq = seq_len // tm
    tok = lambda w: pl.BlockSpec((tm, w), lambda i: (i, 0))
    tab = pl.BlockSpec((tm, GQA_WIDTH), lambda i: (i % tiles_per_seq, 0))
    out_w = (A_COLS, GQA_WIDTH, GQA_KV_WIDTH, GQA_KV_WIDTH, DIFF_WIDTH, DIFF_WIDTH, DIFF_WIDTH, MIX_WIDTH)
    out_dt = (F32, BF16, BF16, BF16, BF16, BF16, BF16, F32)
    return pl.pallas_call(
        _inproj_kernel,
        grid=(t // tm,),
        in_specs=[tok(D_MODEL), _const_spec((1, D_MODEL)), _const_spec((D_MODEL, IN_COLS)),
                  tab, tab, tab, tab,
                  _const_spec((1, GQA_WIDTH)), _const_spec((1, GQA_KV_WIDTH)),
                  _const_spec((GQA_WIDTH, GQA_WIDTH))],
        out_specs=[tok(w) for w in out_w],
        out_shape=[jax.ShapeDtypeStruct((t, w), d) for w, d in zip(out_w, out_dt)],
        compiler_params=_params(("parallel",)),
        name="inproj",
    )(x2, pre_w, w_in_bf, *tabs, qnw, knw, g64)


def _log1p(x):
    return jnp.log(1.0 + x)


def _hgrn_direction(q, xf, v, lb, tri, state_ref, reverse):
    tb = q.shape[0]
    nc = tb // HGRN_CHUNK
    log_sig = jnp.minimum(xf, 0.0) - _log1p(jnp.exp(-jnp.abs(xf)))
    c = _log1p(-lb) + log_sig
    a = jnp.log(lb)
    g = jnp.maximum(a, c) + _log1p(jnp.exp(-jnp.abs(a - c)))
    k = (1.0 - lb) * (1.0 / (1.0 + jnp.exp(xf)))
    qs = _silu(q)

    g_hi = g.astype(BF16)
    g_lo = (g - g_hi.astype(F32)).astype(BF16)
    b = (jnp.dot(tri, g_hi, preferred_element_type=F32)
         + jnp.dot(tri, g_lo, preferred_element_type=F32))

    mid = HGRN_CHUNK // 2 - 1 if reverse else HGRN_CHUNK // 2
    last = 0 if reverse else HGRN_CHUNK - 1
    b_mid, b_last = [], []
    for ci in range(nc):
        r0 = ci * HGRN_CHUNK
        b_mid.append(jnp.broadcast_to(b[r0 + mid:r0 + mid + 1, :], (HGRN_CHUNK, HGRN_DIM)))
        b_last.append(jnp.broadcast_to(b[r0 + last:r0 + last + 1, :], (HGRN_CHUNK, HGRN_DIM)))
    b_mid = jnp.concatenate(b_mid, axis=0)
    b_last = jnp.concatenate(b_last, axis=0)

    qm = (qs * jnp.exp(b - b_mid)).astype(BF16)
    km = (k * jnp.exp(b_mid - b)).astype(BF16)
    kp = (k * jnp.exp(b_last - b)).astype(BF16)
    qd = (qs * jnp.exp(b)).astype(BF16)
    vb = v.astype(BF16)

    scores = lax.dot_general(qm, km, NT_DIMS, preferred_element_type=F32)
    scores = jnp.where(tri > 0, scores, 0.0).astype(BF16)
    o_intra = jnp.dot(scores, vb, preferred_element_type=F32)

    state = state_ref[...]
    o_inter = [None] * nc
    order = range(nc - 1, -1, -1) if reverse else range(nc)
    for ci in order:
        r0 = ci * HGRN_CHUNK
        sl = slice(r0, r0 + HGRN_CHUNK)
        o_inter[ci] = lax.dot_general(qd[sl], state.astype(BF16), NT_DIMS,
                                      preferred_element_type=F32)
        dec = jnp.exp(b[r0 + last:r0 + last + 1, :])
        state = state * dec + lax.dot_general(vb[sl], kp[sl], TN_DIMS,
                                              preferred_element_type=F32)
    state_ref[...] = state
    return o_intra + jnp.concatenate(o_inter, axis=0)


def _hgrn_kernel(qf_ref, xf_ref, vf_ref, qr_ref, xr_ref, vr_ref, lbp_ref, trif_ref, trir_ref,
                 of_ref, or_ref, sf_ref, sr_ref, *, layer):
    @pl.when(pl.program_id(2) == 0)
    def _():
        sf_ref[...] = jnp.zeros_like(sf_ref)
        sr_ref[...] = jnp.zeros_like(sr_ref)

    rows = [lbp_ref[l] for l in range(DEPTH)]
    top = functools.reduce(jnp.maximum, rows)
    e = [jnp.exp(r - top) for r in rows]
    den = functools.reduce(lambda u, w: u + w, e)
    lb = jnp.zeros((2, HGRN_DIM), F32)
    for l in range(1, layer + 1):
        lb = lb + e[l] / den

    of_ref[...] = _hgrn_direction(qf_ref[...], xf_ref[...], vf_ref[...], lb[0:1], trif_ref[...],
                                  sf_ref, False)
    or_ref[...] = _hgrn_direction(qr_ref[...], xr_ref[...], vr_ref[...], lb[1:2], trir_ref[...],
                                  sr_ref, True)


def _hgrn(pa3, hgrn_lb, trif, trir, layer, tb):
    bsz, seq, _ = pa3.shape
    nb = seq // tb
    hw = HGRN_HEADS

    def fwd(col):
        return pl.BlockSpec((None, tb, HGRN_DIM), lambda b, h, i: (b, i, col * hw + h))

    def rev(col):
        return pl.BlockSpec((None, tb, HGRN_DIM), lambda b, h, i: (b, nb - 1 - i, col * hw + h))

    out_f = pl.BlockSpec((None, tb, HGRN_DIM), lambda b, h, i: (b, i, h))
    out_r = pl.BlockSpec((None, tb, HGRN_DIM), lambda b, h, i: (b, nb - 1 - i, h))
    shape = jax.ShapeDtypeStruct((bsz, seq, HGRN_WIDTH), F32)
    return pl.pallas_call(
        functools.partial(_hgrn_kernel, layer=layer),
        grid=(bsz, hw, nb),
        in_specs=[fwd(0), fwd(1), fwd(3), rev(0), rev(2), rev(3),
                  pl.BlockSpec((DEPTH, 2, HGRN_DIM), lambda b, h, i: (0, 0, h)),
                  _const_spec((tb, tb)), _const_spec((tb, tb))],
        out_specs=[out_f, out_r],
        out_shape=[shape, shape],
        scratch_shapes=[pltpu.VMEM((HGRN_DIM, HGRN_DIM), F32),
                        pltpu.VMEM((HGRN_DIM, HGRN_DIM), F32)],
        compiler_params=_params(("parallel", "parallel", "arbitrary")),
        name="hgrn2",
    )(pa3, pa3, pa3, pa3, pa3, pa3, hgrn_lb, trif, trir)


def _flash_kernel(*refs, maps, n_v, n_out, tk):
    n_maps = len(maps)
    q_ref, k_ref = refs[0], refs[1]
    v_refs = refs[2:2 + n_v]
    out_refs = refs[2 + n_v:2 + n_v + n_out]
    scratch = refs[2 + n_v + n_out:]
    acc_ref, m_ref, qv_ref = scratch[:3]
    per_kind = 2 * n_maps
    s_ref, p_ref, al_ref = (
        [scratch[3 + kind * per_kind + slot * n_maps:3 + kind * per_kind + (slot + 1) * n_maps]
         for slot in range(2)] for kind in range(3))
    tq = q_ref.shape[0]
    n_chunks = k_ref.shape[0] // tk
    n_slabs = tk // LANES
    lane = lax.broadcasted_iota(jnp.int32, (tq, LANES), 1)

    def stage_a(ci, slot):
        off = pl.multiple_of(jnp.minimum(ci, n_chunks - 1) * tk, tk)
        kc = k_ref[pl.ds(off, tk), :]
        for m in range(n_maps):
            s_ref[slot][m][...] = lax.dot_general(qv_ref[m], kc, NT_DIMS,
                                                  preferred_element_type=F32)

    def stage_b(slot):
        for m in range(n_maps):
            for r0 in range(0, tq, ROW_BLOCK):
                rows = slice(r0, r0 + ROW_BLOCK)
                top = s_ref[slot][m][rows, 0:LANES]
                for c in range(1, n_slabs):
                    top = jnp.maximum(top, s_ref[slot][m][rows, c * LANES:(c + 1) * LANES])
                m_prev = m_ref[m, rows, :]
                m_next = jnp.maximum(m_prev, jnp.max(top, axis=1, keepdims=True))
                al_ref[slot][m][rows, :] = jnp.exp2(m_prev - m_next)
                m_ref[m, rows, :] = m_next
                for c in range(n_slabs):
                    sl = slice(c * LANES, (c + 1) * LANES)
                    p_ref[slot][m][rows, sl] = jnp.exp2(
                        s_ref[slot][m][rows, sl] - m_next).astype(BF16)

    def stage_c(ci, slot):
        off = pl.multiple_of(ci * tk, tk)
        for m, (_, _, vi, _, _) in enumerate(maps):
            vc = v_refs[vi][pl.ds(off, tk), :]
            acc_ref[m] = acc_ref[m] * al_ref[slot][m][...] + jnp.dot(
                p_ref[slot][m][...], vc, preferred_element_type=F32)

    q = q_ref[...]
    for m, (lo, hi, _, _, _) in enumerate(maps):
        qv_ref[m] = jnp.where((lane >= lo) & (lane < hi), q, jnp.zeros_like(q))
    acc_ref[...] = jnp.zeros_like(acc_ref)
    m_ref[...] = jnp.full_like(m_ref, NEG_BIG)
    stage_a(0, 0)

    def trip(i, cur):
        stage_a(i + 1, 1 - cur)
        stage_b(cur)
        stage_c(i, cur)

    def body(t, carry):
        trip(2 * t, 0)
        trip(2 * t + 1, 1)
        return carry
    lax.fori_loop(0, n_chunks // 2, body, 0)

    for oi, out_ref in enumerate(out_refs):
        halves = [None, None]
        for m, (_, _, _, mo, half) in enumerate(maps):
            if mo != oi:
                continue
            acc = acc_ref[m]
            val = acc * (1.0 / acc[:, GQA_HEAD_DIM:GQA_HEAD_DIM + 1])
            halves[half] = val if half == 0 else pltpu.roll(val, GQA_HEAD_DIM, 1)
        out_ref[...] = jnp.where(lane < GQA_HEAD_DIM, halves[0], halves[1])


def _flash(q3, k4, k_index, vaug, v_indices, maps, n_out, tq, tk, name):
    bsz, seq, qw = q3.shape
    nqb = qw // LANES
    tq = min(tq, seq)
    tk = min(tk, seq // 2)
    n_maps = len(maps)
    n_v = len(v_indices(0))
    per_kind = 2 * n_maps
    once = pl.Buffered(1)

    def kmap(b, j, qi):
        n, lb = k_index(b, j)
        return (n, b, 0, lb)

    kspec = pl.BlockSpec((None, None, seq, LANES), kmap, pipeline_mode=once)
    vspecs = [pl.BlockSpec((None, None, seq, LANES),
                           lambda b, j, qi, n=n: (v_indices(j)[n], b, 0, 0), pipeline_mode=once)
              for n in range(n_v)]
    qspec = pl.BlockSpec((None, tq, LANES), lambda b, j, qi: (b, qi, j))
    return pl.pallas_call(
        functools.partial(_flash_kernel, maps=maps, n_v=n_v, n_out=n_out, tk=tk),
        grid=(bsz, nqb, seq // tq),
        in_specs=[qspec, kspec] + vspecs,
        out_specs=[qspec] * n_out,
        out_shape=[jax.ShapeDtypeStruct((bsz, seq, qw), F32)] * n_out,
        scratch_shapes=([pltpu.VMEM((n_maps, tq, LANES), F32),
                         pltpu.VMEM((n_maps, tq, LANES), F32),
                         pltpu.VMEM((n_maps, tq, LANES), BF16)]
                        + [pltpu.VMEM((tq, tk), F32)] * per_kind
                        + [pltpu.VMEM((tq, tk), BF16)] * per_kind
                        + [pltpu.VMEM((tq, LANES), F32)] * per_kind),
        compiler_params=_params(("parallel", "parallel", "parallel"), FLASH_FLAGS),
        name=name,
    )(q3, k4, *([vaug] * n_v))


def _augment_values(v3, n_heads):
    ones = jnp.ones(v3.shape[:-1] + (1,), v3.dtype)
    zeros = jnp.zeros(v3.shape[:-1] + (LANES - GQA_HEAD_DIM - 1,), v3.dtype)
    return jnp.stack([jnp.concatenate([v3[..., h * GQA_HEAD_DIM:(h + 1) * GQA_HEAD_DIM], ones, zeros],
                                      axis=-1) for h in range(n_heads)])


GQA_MAPS = ((0, GQA_HEAD_DIM, 0, 0, 0), (GQA_HEAD_DIM, LANES, 0, 0, 1))
DIFF_MAPS = tuple((m * DIFF_HEAD_DIM, (m + 1) * DIFF_HEAD_DIM, m // 2, m % 2, m // 2)
                  for m in range(4))


def _outproj_kernel(x_ref, of_ref, or_ref, ob_ref, oc1_ref, oc2_ref, sg_ref, wo_ref,
                    hnw_ref, dnw_ref, lam_ref, postw_ref, g128_ref, g64_ref, out_ref, *, lam_init):
    sg = sg_ref[...]
    a = of_ref[...] + or_ref[...]
    ms = jnp.dot((a * a).astype(BF16), g128_ref[...], preferred_element_type=F32)
    mix_a = a * lax.rsqrt(ms + NORM_EPS) * hnw_ref[...] * sg[:, :HGRN_WIDTH]
    mix_b = ob_ref[...] * sg[:, HGRN_WIDTH:HGRN_WIDTH + GQA_WIDTH]

    lp = lam_ref[...]
    lam = (jnp.exp(jnp.sum(lp[0:1] * lp[1:2], axis=-1, keepdims=True))
           - jnp.exp(jnp.sum(lp[2:3] * lp[3:4], axis=-1, keepdims=True)) + lam_init)
    c = oc1_ref[...] - lam * oc2_ref[...]
    ms = jnp.dot((c * c).astype(BF16), g64_ref[...], preferred_element_type=F32)
    mix_c = (c * lax.rsqrt(ms + NORM_EPS) * dnw_ref[...] * (1.0 - lam_init)
             * sg[:, HGRN_WIDTH + GQA_WIDTH:])

    y = (jnp.dot(mix_a.astype(BF16), wo_ref[0:HGRN_WIDTH, :], preferred_element_type=F32)
         + jnp.dot(mix_b.astype(BF16), wo_ref[HGRN_WIDTH:HGRN_WIDTH + GQA_WIDTH, :],
                   preferred_element_type=F32)
         + jnp.dot(mix_c.astype(BF16), wo_ref[HGRN_WIDTH + GQA_WIDTH:, :],
                   preferred_element_type=F32))
    out_ref[...] = x_ref[...] + (y * lax.rsqrt(jnp.mean(y * y, axis=-1, keepdims=True) + NORM_EPS)
                                 * postw_ref[...])


def _outproj(x2, of2, or2, ob2, oc1, oc2, sg, wo_bf, hnw, dnw, lam_p, postw, g128, g64, lam_init, tm):
    t = x2.shape[0]
    tok = lambda w: pl.BlockSpec((tm, w), lambda i: (i, 0))
    return pl.pallas_call(
        functools.partial(_outproj_kernel, lam_init=lam_init),
        grid=(t // tm,),
        in_specs=[tok(D_MODEL), tok(HGRN_WIDTH), tok(HGRN_WIDTH), tok(GQA_WIDTH), tok(DIFF_WIDTH),
                  tok(DIFF_WIDTH), tok(MIX_WIDTH), _const_spec((MIX_WIDTH, D_MODEL)),
                  _const_spec((1, HGRN_WIDTH)), _const_spec((1, DIFF_WIDTH)),
                  _const_spec((4, DIFF_HEAD_DIM)), _const_spec((1, D_MODEL)),
                  _const_spec((HGRN_WIDTH, HGRN_WIDTH)), _const_spec((DIFF_WIDTH, DIFF_WIDTH))],
        out_specs=tok(D_MODEL),
        out_shape=jax.ShapeDtypeStruct((t, D_MODEL), F32),
        compiler_params=_params(("parallel",)),
        name="outproj",
    )(x2, of2, or2, ob2, oc1, oc2, sg, wo_bf, hnw, dnw, lam_p, postw, g128, g64)


def _group_mean_matrix(width, group):
    idx = jnp.arange(width) // group
    return ((idx[:, None] == idx[None, :]).astype(F32) / group).astype(BF16)


def _rope_tables(seq_len):
    half = ROPE_DIM // 2
    inv = jnp.power(ROPE_THETA, -jnp.arange(0, ROPE_DIM, 2, dtype=F32) / ROPE_DIM)
    pos = jnp.arange(seq_len, dtype=F32)
    rows = seq_len // GRID_W
    row_pos = jnp.repeat(jnp.arange(rows, dtype=F32), GRID_W)
    col_pos = jnp.tile(jnp.arange(GRID_W, dtype=F32), rows)
    sign = jnp.concatenate([-jnp.ones((half,), F32), jnp.ones((half,), F32)])

    def tab(p):
        ang = p[:, None] * inv[None, :]
        ang = jnp.concatenate([ang, ang], axis=-1)
        return jnp.cos(ang), jnp.sin(ang) * sign

    cr, sr = tab(row_pos)
    cc, sc = tab(col_pos)
    c1, s1 = tab(pos)
    rep_b = GQA_WIDTH // (2 * ROPE_DIM)
    rep_c = DIFF_WIDTH // ROPE_DIM
    return (jnp.tile(jnp.concatenate([cr, cc], axis=-1), (1, rep_b)),
            jnp.tile(jnp.concatenate([sr, sc], axis=-1), (1, rep_b)),
            jnp.tile(c1, (1, rep_c)), jnp.tile(s1, (1, rep_c)))


def _chunk_tri(tb, reverse):
    r = jnp.arange(tb)
    same = (r[:, None] // HGRN_CHUNK) == (r[None, :] // HGRN_CHUNK)
    tri = (r[None, :] >= r[:, None]) if reverse else (r[None, :] <= r[:, None])
    return (same & tri).astype(BF16)


def _trunk(x, pre_norm_w, w_in_bf, hgrn_lb, hgrn_norm_w, gqa_q_norm_w, gqa_k_norm_w,
           diff_lambda, diff_norm_w, w_out_bf, post_norm_w, *, tm, tb, gqa_tile, diff_tile):
    bsz, seq, _ = x.shape
    t = bsz * seq
    tabs = _rope_tables(seq)
    g64 = _group_mean_matrix(GQA_WIDTH, GQA_HEAD_DIM)
    g128 = _group_mean_matrix(HGRN_WIDTH, HGRN_DIM)
    trif = _chunk_tri(tb, False)
    trir = _chunk_tri(tb, True)
    x2 = x.reshape(t, D_MODEL)
    for layer in range(DEPTH):
        qnw = jnp.tile(gqa_q_norm_w[layer], GQA_HEADS)[None, :]
        knw = jnp.tile(gqa_k_norm_w[layer], GQA_KV_HEADS)[None, :]
        pa, qb, kb, vb, qc, kc, vc, sg = _inproj(
            x2, seq, pre_norm_w[layer][None, :], w_in_bf[layer], tabs, qnw, knw, g64, tm)

        o_f, o_r = _hgrn(pa.reshape(bsz, seq, A_COLS), hgrn_lb, trif, trir, layer, tb)

        kb3 = kb.reshape(bsz, seq, -1)
        kdup = jnp.stack([jnp.concatenate([kb3[..., h * GQA_HEAD_DIM:(h + 1) * GQA_HEAD_DIM]] * 2, axis=-1)
                          for h in range(GQA_KV_HEADS)])
        (ob,) = _flash(qb.reshape(bsz, seq, -1), kdup, lambda b, j: (j, 0),
                       _augment_values(vb.reshape(bsz, seq, -1), GQA_KV_HEADS), lambda j: (j,),
                       GQA_MAPS, 1, *gqa_tile, "flash_gqa")

        oc1, oc2 = _flash(qc.reshape(bsz, seq, -1), kc.reshape(1, bsz, seq, -1), lambda b, j: (0, j),
                          _augment_values(vc.reshape(bsz, seq, -1), DIFF_HEADS),
                          lambda j: (2 * j, 2 * j + 1), DIFF_MAPS, 2, *diff_tile, "flash_diff")

        lam_init = 0.8 - 0.6 * math.exp(-0.3 * layer)
        x2 = _outproj(x2, o_f.reshape(t, -1), o_r.reshape(t, -1), ob.reshape(t, -1),
                      oc1.reshape(t, -1), oc2.reshape(t, -1), sg, w_out_bf[layer],
                      hgrn_norm_w[layer][None, :],
                      jnp.tile(diff_norm_w[layer], DIFF_HEADS)[None, :],
                      diff_lambda[layer], post_norm_w[layer][None, :], g128, g64, lam_init, tm)
    return x2.reshape(bsz, seq, D_MODEL)


def kernel(x_prompt, x_sample, pre_norm_w, w_in, hgrn_lb, hgrn_norm_w, gqa_q_norm_w, gqa_k_norm_w,
           diff_lambda, diff_norm_w, w_out, post_norm_w):
    w_in_bf = w_in.astype(BF16)
    w_out_bf = w_out.astype(BF16)
    run = functools.partial(_trunk, pre_norm_w=pre_norm_w, w_in_bf=w_in_bf, hgrn_lb=hgrn_lb,
                            hgrn_norm_w=hgrn_norm_w, gqa_q_norm_w=gqa_q_norm_w,
                            gqa_k_norm_w=gqa_k_norm_w, diff_lambda=diff_lambda,
                            diff_norm_w=diff_norm_w, w_out_bf=w_out_bf, post_norm_w=post_norm_w,
                            tm=256, tb=256, gqa_tile=(1024, 1024), diff_tile=(512, 1024))
    return (run(x_prompt), run(x_sample))
```

```python
import functools
import math

import jax
import jax.numpy as jnp
from jax import lax
from jax.experimental import pallas as pl
from jax.experimental.pallas import tpu as pltpu

F32 = jnp.float32
BF16 = jnp.bfloat16

D_MODEL = 1024
DEPTH = 2
GRID_W = 64
HGRN_HEADS = 4
HGRN_DIM = 128
HGRN_WIDTH = HGRN_HEADS * HGRN_DIM
HGRN_CHUNK = 32
GQA_HEADS = 4
GQA_KV_HEADS = 2
GQA_HEAD_DIM = 64
GQA_WIDTH = GQA_HEADS * GQA_HEAD_DIM
GQA_KV_WIDTH = GQA_KV_HEADS * GQA_HEAD_DIM
DIFF_HEADS = 4
DIFF_HEAD_DIM = 32
DIFF_WIDTH = DIFF_HEADS * 2 * DIFF_HEAD_DIM
MIX_WIDTH = HGRN_WIDTH + GQA_WIDTH + DIFF_WIDTH
ROPE_THETA = 10000.0
ROPE_DIM = 32
NORM_EPS = 1e-6

A_COLS = 4 * HGRN_WIDTH
OFF_AG = A_COLS
OFF_B = OFF_AG + HGRN_WIDTH
OFF_BG = OFF_B + GQA_WIDTH + 2 * GQA_KV_WIDTH
OFF_C = OFF_BG + GQA_WIDTH
IN_COLS = OFF_C + 4 * DIFF_WIDTH

LANES = 128
NEG_BIG = -1e30
LOG2E = math.log2(math.e)
VMEM_LIMIT = 56 * 1024 * 1024

SUBLANES = 8
VALUE_ROWS = 80
COL_BLOCK = 128
CHUNKS_PER_TRIP = 4
MAX_LAG_LOG2 = 100.0

NT_DIMS = (((1,), (1,)), ((), ()))
TN_DIMS = (((0,), (0,)), ((), ()))


def _params(sem, flags=None):
    return pltpu.CompilerParams(dimension_semantics=sem, vmem_limit_bytes=VMEM_LIMIT, flags=flags)


def _const_spec(shape):
    nd = len(shape)
    return pl.BlockSpec(shape, lambda *_: (0,) * nd)


def _rot_half16(x):
    outs = []
    for s in range(x.shape[1] // LANES):
        xs = x[:, s * LANES:(s + 1) * LANES]
        up = pltpu.roll(xs, ROPE_DIM // 2, 1)
        dn = pltpu.roll(xs, LANES - ROPE_DIM // 2, 1)
        lane = lax.broadcasted_iota(jnp.int32, xs.shape, 1)
        outs.append(jnp.where((lane & (ROPE_DIM - 1)) < ROPE_DIM // 2, dn, up))
    return outs[0] if len(outs) == 1 else jnp.concatenate(outs, axis=1)


def _silu(x):
    return x * (1.0 / (1.0 + jnp.exp(-x)))


def _inproj_kernel(x_ref, pw_ref, w_ref, cosb_ref, sinb_ref, cosc_ref, sinc_ref,
                   qnw_ref, knw_ref, g64_ref,
                   pa_ref, qb_ref, kb_ref, vb_ref, qc_ref, kc_ref, vc_ref, sg_ref):
    x = x_ref[...]
    h = x * lax.rsqrt(jnp.mean(x * x, axis=-1, keepdims=True) + NORM_EPS) * pw_ref[...]
    hb = h.astype(BF16)

    def proj(lo, hi):
        return jnp.dot(hb, w_ref[:, lo:hi], preferred_element_type=F32)

    pa_ref[...] = proj(0, A_COLS)
    sg_ref[:, 0:HGRN_WIDTH] = _silu(proj(OFF_AG, OFF_B))

    def head_rms(t, w, g):
        ms = jnp.dot((t * t).astype(BF16), g, preferred_element_type=F32)
        return t * lax.rsqrt(ms + NORM_EPS) * w

    def rope(t, cos, sin):
        return t * cos + _rot_half16(t) * sin

    g64 = g64_ref[...]
    cosb = cosb_ref[...]
    sinb = sinb_ref[...]
    bq = head_rms(proj(OFF_B, OFF_B + GQA_WIDTH), qnw_ref[...], g64)
    qb_ref[...] = (rope(bq, cosb, sinb) * (LOG2E / math.sqrt(GQA_HEAD_DIM))).astype(BF16)
    off_k = OFF_B + GQA_WIDTH
    bk = head_rms(proj(off_k, off_k + GQA_KV_WIDTH), knw_ref[...],
                  g64[:GQA_KV_WIDTH, :GQA_KV_WIDTH])
    kb_ref[...] = rope(bk, cosb[:, :GQA_KV_WIDTH], sinb[:, :GQA_KV_WIDTH]).astype(BF16)
    off_v = off_k + GQA_KV_WIDTH
    vb_ref[...] = proj(off_v, OFF_BG).astype(BF16)
    sg_ref[:, HGRN_WIDTH:HGRN_WIDTH + GQA_WIDTH] = _silu(proj(OFF_BG, OFF_C))

    cosc = cosc_ref[...]
    sinc = sinc_ref[...]
    cq = proj(OFF_C, OFF_C + DIFF_WIDTH)
    qc_ref[...] = (rope(cq, cosc, sinc) * (LOG2E / math.sqrt(DIFF_HEAD_DIM))).astype(BF16)
    ck = proj(OFF_C + DIFF_WIDTH, OFF_C + 2 * DIFF_WIDTH)
    kc_ref[...] = rope(ck, cosc, sinc).astype(BF16)
    vc_ref[...] = proj(OFF_C + 2 * DIFF_WIDTH, OFF_C + 3 * DIFF_WIDTH).astype(BF16)
    sg_ref[:, HGRN_WIDTH + GQA_WIDTH:] = _silu(proj(OFF_C + 3 * DIFF_WIDTH, IN_COLS))


def _inproj(x2, seq_len, pre_w, w_in_bf, tabs, qnw, knw, g64, tm):
    t = x2.shape[0]
    tiles_per_seq = seq_len // tm
    tok = lambda w: pl.BlockSpec((tm, w), lambda i: (i, 0))
    tab = pl.BlockSpec((tm, GQA_WIDTH), lambda i: (i % tiles_per_seq, 0))
    out_w = (A_COLS, GQA_WIDTH, GQA_KV_WIDTH, GQA_KV_WIDTH, DIFF_WIDTH, DIFF_WIDTH, DIFF_WIDTH, MIX_WIDTH)
    out_dt = (F32, BF16, BF16, BF16, BF16, BF16, BF16, F32)
    return pl.pallas_call(
        _inproj_kernel,
        grid=(t // tm,),
        in_specs=[tok(D_MODEL), _const_spec((1, D_MODEL)), _const_spec((D_MODEL, IN_COLS)),
                  tab, tab, tab, tab,
                  _const_spec((1, GQA_WIDTH)), _const_spec((1, GQA_KV_WIDTH)),
                  _const_spec((GQA_WIDTH, GQA_WIDTH))],
        out_specs=[tok(w) for w in out_w],
        out_shape=[jax.ShapeDtypeStruct((t, w), d) for w, d in zip(out_w, out_dt)],
        compiler_params=_params(("parallel",)),
        name="inproj",
    )(x2, pre_w, w_in_bf, *tabs, qnw, knw, g64)


def _log1p(x):
    return jnp.log(1.0 + x)


def _hgrn_direction(q, xf, v, lb, tri, state_ref, reverse):
    tb = q.shape[0]
    nc = tb // HGRN_CHUNK
    log_sig = jnp.minimum(xf, 0.0) - _log1p(jnp.exp(-jnp.abs(xf)))
    c = _log1p(-lb) + log_sig
    a = jnp.log(lb)
    g = jnp.maximum(a, c) + _log1p(jnp.exp(-jnp.abs(a - c)))
    k = (1.0 - lb) * (1.0 / (1.0 + jnp.exp(xf)))
    qs = _silu(q)

    g_hi = g.astype(BF16)
    g_lo = (g - g_hi.astype(F32)).astype(BF16)
    b = (jnp.dot(tri, g_hi, preferred_element_type=F32)
         + jnp.dot(tri, g_lo, preferred_element_type=F32))

    mid = HGRN_CHUNK // 2 - 1 if reverse else HGRN_CHUNK // 2
    last = 0 if reverse else HGRN_CHUNK - 1
    b_mid, b_last = [], []
    for ci in range(nc):
        r0 = ci * HGRN_CHUNK
        b_mid.append(jnp.broadcast_to(b[r0 + mid:r0 + mid + 1, :], (HGRN_CHUNK, HGRN_DIM)))
        b_last.append(jnp.broadcast_to(b[r0 + last:r0 + last + 1, :], (HGRN_CHUNK, HGRN_DIM)))
    b_mid = jnp.concatenate(b_mid, axis=0)
    b_last = jnp.concatenate(b_last, axis=0)

    qm = (qs * jnp.exp(b - b_mid)).astype(BF16)
    km = (k * jnp.exp(b_mid - b)).astype(BF16)
    kp = (k * jnp.exp(b_last - b)).astype(BF16)
    qd = (qs * jnp.exp(b)).astype(BF16)
    vb = v.astype(BF16)

    scores = lax.dot_general(qm, km, NT_DIMS, preferred_element_type=F32)
    scores = jnp.where(tri > 0, scores, 0.0).astype(BF16)
    o_intra = jnp.dot(scores, vb, preferred_element_type=F32)

    state = state_ref[...]
    o_inter = [None] * nc
    order = range(nc - 1, -1, -1) if reverse else range(nc)
    for ci in order:
        r0 = ci * HGRN_CHUNK
        sl = slice(r0, r0 + HGRN_CHUNK)
        o_inter[ci] = lax.dot_general(qd[sl], state.astype(BF16), NT_DIMS,
                                      preferred_element_type=F32)
        dec = jnp.exp(b[r0 + last:r0 + last + 1, :])
        state = state * dec + lax.dot_general(vb[sl], kp[sl], TN_DIMS,
                                              preferred_element_type=F32)
    state_ref[...] = state
    return o_intra + jnp.concatenate(o_inter, axis=0)


def _hgrn_kernel(qf_ref, xf_ref, vf_ref, qr_ref, xr_ref, vr_ref, lbp_ref, trif_ref, trir_ref,
                 of_ref, or_ref, sf_ref, sr_ref, *, layer):
    @pl.when(pl.program_id(2) == 0)
    def _():
        sf_ref[...] = jnp.zeros_like(sf_ref)
        sr_ref[...] = jnp.zeros_like(sr_ref)

    rows = [lbp_ref[l] for l in range(DEPTH)]
    top = functools.reduce(jnp.maximum, rows)
    e = [jnp.exp(r - top) for r in rows]
    den = functools.reduce(lambda u, w: u + w, e)
    lb = jnp.zeros((2, HGRN_DIM), F32)
    for l in range(1, layer + 1):
        lb = lb + e[l] / den

    of_ref[...] = _hgrn_direction(qf_ref[...], xf_ref[...], vf_ref[...], lb[0:1], trif_ref[...],
                                  sf_ref, False)
    or_ref[...] = _hgrn_direction(qr_ref[...], xr_ref[...], vr_ref[...], lb[1:2], trir_ref[...],
                                  sr_ref, True)


def _hgrn(pa3, hgrn_lb, trif, trir, layer, tb):
    bsz, seq, _ = pa3.shape
    nb = seq // tb
    hw = HGRN_HEADS

    def fwd(col):
        return pl.BlockSpec((None, tb, HGRN_DIM), lambda b, h, i: (b, i, col * hw + h))

    def rev(col):
        return pl.BlockSpec((None, tb, HGRN_DIM), lambda b, h, i: (b, nb - 1 - i, col * hw + h))

    out_f = pl.BlockSpec((None, tb, HGRN_DIM), lambda b, h, i: (b, i, h))
    out_r = pl.BlockSpec((None, tb, HGRN_DIM), lambda b, h, i: (b, nb - 1 - i, h))
    shape = jax.ShapeDtypeStruct((bsz, seq, HGRN_WIDTH), F32)
    return pl.pallas_call(
        functools.partial(_hgrn_kernel, layer=layer),
        grid=(bsz, hw, nb),
        in_specs=[fwd(0), fwd(1), fwd(3), rev(0), rev(2), rev(3),
                  pl.BlockSpec((DEPTH, 2, HGRN_DIM), lambda b, h, i: (0, 0, h)),
                  _const_spec((tb, tb)), _const_spec((tb, tb))],
        out_specs=[out_f, out_r],
        out_shape=[shape, shape],
        scratch_shapes=[pltpu.VMEM((HGRN_DIM, HGRN_DIM), F32),
                        pltpu.VMEM((HGRN_DIM, HGRN_DIM), F32)],
        compiler_params=_params(("parallel", "parallel", "arbitrary")),
        name="hgrn2",
    )(pa3, pa3, pa3, pa3, pa3, pa3, hgrn_lb, trif, trir)


def _flash_kernel(*refs, maps, n_v, n_out, tk):
    n_maps = len(maps)
    qt_ref, k_ref = refs[0], refs[1]
    v_refs = refs[2:2 + n_v]
    out_refs = refs[2 + n_v:2 + n_v + n_out]
    scratch = refs[2 + n_v + n_out:]
    acc_ref, m_ref, gap_ref, qv_ref = scratch[:4]
    per_kind = 2 * n_maps
    s_ref, p_ref = (
        [scratch[4 + kind * per_kind + slot * n_maps:4 + kind * per_kind + (slot + 1) * n_maps]
         for slot in range(2)] for kind in range(2))
    tq = qt_ref.shape[1]
    n_chunks = k_ref.shape[0] // tk
    sub = m_ref.shape[1]

    def scores(ci, m):
        off = pl.multiple_of(ci * tk, tk)
        return jnp.dot(k_ref[pl.ds(off, tk), :], qv_ref[m], preferred_element_type=F32)

    def column_max(s):
        top = jnp.max(s.reshape(tk // sub, sub, tq), axis=0)
        return jnp.broadcast_to(jnp.max(top, axis=0, keepdims=True), (sub, tq))

    def lagged_chunk(ci, slot):
        for m, (_, _, vi, _, _) in enumerate(maps):
            shift = m_ref[m]
            s = scores(ci, m)
            p_ref[slot][m][...] = jnp.exp2(s - shift[0:1, :]).astype(BF16)
            top = column_max(s)
            gap_ref[m] = jnp.maximum(gap_ref[m], top - shift)
            m_next = jnp.maximum(shift, top)
            m_ref[m] = m_next
            acc_ref[m] = (acc_ref[m] + jnp.dot(v_refs[vi][ci], p_ref[slot][m][...],
                                               preferred_element_type=F32)
                          ) * jnp.exp2(shift - m_next)[0:1, :]

    def exact_chunk(ci, slot):
        for m, (_, _, vi, _, _) in enumerate(maps):
            s_ref[slot][m][...] = scores(ci, m)
        for m, (_, _, vi, _, _) in enumerate(maps):
            m_prev = m_ref[m]
            m_next = jnp.maximum(m_prev, column_max(s_ref[slot][m][...]))
            m_ref[m] = m_next
            for c0 in range(0, tq, COL_BLOCK):
                cols = slice(c0, c0 + COL_BLOCK)
                p_ref[slot][m][:, cols] = jnp.exp2(
                    s_ref[slot][m][:, cols] - m_next[0:1, cols]).astype(BF16)
            acc_ref[m] = acc_ref[m] * jnp.exp2(m_prev - m_next)[0:1, :] + jnp.dot(
                v_refs[vi][ci], p_ref[slot][m][...], preferred_element_type=F32)

    def key_loop(chunk_fn):
        def body(t, carry):
            for u in range(CHUNKS_PER_TRIP):
                chunk_fn(t * CHUNKS_PER_TRIP + u, u % 2)
            return carry
        lax.fori_loop(0, n_chunks // CHUNKS_PER_TRIP, body, 0)

    qt = qt_ref[...]
    row = lax.broadcasted_iota(jnp.int32, qt.shape, 0)
    for m, (lo, hi, _, _, _) in enumerate(maps):
        qv_ref[m] = jnp.where((row >= lo) & (row < hi), qt, jnp.zeros_like(qt))

    acc_ref[...] = jnp.zeros_like(acc_ref)
    gap_ref[...] = jnp.zeros_like(gap_ref)
    for m in range(n_maps):
        m_ref[m] = column_max(scores(0, m))
    key_loop(lagged_chunk)

    @pl.when(jnp.logical_not(jnp.max(gap_ref[...]) <= MAX_LAG_LOG2))
    def _():
        acc_ref[...] = jnp.zeros_like(acc_ref)
        m_ref[...] = jnp.full_like(m_ref, NEG_BIG)
        key_loop(exact_chunk)

    for oi, out_ref in enumerate(out_refs):
        for m, (_, _, _, mo, half) in enumerate(maps):
            if mo == oi:
                acc = acc_ref[m]
                out_ref[half * GQA_HEAD_DIM:(half + 1) * GQA_HEAD_DIM, :] = (
                    acc[0:GQA_HEAD_DIM, :] * (1.0 / acc[GQA_HEAD_DIM:GQA_HEAD_DIM + 1, :]))


def _flash(q3, k4, k_index, vt5, v_indices, maps, n_out, tq, tk, name):
    bsz, seq, qw = q3.shape
    nqb = qw // LANES
    n_maps = len(maps)
    n_v = len(v_indices(0))
    per_kind = 2 * n_maps
    once = pl.Buffered(1)
    qt3 = jnp.swapaxes(q3, 1, 2)

    def kmap(b, j, qi):
        n, lb = k_index(b, j)
        return (n, b, 0, lb)

    kspec = pl.BlockSpec((None, None, seq, LANES), kmap, pipeline_mode=once)
    vspecs = [pl.BlockSpec((None, None, seq // tk, VALUE_ROWS, tk),
                           lambda b, j, qi, n=n: (v_indices(j)[n], b, 0, 0, 0), pipeline_mode=once)
              for n in range(n_v)]
    qspec = pl.BlockSpec((None, LANES, tq), lambda b, j, qi: (b, j, qi))
    outs = pl.pallas_call(
        functools.partial(_flash_kernel, maps=maps, n_v=n_v, n_out=n_out, tk=tk),
        grid=(bsz, nqb, seq // tq),
        in_specs=[qspec, kspec] + vspecs,
        out_specs=[qspec] * n_out,
        out_shape=[jax.ShapeDtypeStruct((bsz, qw, seq), F32)] * n_out,
        scratch_shapes=([pltpu.VMEM((n_maps, VALUE_ROWS, tq), F32),
                         pltpu.VMEM((n_maps, SUBLANES, tq), F32),
                         pltpu.VMEM((n_maps, SUBLANES, tq), F32),
                         pltpu.VMEM((n_maps, LANES, tq), BF16)]
                        + [pltpu.VMEM((tk, tq), F32)] * per_kind
                        + [pltpu.VMEM((tk, tq), BF16)] * per_kind),
        compiler_params=_params(("parallel", "parallel", "parallel")),
        name=name,
    )(qt3, k4, *([vt5] * n_v))
    return [jnp.swapaxes(o, 1, 2) for o in outs]


def _augment_values(v3, n_heads, tk):
    bsz, seq, _ = v3.shape
    ones = jnp.ones((bsz, seq, 1), v3.dtype)
    zeros = jnp.zeros((bsz, seq, VALUE_ROWS - GQA_HEAD_DIM - 1), v3.dtype)
    heads = []
    for h in range(n_heads):
        va = jnp.concatenate([v3[..., h * GQA_HEAD_DIM:(h + 1) * GQA_HEAD_DIM], ones, zeros], axis=-1)
        heads.append(va.reshape(bsz, seq // tk, tk, VALUE_ROWS).swapaxes(2, 3))
    return jnp.stack(heads)


GQA_MAPS = ((0, GQA_HEAD_DIM, 0, 0, 0), (GQA_HEAD_DIM, LANES, 0, 0, 1))
DIFF_MAPS = tuple((m * DIFF_HEAD_DIM, (m + 1) * DIFF_HEAD_DIM, m // 2, m % 2, m // 2)
                  for m in range(4))


def _outproj_kernel(x_ref, of_ref, or_ref, ob_ref, oc1_ref, oc2_ref, sg_ref, wo_ref,
                    hnw_ref, dnw_ref, lam_ref, postw_ref, g128_ref, g64_ref, out_ref, *, lam_init):
    sg = sg_ref[...]
    a = of_ref[...] + or_ref[...]
    ms = jnp.dot((a * a).astype(BF16), g128_ref[...], preferred_element_type=F32)
    mix_a = a * lax.rsqrt(ms + NORM_EPS) * hnw_ref[...] * sg[:, :HGRN_WIDTH]
    mix_b = ob_ref[...] * sg[:, HGRN_WIDTH:HGRN_WIDTH + GQA_WIDTH]

    lp = lam_ref[...]
    lam = (jnp.exp(jnp.sum(lp[0:1] * lp[1:2], axis=-1, keepdims=True))
           - jnp.exp(jnp.sum(lp[2:3] * lp[3:4], axis=-1, keepdims=True)) + lam_init)
    c = oc1_ref[...] - lam * oc2_ref[...]
    ms = jnp.dot((c * c).astype(BF16), g64_ref[...], preferred_element_type=F32)
    mix_c = (c * lax.rsqrt(ms + NORM_EPS) * dnw_ref[...] * (1.0 - lam_init)
             * sg[:, HGRN_WIDTH + GQA_WIDTH:])

    y = (jnp.dot(mix_a.astype(BF16), wo_ref[0:HGRN_WIDTH, :], preferred_element_type=F32)
         + jnp.dot(mix_b.astype(BF16), wo_ref[HGRN_WIDTH:HGRN_WIDTH + GQA_WIDTH, :],
                   preferred_element_type=F32)
         + jnp.dot(mix_c.astype(BF16), wo_ref[HGRN_WIDTH + GQA_WIDTH:, :],
                   preferred_element_type=F32))
    out_ref[...] = x_ref[...] + (y * lax.rsqrt(jnp.mean(y * y, axis=-1, keepdims=True) + NORM_EPS)
                                 * postw_ref[...])


def _outproj(x2, of2, or2, ob2, oc1, oc2, sg, wo_bf, hnw, dnw, lam_p, postw, g128, g64, lam_init, tm):
    t = x2.shape[0]
    tok = lambda w: pl.BlockSpec((tm, w), lambda i: (i, 0))
    return pl.pallas_call(
        functools.partial(_outproj_kernel, lam_init=lam_init),
        grid=(t // tm,),
        in_specs=[tok(D_MODEL), tok(HGRN_WIDTH), tok(HGRN_WIDTH), tok(GQA_WIDTH), tok(DIFF_WIDTH),
                  tok(DIFF_WIDTH), tok(MIX_WIDTH), _const_spec((MIX_WIDTH, D_MODEL)),
                  _const_spec((1, HGRN_WIDTH)), _const_spec((1, DIFF_WIDTH)),
                  _const_spec((4, DIFF_HEAD_DIM)), _const_spec((1, D_MODEL)),
                  _const_spec((HGRN_WIDTH, HGRN_WIDTH)), _const_spec((DIFF_WIDTH, DIFF_WIDTH))],
        out_specs=tok(D_MODEL),
        out_shape=jax.ShapeDtypeStruct((t, D_MODEL), F32),
        compiler_params=_params(("parallel",)),
        name="outproj",
    )(x2, of2, or2, ob2, oc1, oc2, sg, wo_bf, hnw, dnw, lam_p, postw, g128, g64)


def _group_mean_matrix(width, group):
    idx = jnp.arange(width) // group
    return ((idx[:, None] == idx[None, :]).astype(F32) / group).astype(BF16)


def _rope_tables(seq_len):
    half = ROPE_DIM // 2
    inv = jnp.power(ROPE_THETA, -jnp.arange(0, ROPE_DIM, 2, dtype=F32) / ROPE_DIM)
    pos = jnp.arange(seq_len, dtype=F32)
    rows = seq_len // GRID_W
    row_pos = jnp.repeat(jnp.arange(rows, dtype=F32), GRID_W)
    col_pos = jnp.tile(jnp.arange(GRID_W, dtype=F32), rows)
    sign = jnp.concatenate([-jnp.ones((half,), F32), jnp.ones((half,), F32)])

    def tab(p):
        ang = p[:, None] * inv[None, :]
        ang = jnp.concatenate([ang, ang], axis=-1)
        return jnp.cos(ang), jnp.sin(ang) * sign

    cr, sr = tab(row_pos)
    cc, sc = tab(col_pos)
    c1, s1 = tab(pos)
    rep_b = GQA_WIDTH // (2 * ROPE_DIM)
    rep_c = DIFF_WIDTH // ROPE_DIM
    return (jnp.tile(jnp.concatenate([cr, cc], axis=-1), (1, rep_b)),
            jnp.tile(jnp.concatenate([sr, sc], axis=-1), (1, rep_b)),
            jnp.tile(c1, (1, rep_c)), jnp.tile(s1, (1, rep_c)))


def _chunk_tri(tb, reverse):
    r = jnp.arange(tb)
    same = (r[:, None] // HGRN_CHUNK) == (r[None, :] // HGRN_CHUNK)
    tri = (r[None, :] >= r[:, None]) if reverse else (r[None, :] <= r[:, None])
    return (same & tri).astype(BF16)


def _trunk(x, pre_norm_w, w_in_bf, hgrn_lb, hgrn_norm_w, gqa_q_norm_w, gqa_k_norm_w,
           diff_lambda, diff_norm_w, w_out_bf, post_norm_w, *, tm, tb, gqa_tile, diff_tile):
    bsz, seq, _ = x.shape
    t = bsz * seq
    tabs = _rope_tables(seq)
    g64 = _group_mean_matrix(GQA_WIDTH, GQA_HEAD_DIM)
    g128 = _group_mean_matrix(HGRN_WIDTH, HGRN_DIM)
    trif = _chunk_tri(tb, False)
    trir = _chunk_tri(tb, True)
    x2 = x.reshape(t, D_MODEL)
    for layer in range(DEPTH):
        qnw = jnp.tile(gqa_q_norm_w[layer], GQA_HEADS)[None, :]
        knw = jnp.tile(gqa_k_norm_w[layer], GQA_KV_HEADS)[None, :]
        pa, qb, kb, vb, qc, kc, vc, sg = _inproj(
            x2, seq, pre_norm_w[layer][None, :], w_in_bf[layer], tabs, qnw, knw, g64, tm)

        o_f, o_r = _hgrn(pa.reshape(bsz, seq, A_COLS), hgrn_lb, trif, trir, layer, tb)

        kb3 = kb.reshape(bsz, seq, -1)
        kdup = jnp.stack([jnp.concatenate([kb3[..., h * GQA_HEAD_DIM:(h + 1) * GQA_HEAD_DIM]] * 2, axis=-1)
                          for h in range(GQA_KV_HEADS)])
        (ob,) = _flash(qb.reshape(bsz, seq, -1), kdup, lambda b, j: (j, 0),
                       _augment_values(vb.reshape(bsz, seq, -1), GQA_KV_HEADS, gqa_tile[1]), lambda j: (j,),
                       GQA_MAPS, 1, *gqa_tile, "flash_gqa")

        oc1, oc2 = _flash(qc.reshape(bsz, seq, -1), kc.reshape(1, bsz, seq, -1), lambda b, j: (0, j),
                          _augment_values(vc.reshape(bsz, seq, -1), DIFF_HEADS, diff_tile[1]),
                          lambda j: (2 * j, 2 * j + 1), DIFF_MAPS, 2, *diff_tile, "flash_diff")

        lam_init = 0.8 - 0.6 * math.exp(-0.3 * layer)
        x2 = _outproj(x2, o_f.reshape(t, -1), o_r.reshape(t, -1), ob.reshape(t, -1),
                      oc1.reshape(t, -1), oc2.reshape(t, -1), sg, w_out_bf[layer],
                      hgrn_norm_w[layer][None, :],
                      jnp.tile(diff_norm_w[layer], DIFF_HEADS)[None, :],
                      diff_lambda[layer], post_norm_w[layer][None, :], g128, g64, lam_init, tm)
    return x2.reshape(bsz, seq, D_MODEL)


def kernel(x_prompt, x_sample, pre_norm_w, w_in, hgrn_lb, hgrn_norm_w, gqa_q_norm_w, gqa_k_norm_w,
           diff_lambda, diff_norm_w, w_out, post_norm_w):
    w_in_bf = w_in.astype(BF16)
    w_out_bf = w_out.astype(BF16)
    run = functools.partial(_trunk, pre_norm_w=pre_norm_w, w_in_bf=w_in_bf, hgrn_lb=hgrn_lb,
                            hgrn_norm_w=hgrn_norm_w, gqa_q_norm_w=gqa_q_norm_w,
                            gqa_k_norm_w=gqa_k_norm_w, diff_lambda=diff_lambda,
                            diff_norm_w=diff_norm_w, w_out_bf=w_out_bf, post_norm_w=post_norm_w,
                            tm=256, tb=256, gqa_tile=(1024, 512), diff_tile=(1024, 512))
    return (run(x_prompt), run(x_sample))
```

```python
import functools
import math

import jax
import jax.numpy as jnp
from jax import lax
from jax.experimental import pallas as pl
from jax.experimental.pallas import tpu as pltpu

F32 = jnp.float32
BF16 = jnp.bfloat16

D_MODEL = 1024
DEPTH = 2
GRID_W = 64
HGRN_HEADS = 4
HGRN_DIM = 128
HGRN_WIDTH = HGRN_HEADS * HGRN_DIM
HGRN_CHUNK = 32
GQA_HEADS = 4
GQA_KV_HEADS = 2
GQA_HEAD_DIM = 64
GQA_WIDTH = GQA_HEADS * GQA_HEAD_DIM
GQA_KV_WIDTH = GQA_KV_HEADS * GQA_HEAD_DIM
DIFF_HEADS = 4
DIFF_HEAD_DIM = 32
DIFF_WIDTH = DIFF_HEADS * 2 * DIFF_HEAD_DIM
MIX_WIDTH = HGRN_WIDTH + GQA_WIDTH + DIFF_WIDTH
ROPE_THETA = 10000.0
ROPE_DIM = 32
NORM_EPS = 1e-6

A_COLS = 4 * HGRN_WIDTH
OFF_AG = A_COLS
OFF_B = OFF_AG + HGRN_WIDTH
OFF_BG = OFF_B + GQA_WIDTH + 2 * GQA_KV_WIDTH
OFF_C = OFF_BG + GQA_WIDTH
IN_COLS = OFF_C + 4 * DIFF_WIDTH

LANES = 128
NEG_BIG = -1e30
LOG2E = math.log2(math.e)
VMEM_LIMIT = 56 * 1024 * 1024

SUBLANES = 8
VALUE_ROWS = 80
COL_BLOCK = 128
CHUNKS_PER_TRIP = 4
MAX_LAG_LOG2 = 100.0

NT_DIMS = (((1,), (1,)), ((), ()))
TN_DIMS = (((0,), (0,)), ((), ()))


def _params(sem, flags=None):
    return pltpu.CompilerParams(dimension_semantics=sem, vmem_limit_bytes=VMEM_LIMIT, flags=flags)


def _const_spec(shape):
    nd = len(shape)
    return pl.BlockSpec(shape, lambda *_: (0,) * nd)


def _rot_half16(x):
    outs = []
    for s in range(x.shape[1] // LANES):
        xs = x[:, s * LANES:(s + 1) * LANES]
        up = pltpu.roll(xs, ROPE_DIM // 2, 1)
        dn = pltpu.roll(xs, LANES - ROPE_DIM // 2, 1)
        lane = lax.broadcasted_iota(jnp.int32, xs.shape, 1)
        outs.append(jnp.where((lane & (ROPE_DIM - 1)) < ROPE_DIM // 2, dn, up))
    return outs[0] if len(outs) == 1 else jnp.concatenate(outs, axis=1)


def _silu(x):
    return x * (1.0 / (1.0 + jnp.exp(-x)))


def _inproj_kernel(x_ref, pw_ref, w_ref, cosb_ref, sinb_ref, cosc_ref, sinc_ref,
                   qnw_ref, knw_ref, g64_ref,
                   pa_ref, qb_ref, kb_ref, vb_ref, qc_ref, kc_ref, vc_ref, sg_ref):
    x = x_ref[...]
    h = x * lax.rsqrt(jnp.mean(x * x, axis=-1, keepdims=True) + NORM_EPS) * pw_ref[...]
    hb = h.astype(BF16)

    def proj(lo, hi):
        return jnp.dot(hb, w_ref[:, lo:hi], preferred_element_type=F32)

    pa_ref[...] = proj(0, A_COLS)
    sg_ref[:, 0:HGRN_WIDTH] = _silu(proj(OFF_AG, OFF_B))

    def head_rms(t, w, g):
        ms = jnp.dot((t * t).astype(BF16), g, preferred_element_type=F32)
        return t * lax.rsqrt(ms + NORM_EPS) * w

    def rope(t, cos, sin):
        return t * cos + _rot_half16(t) * sin

    g64 = g64_ref[...]
    cosb = cosb_ref[...]
    sinb = sinb_ref[...]
    bq = head_rms(proj(OFF_B, OFF_B + GQA_WIDTH), qnw_ref[...], g64)
    qb_ref[...] = (rope(bq, cosb, sinb) * (LOG2E / math.sqrt(GQA_HEAD_DIM))).astype(BF16)
    off_k = OFF_B + GQA_WIDTH
    bk = head_rms(proj(off_k, off_k + GQA_KV_WIDTH), knw_ref[...],
                  g64[:GQA_KV_WIDTH, :GQA_KV_WIDTH])
    kb_ref[...] = rope(bk, cosb[:, :GQA_KV_WIDTH], sinb[:, :GQA_KV_WIDTH]).astype(BF16)
    off_v = off_k + GQA_KV_WIDTH
    vb_ref[...] = proj(off_v, OFF_BG).astype(BF16)
    sg_ref[:, HGRN_WIDTH:HGRN_WIDTH + GQA_WIDTH] = _silu(proj(OFF_BG, OFF_C))

    cosc = cosc_ref[...]
    sinc = sinc_ref[...]
    cq = proj(OFF_C, OFF_C + DIFF_WIDTH)
    qc_ref[...] = (rope(cq, cosc, sinc) * (LOG2E / math.sqrt(DIFF_HEAD_DIM))).astype(BF16)
    ck = proj(OFF_C + DIFF_WIDTH, OFF_C + 2 * DIFF_WIDTH)
    kc_ref[...] = rope(ck, cosc, sinc).astype(BF16)
    vc_ref[...] = proj(OFF_C + 2 * DIFF_WIDTH, OFF_C + 3 * DIFF_WIDTH).astype(BF16)
    sg_ref[:, HGRN_WIDTH + GQA_WIDTH:] = _silu(proj(OFF_C + 3 * DIFF_WIDTH, IN_COLS))


def _inproj(x2, seq_len, pre_w, w_in_bf, tabs, qnw, knw, g64, tm):
    t = x2.shape[0]
    tiles_per_seq = seq_len // tm
    tok = lambda w: pl.BlockSpec((tm, w), lambda i: (i, 0))
    tab = pl.BlockSpec((tm, GQA_WIDTH), lambda i: (i % tiles_per_seq, 0))
    out_w = (A_COLS, GQA_WIDTH, GQA_KV_WIDTH, GQA_KV_WIDTH, DIFF_WIDTH, DIFF_WIDTH, DIFF_WIDTH, MIX_WIDTH)
    out_dt = (F32, BF16, BF16, BF16, BF16, BF16, BF16, F32)
    return pl.pallas_call(
        _inproj_kernel,
        grid=(t // tm,),
        in_specs=[tok(D_MODEL), _const_spec((1, D_MODEL)), _const_spec((D_MODEL, IN_COLS)),
                  tab, tab, tab, tab,
                  _const_spec((1, GQA_WIDTH)), _const_spec((1, GQA_KV_WIDTH)),
                  _const_spec((GQA_WIDTH, GQA_WIDTH))],
        out_specs=[tok(w) for w in out_w],
        out_shape=[jax.ShapeDtypeStruct((t, w), d) for w, d in zip(out_w, out_dt)],
        compiler_params=_params(("parallel",)),
        name="inproj",
    )(x2, pre_w, w_in_bf, *tabs, qnw, knw, g64)


def _log1p(x):
    return jnp.log(1.0 + x)


def _hgrn_block(q, xf, v, lb, tri, state_refs):
    tb = q.shape[0] // 2
    nc = tb // HGRN_CHUNK
    log_sig = jnp.minimum(xf, 0.0) - _log1p(jnp.exp(-jnp.abs(xf)))
    c = _log1p(-lb) + log_sig
    a = jnp.log(lb)
    g = jnp.maximum(a, c) + _log1p(jnp.exp(-jnp.abs(a - c)))
    k = (1.0 - lb) * (1.0 / (1.0 + jnp.exp(xf)))
    qs = _silu(q)

    g_hi = g.astype(BF16)
    g_lo = (g - g_hi.astype(F32)).astype(BF16)
    b2 = jnp.dot(tri, jnp.concatenate([g_hi, g_lo], axis=1), preferred_element_type=F32)
    b = b2[:, :HGRN_DIM] + b2[:, HGRN_DIM:]

    mid = (HGRN_CHUNK // 2, HGRN_CHUNK // 2 - 1)
    last = (HGRN_CHUNK - 1, 0)
    b_mid, b_last = [], []
    for d in range(2):
        for ci in range(nc):
            r0 = d * tb + ci * HGRN_CHUNK
            b_mid.append(jnp.broadcast_to(b[r0 + mid[d]:r0 + mid[d] + 1, :], (HGRN_CHUNK, HGRN_DIM)))
            b_last.append(jnp.broadcast_to(b[r0 + last[d]:r0 + last[d] + 1, :],
                                           (HGRN_CHUNK, HGRN_DIM)))
    b_mid = jnp.concatenate(b_mid, axis=0)
    b_last = jnp.concatenate(b_last, axis=0)

    qm = (qs * jnp.exp(b - b_mid)).astype(BF16)
    km = (k * jnp.exp(b_mid - b)).astype(BF16)
    kp = (k * jnp.exp(b_last - b)).astype(BF16)
    qd = (qs * jnp.exp(b)).astype(BF16)
    vb = v.astype(BF16)

    scores = lax.dot_general(qm, km, NT_DIMS, preferred_element_type=F32)
    scores = jnp.where(tri > 0, scores, 0.0).astype(BF16)
    o_intra = jnp.dot(scores, vb, preferred_element_type=F32)

    chunk_rows = [slice(r0, r0 + HGRN_CHUNK) for r0 in range(0, 2 * tb, HGRN_CHUNK)]
    d_state = [lax.dot_general(vb[sl], kp[sl], TN_DIMS, preferred_element_type=F32)
               for sl in chunk_rows]

    states = [ref[...] for ref in state_refs]
    o_inter = [None] * (2 * nc)
    for step in range(nc):
        for d in range(2):
            ci = d * nc + (step if d == 0 else nc - 1 - step)
            sl = chunk_rows[ci]
            o_inter[ci] = lax.dot_general(qd[sl], states[d].astype(BF16), NT_DIMS,
                                          preferred_element_type=F32)
            dec = jnp.exp(b[sl.start + last[d]:sl.start + last[d] + 1, :])
            states[d] = states[d] * dec + d_state[ci]
    for ref, st in zip(state_refs, states):
        ref[...] = st
    return o_intra + jnp.concatenate(o_inter, axis=0)


def _hgrn_kernel(qf_ref, xf_ref, vf_ref, qr_ref, xr_ref, vr_ref, lbp_ref, tri_ref,
                 of_ref, or_ref, sf_ref, sr_ref, *, layer):
    @pl.when(pl.program_id(2) == 0)
    def _():
        sf_ref[...] = jnp.zeros_like(sf_ref)
        sr_ref[...] = jnp.zeros_like(sr_ref)

    rows = [lbp_ref[l] for l in range(DEPTH)]
    top = functools.reduce(jnp.maximum, rows)
    e = [jnp.exp(r - top) for r in rows]
    den = functools.reduce(lambda u, w: u + w, e)
    lb = jnp.zeros((2, HGRN_DIM), F32)
    for l in range(1, layer + 1):
        lb = lb + e[l] / den

    tb = qf_ref.shape[0]
    stack = lambda f_ref, r_ref: jnp.concatenate([f_ref[...], r_ref[...]], axis=0)
    lb_rows = jnp.concatenate([jnp.broadcast_to(lb[0:1], (tb, HGRN_DIM)),
                               jnp.broadcast_to(lb[1:2], (tb, HGRN_DIM))], axis=0)
    out = _hgrn_block(stack(qf_ref, qr_ref), stack(xf_ref, xr_ref), stack(vf_ref, vr_ref),
                      lb_rows, tri_ref[...], (sf_ref, sr_ref))
    of_ref[...] = out[:tb]
    or_ref[...] = out[tb:]


def _hgrn(pa3, hgrn_lb, tri, layer, tb):
    bsz, seq, _ = pa3.shape
    nb = seq // tb
    hw = HGRN_HEADS

    def fwd(col):
        return pl.BlockSpec((None, tb, HGRN_DIM), lambda b, h, i: (b, i, col * hw + h))

    def rev(col):
        return pl.BlockSpec((None, tb, HGRN_DIM), lambda b, h, i: (b, nb - 1 - i, col * hw + h))

    out_f = pl.BlockSpec((None, tb, HGRN_DIM), lambda b, h, i: (b, i, h))
    out_r = pl.BlockSpec((None, tb, HGRN_DIM), lambda b, h, i: (b, nb - 1 - i, h))
    shape = jax.ShapeDtypeStruct((bsz, seq, HGRN_WIDTH), F32)
    return pl.pallas_call(
        functools.partial(_hgrn_kernel, layer=layer),
        grid=(bsz, hw, nb),
        in_specs=[fwd(0), fwd(1), fwd(3), rev(0), rev(2), rev(3),
                  pl.BlockSpec((DEPTH, 2, HGRN_DIM), lambda b, h, i: (0, 0, h)),
                  _const_spec((2 * tb, 2 * tb))],
        out_specs=[out_f, out_r],
        out_shape=[shape, shape],
        scratch_shapes=[pltpu.VMEM((HGRN_DIM, HGRN_DIM), F32),
                        pltpu.VMEM((HGRN_DIM, HGRN_DIM), F32)],
        compiler_params=_params(("parallel", "parallel", "arbitrary")),
        name="hgrn2",
    )(pa3, pa3, pa3, pa3, pa3, pa3, hgrn_lb, tri)


def _flash_kernel(*refs, maps, n_v, n_out, tk):
    n_maps = len(maps)
    qt_ref, k_ref = refs[0], refs[1]
    v_refs = refs[2:2 + n_v]
    out_refs = refs[2 + n_v:2 + n_v + n_out]
    scratch = refs[2 + n_v + n_out:]
    acc_ref, m_ref, gap_ref, qv_ref = scratch[:4]
    per_kind = 2 * n_maps
    s_ref, p_ref = (
        [scratch[4 + kind * per_kind + slot * n_maps:4 + kind * per_kind + (slot + 1) * n_maps]
         for slot in range(2)] for kind in range(2))
    tq = qt_ref.shape[1]
    n_chunks = k_ref.shape[0] // tk
    sub = m_ref.shape[1]

    def scores(ci, m):
        off = pl.multiple_of(ci * tk, tk)
        return jnp.dot(k_ref[pl.ds(off, tk), :], qv_ref[m], preferred_element_type=F32)

    def column_max(s):
        top = jnp.max(s.reshape(tk // sub, sub, tq), axis=0)
        return jnp.broadcast_to(jnp.max(top, axis=0, keepdims=True), (sub, tq))

    def lagged_chunk(ci, slot):
        for m, (_, _, vi, _, _) in enumerate(maps):
            shift = m_ref[m]
            s = scores(ci, m)
            p_ref[slot][m][...] = jnp.exp2(s - shift[0:1, :]).astype(BF16)
            top = column_max(s)
            gap_ref[m] = jnp.maximum(gap_ref[m], top - shift)
            m_next = jnp.maximum(shift, top)
            m_ref[m] = m_next
            acc_ref[m] = (acc_ref[m] + jnp.dot(v_refs[vi][ci], p_ref[slot][m][...],
                                               preferred_element_type=F32)
                          ) * jnp.exp2(shift - m_next)[0:1, :]

    def exact_chunk(ci, slot):
        for m, (_, _, vi, _, _) in enumerate(maps):
            s_ref[slot][m][...] = scores(ci, m)
        for m, (_, _, vi, _, _) in enumerate(maps):
            m_prev = m_ref[m]
            m_next = jnp.maximum(m_prev, column_max(s_ref[slot][m][...]))
            m_ref[m] = m_next
            for c0 in range(0, tq, COL_BLOCK):
                cols = slice(c0, c0 + COL_BLOCK)
                p_ref[slot][m][:, cols] = jnp.exp2(
                    s_ref[slot][m][:, cols] - m_next[0:1, cols]).astype(BF16)
            acc_ref[m] = acc_ref[m] * jnp.exp2(m_prev - m_next)[0:1, :] + jnp.dot(
                v_refs[vi][ci], p_ref[slot][m][...], preferred_element_type=F32)

    def key_loop(chunk_fn):
        def body(t, carry):
            for u in range(CHUNKS_PER_TRIP):
                chunk_fn(t * CHUNKS_PER_TRIP + u, u % 2)
            return carry
        lax.fori_loop(0, n_chunks // CHUNKS_PER_TRIP, body, 0)

    qt = qt_ref[...]
    row = lax.broadcasted_iota(jnp.int32, qt.shape, 0)
    for m, (lo, hi, _, _, _) in enumerate(maps):
        qv_ref[m] = jnp.where((row >= lo) & (row < hi), qt, jnp.zeros_like(qt))

    acc_ref[...] = jnp.zeros_like(acc_ref)
    gap_ref[...] = jnp.zeros_like(gap_ref)
    for m in range(n_maps):
        m_ref[m] = column_max(scores(0, m))
    key_loop(lagged_chunk)

    @pl.when(jnp.logical_not(jnp.max(gap_ref[...]) <= MAX_LAG_LOG2))
    def _():
        acc_ref[...] = jnp.zeros_like(acc_ref)
        m_ref[...] = jnp.full_like(m_ref, NEG_BIG)
        key_loop(exact_chunk)

    for oi, out_ref in enumerate(out_refs):
        for m, (_, _, _, mo, half) in enumerate(maps):
            if mo == oi:
                acc = acc_ref[m]
                out_ref[half * GQA_HEAD_DIM:(half + 1) * GQA_HEAD_DIM, :] = (
                    acc[0:GQA_HEAD_DIM, :] * (1.0 / acc[GQA_HEAD_DIM:GQA_HEAD_DIM + 1, :]))


def _flash(q3, k4, k_index, vt5, v_indices, maps, n_out, tq, tk, name):
    bsz, seq, qw = q3.shape
    nqb = qw // LANES
    n_maps = len(maps)
    n_v = len(v_indices(0))
    per_kind = 2 * n_maps
    once = pl.Buffered(1)
    qt3 = jnp.swapaxes(q3, 1, 2)

    def kmap(b, j, qi):
        n, lb = k_index(b, j)
        return (n, b, 0, lb)

    kspec = pl.BlockSpec((None, None, seq, LANES), kmap, pipeline_mode=once)
    vspecs = [pl.BlockSpec((None, None, seq // tk, VALUE_ROWS, tk),
                           lambda b, j, qi, n=n: (v_indices(j)[n], b, 0, 0, 0), pipeline_mode=once)
              for n in range(n_v)]
    qspec = pl.BlockSpec((None, LANES, tq), lambda b, j, qi: (b, j, qi))
    outs = pl.pallas_call(
        functools.partial(_flash_kernel, maps=maps, n_v=n_v, n_out=n_out, tk=tk),
        grid=(bsz, nqb, seq // tq),
        in_specs=[qspec, kspec] + vspecs,
        out_specs=[qspec] * n_out,
        out_shape=[jax.ShapeDtypeStruct((bsz, qw, seq), F32)] * n_out,
        scratch_shapes=([pltpu.VMEM((n_maps, VALUE_ROWS, tq), F32),
                         pltpu.VMEM((n_maps, SUBLANES, tq), F32),
                         pltpu.VMEM((n_maps, SUBLANES, tq), F32),
                         pltpu.VMEM((n_maps, LANES, tq), BF16)]
                        + [pltpu.VMEM((tk, tq), F32)] * per_kind
                        + [pltpu.VMEM((tk, tq), BF16)] * per_kind),
        compiler_params=_params(("parallel", "parallel", "parallel")),
        name=name,
    )(qt3, k4, *([vt5] * n_v))
    return [jnp.swapaxes(o, 1, 2) for o in outs]


def _augment_values(v3, n_heads, tk):
    bsz, seq, _ = v3.shape
    ones = jnp.ones((bsz, seq, 1), v3.dtype)
    zeros = jnp.zeros((bsz, seq, VALUE_ROWS - GQA_HEAD_DIM - 1), v3.dtype)
    heads = []
    for h in range(n_heads):
        va = jnp.concatenate([v3[..., h * GQA_HEAD_DIM:(h + 1) * GQA_HEAD_DIM], ones, zeros], axis=-1)
        heads.append(va.reshape(bsz, seq // tk, tk, VALUE_ROWS).swapaxes(2, 3))
    return jnp.stack(heads)


GQA_MAPS = ((0, GQA_HEAD_DIM, 0, 0, 0), (GQA_HEAD_DIM, LANES, 0, 0, 1))
DIFF_MAPS = tuple((m * DIFF_HEAD_DIM, (m + 1) * DIFF_HEAD_DIM, m // 2, m % 2, m // 2)
                  for m in range(4))


def _outproj_kernel(x_ref, of_ref, or_ref, ob_ref, oc1_ref, oc2_ref, sg_ref, wo_ref,
                    hnw_ref, dnw_ref, lam_ref, postw_ref, g128_ref, g64_ref, out_ref, *, lam_init):
    sg = sg_ref[...]
    a = of_ref[...] + or_ref[...]
    ms = jnp.dot((a * a).astype(BF16), g128_ref[...], preferred_element_type=F32)
    mix_a = a * lax.rsqrt(ms + NORM_EPS) * hnw_ref[...] * sg[:, :HGRN_WIDTH]
    mix_b = ob_ref[...] * sg[:, HGRN_WIDTH:HGRN_WIDTH + GQA_WIDTH]

    lp = lam_ref[...]
    lam = (jnp.exp(jnp.sum(lp[0:1] * lp[1:2], axis=-1, keepdims=True))
           - jnp.exp(jnp.sum(lp[2:3] * lp[3:4], axis=-1, keepdims=True)) + lam_init)
    c = oc1_ref[...] - lam * oc2_ref[...]
    ms = jnp.dot((c * c).astype(BF16), g64_ref[...], preferred_element_type=F32)
    mix_c = (c * lax.rsqrt(ms + NORM_EPS) * dnw_ref[...] * (1.0 - lam_init)
             * sg[:, HGRN_WIDTH + GQA_WIDTH:])

    y = (jnp.dot(mix_a.astype(BF16), wo_ref[0:HGRN_WIDTH, :], preferred_element_type=F32)
         + jnp.dot(mix_b.astype(BF16), wo_ref[HGRN_WIDTH:HGRN_WIDTH + GQA_WIDTH, :],
                   preferred_element_type=F32)
         + jnp.dot(mix_c.astype(BF16), wo_ref[HGRN_WIDTH + GQA_WIDTH:, :],
                   preferred_element_type=F32))
    out_ref[...] = x_ref[...] + (y * lax.rsqrt(jnp.mean(y * y, axis=-1, keepdims=True) + NORM_EPS)
                                 * postw_ref[...])


def _outproj(x2, of2, or2, ob2, oc1, oc2, sg, wo_bf, hnw, dnw, lam_p, postw, g128, g64, lam_init, tm):
    t = x2.shape[0]
    tok = lambda w: pl.BlockSpec((tm, w), lambda i: (i, 0))
    return pl.pallas_call(
        functools.partial(_outproj_kernel, lam_init=lam_init),
        grid=(t // tm,),
        in_specs=[tok(D_MODEL), tok(HGRN_WIDTH), tok(HGRN_WIDTH), tok(GQA_WIDTH), tok(DIFF_WIDTH),
                  tok(DIFF_WIDTH), tok(MIX_WIDTH), _const_spec((MIX_WIDTH, D_MODEL)),
                  _const_spec((1, HGRN_WIDTH)), _const_spec((1, DIFF_WIDTH)),
                  _const_spec((4, DIFF_HEAD_DIM)), _const_spec((1, D_MODEL)),
                  _const_spec((HGRN_WIDTH, HGRN_WIDTH)), _const_spec((DIFF_WIDTH, DIFF_WIDTH))],
        out_specs=tok(D_MODEL),
        out_shape=jax.ShapeDtypeStruct((t, D_MODEL), F32),
        compiler_params=_params(("parallel",)),
        name="outproj",
    )(x2, of2, or2, ob2, oc1, oc2, sg, wo_bf, hnw, dnw, lam_p, postw, g128, g64)


def _group_mean_matrix(width, group):
    idx = jnp.arange(width) // group
    return ((idx[:, None] == idx[None, :]).astype(F32) / group).astype(BF16)


def _rope_tables(seq_len):
    half = ROPE_DIM // 2
    inv = jnp.power(ROPE_THETA, -jnp.arange(0, ROPE_DIM, 2, dtype=F32) / ROPE_DIM)
    pos = jnp.arange(seq_len, dtype=F32)
    rows = seq_len // GRID_W
    row_pos = jnp.repeat(jnp.arange(rows, dtype=F32), GRID_W)
    col_pos = jnp.tile(jnp.arange(GRID_W, dtype=F32), rows)
    sign = jnp.concatenate([-jnp.ones((half,), F32), jnp.ones((half,), F32)])

    def tab(p):
        ang = p[:, None] * inv[None, :]
        ang = jnp.concatenate([ang, ang], axis=-1)
        return jnp.cos(ang), jnp.sin(ang) * sign

    cr, sr = tab(row_pos)
    cc, sc = tab(col_pos)
    c1, s1 = tab(pos)
    rep_b = GQA_WIDTH // (2 * ROPE_DIM)
    rep_c = DIFF_WIDTH // ROPE_DIM
    return (jnp.tile(jnp.concatenate([cr, cc], axis=-1), (1, rep_b)),
            jnp.tile(jnp.concatenate([sr, sc], axis=-1), (1, rep_b)),
            jnp.tile(c1, (1, rep_c)), jnp.tile(s1, (1, rep_c)))


def _chunk_tri(tb):
    r = jnp.arange(2 * tb)
    same = (r[:, None] // HGRN_CHUNK) == (r[None, :] // HGRN_CHUNK)
    tri = jnp.where(r[:, None] < tb, r[None, :] <= r[:, None], r[None, :] >= r[:, None])
    return (same & tri).astype(BF16)


def _trunk(x, pre_norm_w, w_in_bf, hgrn_lb, hgrn_norm_w, gqa_q_norm_w, gqa_k_norm_w,
           diff_lambda, diff_norm_w, w_out_bf, post_norm_w, *, tm, tb, gqa_tile, diff_tile):
    bsz, seq, _ = x.shape
    t = bsz * seq
    tabs = _rope_tables(seq)
    g64 = _group_mean_matrix(GQA_WIDTH, GQA_HEAD_DIM)
    g128 = _group_mean_matrix(HGRN_WIDTH, HGRN_DIM)
    tri = _chunk_tri(tb)
    x2 = x.reshape(t, D_MODEL)
    for layer in range(DEPTH):
        qnw = jnp.tile(gqa_q_norm_w[layer], GQA_HEADS)[None, :]
        knw = jnp.tile(gqa_k_norm_w[layer], GQA_KV_HEADS)[None, :]
        pa, qb, kb, vb, qc, kc, vc, sg = _inproj(
            x2, seq, pre_norm_w[layer][None, :], w_in_bf[layer], tabs, qnw, knw, g64, tm)

        o_f, o_r = _hgrn(pa.reshape(bsz, seq, A_COLS), hgrn_lb, tri, layer, tb)

        kb3 = kb.reshape(bsz, seq, -1)
        kdup = jnp.stack([jnp.concatenate([kb3[..., h * GQA_HEAD_DIM:(h + 1) * GQA_HEAD_DIM]] * 2, axis=-1)
                          for h in range(GQA_KV_HEADS)])
        (ob,) = _flash(qb.reshape(bsz, seq, -1), kdup, lambda b, j: (j, 0),
                       _augment_values(vb.reshape(bsz, seq, -1), GQA_KV_HEADS, gqa_tile[1]), lambda j: (j,),
                       GQA_MAPS, 1, *gqa_tile, "flash_gqa")

        oc1, oc2 = _flash(qc.reshape(bsz, seq, -1), kc.reshape(1, bsz, seq, -1), lambda b, j: (0, j),
                          _augment_values(vc.reshape(bsz, seq, -1), DIFF_HEADS, diff_tile[1]),
                          lambda j: (2 * j, 2 * j + 1), DIFF_MAPS, 2, *diff_tile, "flash_diff")

        lam_init = 0.8 - 0.6 * math.exp(-0.3 * layer)
        x2 = _outproj(x2, o_f.reshape(t, -1), o_r.reshape(t, -1), ob.reshape(t, -1),
                      oc1.reshape(t, -1), oc2.reshape(t, -1), sg, w_out_bf[layer],
                      hgrn_norm_w[layer][None, :],
                      jnp.tile(diff_norm_w[layer], DIFF_HEADS)[None, :],
                      diff_lambda[layer], post_norm_w[layer][None, :], g128, g64, lam_init, tm)
    return x2.reshape(bsz, seq, D_MODEL)


def kernel(x_prompt, x_sample, pre_norm_w, w_in, hgrn_lb, hgrn_norm_w, gqa_q_norm_w, gqa_k_norm_w,
           diff_lambda, diff_norm_w, w_out, post_norm_w):
    w_in_bf = w_in.astype(BF16)
    w_out_bf = w_out.astype(BF16)
    run = functools.partial(_trunk, pre_norm_w=pre_norm_w, w_in_bf=w_in_bf, hgrn_lb=hgrn_lb,
                            hgrn_norm_w=hgrn_norm_w, gqa_q_norm_w=gqa_q_norm_w,
                            gqa_k_norm_w=gqa_k_norm_w, diff_lambda=diff_lambda,
                            diff_norm_w=diff_norm_w, w_out_bf=w_out_bf, post_norm_w=post_norm_w,
                            tm=512, tb=256, gqa_tile=(1024, 512), diff_tile=(1024, 512))
    return (run(x_prompt), run(x_sample))
```

```python
import functools
import math

import jax
import jax.numpy as jnp
from jax import lax
from jax.experimental import pallas as pl
from jax.experimental.pallas import tpu as pltpu

F32 = jnp.float32
BF16 = jnp.bfloat16

D_MODEL = 1024
DEPTH = 2
GRID_W = 64
HGRN_HEADS = 4
HGRN_DIM = 128
HGRN_WIDTH = HGRN_HEADS * HGRN_DIM
HGRN_CHUNK = 32
GQA_HEADS = 4
GQA_KV_HEADS = 2
GQA_HEAD_DIM = 64
GQA_WIDTH = GQA_HEADS * GQA_HEAD_DIM
GQA_KV_WIDTH = GQA_KV_HEADS * GQA_HEAD_DIM
DIFF_HEADS = 4
DIFF_HEAD_DIM = 32
DIFF_WIDTH = DIFF_HEADS * 2 * DIFF_HEAD_DIM
MIX_WIDTH = HGRN_WIDTH + GQA_WIDTH + DIFF_WIDTH
ROPE_THETA = 10000.0
ROPE_DIM = 32
NORM_EPS = 1e-6

A_COLS = 4 * HGRN_WIDTH
OFF_AG = A_COLS
OFF_B = OFF_AG + HGRN_WIDTH
OFF_BG = OFF_B + GQA_WIDTH + 2 * GQA_KV_WIDTH
OFF_C = OFF_BG + GQA_WIDTH
IN_COLS = OFF_C + 4 * DIFF_WIDTH

LANES = 128
NEG_BIG = -1e30
LOG2E = math.log2(math.e)
VMEM_LIMIT = 56 * 1024 * 1024

SUBLANES = 8
VALUE_ROWS = 80
COL_BLOCK = 128
CHUNKS_PER_TRIP = 4
MAX_LAG_LOG2 = 100.0

NT_DIMS = (((1,), (1,)), ((), ()))
TN_DIMS = (((0,), (0,)), ((), ()))


def _params(sem, flags=None):
    return pltpu.CompilerParams(dimension_semantics=sem, vmem_limit_bytes=VMEM_LIMIT, flags=flags)


def _const_spec(shape):
    nd = len(shape)
    return pl.BlockSpec(shape, lambda *_: (0,) * nd)


def _rot_half16(x):
    outs = []
    for s in range(x.shape[1] // LANES):
        xs = x[:, s * LANES:(s + 1) * LANES]
        up = pltpu.roll(xs, ROPE_DIM // 2, 1)
        dn = pltpu.roll(xs, LANES - ROPE_DIM // 2, 1)
        lane = lax.broadcasted_iota(jnp.int32, xs.shape, 1)
        outs.append(jnp.where((lane & (ROPE_DIM - 1)) < ROPE_DIM // 2, dn, up))
    return outs[0] if len(outs) == 1 else jnp.concatenate(outs, axis=1)


def _silu(x):
    return x * (1.0 / (1.0 + jnp.exp(-x)))


def _inproj_kernel(x_ref, pw_ref, w_ref, cosb_ref, sinb_ref, cosc_ref, sinc_ref,
                   qnw_ref, knw_ref, g64_ref,
                   pa_ref, qb_ref, kb_ref, vb_ref, qc_ref, kc_ref, vc_ref, sg_ref):
    x = x_ref[...]
    h = x * lax.rsqrt(jnp.mean(x * x, axis=-1, keepdims=True) + NORM_EPS) * pw_ref[...]
    hb = h.astype(BF16)

    def proj(lo, hi):
        return jnp.dot(hb, w_ref[:, lo:hi], preferred_element_type=F32)

    pa_ref[...] = proj(0, A_COLS)
    sg_ref[:, 0:HGRN_WIDTH] = _silu(proj(OFF_AG, OFF_B))

    def head_rms(t, w, g):
        ms = jnp.dot((t * t).astype(BF16), g, preferred_element_type=F32)
        return t * lax.rsqrt(ms + NORM_EPS) * w

    def rope(t, cos, sin):
        return t * cos + _rot_half16(t) * sin

    g64 = g64_ref[...]
    cosb = cosb_ref[...]
    sinb = sinb_ref[...]
    bq = head_rms(proj(OFF_B, OFF_B + GQA_WIDTH), qnw_ref[...], g64)
    qb_ref[...] = (rope(bq, cosb, sinb) * (LOG2E / math.sqrt(GQA_HEAD_DIM))).astype(BF16)
    off_k = OFF_B + GQA_WIDTH
    bk = head_rms(proj(off_k, off_k + GQA_KV_WIDTH), knw_ref[...],
                  g64[:GQA_KV_WIDTH, :GQA_KV_WIDTH])
    kb_ref[...] = rope(bk, cosb[:, :GQA_KV_WIDTH], sinb[:, :GQA_KV_WIDTH]).astype(BF16)
    off_v = off_k + GQA_KV_WIDTH
    vb_ref[...] = proj(off_v, OFF_BG).astype(BF16)
    sg_ref[:, HGRN_WIDTH:HGRN_WIDTH + GQA_WIDTH] = _silu(proj(OFF_BG, OFF_C))

    cosc = cosc_ref[...]
    sinc = sinc_ref[...]
    cq = proj(OFF_C, OFF_C + DIFF_WIDTH)
    qc_ref[...] = (rope(cq, cosc, sinc) * (LOG2E / math.sqrt(DIFF_HEAD_DIM))).astype(BF16)
    ck = proj(OFF_C + DIFF_WIDTH, OFF_C + 2 * DIFF_WIDTH)
    kc_ref[...] = rope(ck, cosc, sinc).astype(BF16)
    vc_ref[...] = proj(OFF_C + 2 * DIFF_WIDTH, OFF_C + 3 * DIFF_WIDTH).astype(BF16)
    sg_ref[:, HGRN_WIDTH + GQA_WIDTH:] = _silu(proj(OFF_C + 3 * DIFF_WIDTH, IN_COLS))


def _inproj(x2, seq_len, pre_w, w_in_bf, tabs, qnw, knw, g64, tm):
    t = x2.shape[0]
    tiles_per_seq = seq_len // tm
    tok = lambda w: pl.BlockSpec((tm, w), lambda i: (i, 0))
    tab = pl.BlockSpec((tm, GQA_WIDTH), lambda i: (i % tiles_per_seq, 0))
    out_w = (A_COLS, GQA_WIDTH, GQA_KV_WIDTH, GQA_KV_WIDTH, DIFF_WIDTH, DIFF_WIDTH, DIFF_WIDTH, MIX_WIDTH)
    out_dt = (F32, BF16, BF16, BF16, BF16, BF16, BF16, F32)
    return pl.pallas_call(
        _inproj_kernel,
        grid=(t // tm,),
        in_specs=[tok(D_MODEL), _const_spec((1, D_MODEL)), _const_spec((D_MODEL, IN_COLS)),
                  tab, tab, tab, tab,
                  _const_spec((1, GQA_WIDTH)), _const_spec((1, GQA_KV_WIDTH)),
                  _const_spec((GQA_WIDTH, GQA_WIDTH))],
        out_specs=[tok(w) for w in out_w],
        out_shape=[jax.ShapeDtypeStruct((t, w), d) for w, d in zip(out_w, out_dt)],
        compiler_params=_params(("parallel",)),
        name="inproj",
    )(x2, pre_w, w_in_bf, *tabs, qnw, knw, g64)


def _log1p(x):
    return jnp.log(1.0 + x)


def _hgrn_block(q, xf, v, lb, tri, state_refs):
    tb = q.shape[0] // 2
    nc = tb // HGRN_CHUNK
    log_sig = jnp.minimum(xf, 0.0) - _log1p(jnp.exp(-jnp.abs(xf)))
    c = _log1p(-lb) + log_sig
    a = jnp.log(lb)
    g = jnp.maximum(a, c) + _log1p(jnp.exp(-jnp.abs(a - c)))
    k = (1.0 - lb) * (1.0 / (1.0 + jnp.exp(xf)))
    qs = _silu(q)

    g_hi = g.astype(BF16)
    g_lo = (g - g_hi.astype(F32)).astype(BF16)
    b2 = jnp.dot(tri, jnp.concatenate([g_hi, g_lo], axis=1), preferred_element_type=F32)
    b = b2[:, :HGRN_DIM] + b2[:, HGRN_DIM:]

    mid = (HGRN_CHUNK // 2, HGRN_CHUNK // 2 - 1)
    last = (HGRN_CHUNK - 1, 0)
    b_mid, b_last = [], []
    for d in range(2):
        for ci in range(nc):
            r0 = d * tb + ci * HGRN_CHUNK
            b_mid.append(jnp.broadcast_to(b[r0 + mid[d]:r0 + mid[d] + 1, :], (HGRN_CHUNK, HGRN_DIM)))
            b_last.append(jnp.broadcast_to(b[r0 + last[d]:r0 + last[d] + 1, :],
                                           (HGRN_CHUNK, HGRN_DIM)))
    b_mid = jnp.concatenate(b_mid, axis=0)
    b_last = jnp.concatenate(b_last, axis=0)

    qm = (qs * jnp.exp(b - b_mid)).astype(BF16)
    km = (k * jnp.exp(b_mid - b)).astype(BF16)
    kp = (k * jnp.exp(b_last - b)).astype(BF16)
    qd = (qs * jnp.exp(b)).astype(BF16)
    vb = v.astype(BF16)

    scores = lax.dot_general(qm, km, NT_DIMS, preferred_element_type=F32)
    scores = jnp.where(tri > 0, scores, 0.0).astype(BF16)
    o_intra = jnp.dot(scores, vb, preferred_element_type=F32)

    chunk_rows = [slice(r0, r0 + HGRN_CHUNK) for r0 in range(0, 2 * tb, HGRN_CHUNK)]
    d_state = [lax.dot_general(vb[sl], kp[sl], TN_DIMS, preferred_element_type=F32)
               for sl in chunk_rows]

    states = [ref[...] for ref in state_refs]
    o_inter = [None] * (2 * nc)
    for step in range(nc):
        for d in range(2):
            ci = d * nc + (step if d == 0 else nc - 1 - step)
            sl = chunk_rows[ci]
            o_inter[ci] = lax.dot_general(qd[sl], states[d].astype(BF16), NT_DIMS,
                                          preferred_element_type=F32)
            dec = jnp.exp(b[sl.start + last[d]:sl.start + last[d] + 1, :])
            states[d] = states[d] * dec + d_state[ci]
    for ref, st in zip(state_refs, states):
        ref[...] = st
    return o_intra + jnp.concatenate(o_inter, axis=0)


def _hgrn_kernel(qf_ref, xf_ref, vf_ref, qr_ref, xr_ref, vr_ref, lbp_ref, tri_ref,
                 of_ref, or_ref, sf_ref, sr_ref, *, layer):
    @pl.when(pl.program_id(2) == 0)
    def _():
        sf_ref[...] = jnp.zeros_like(sf_ref)
        sr_ref[...] = jnp.zeros_like(sr_ref)

    rows = [lbp_ref[l] for l in range(DEPTH)]
    top = functools.reduce(jnp.maximum, rows)
    e = [jnp.exp(r - top) for r in rows]
    den = functools.reduce(lambda u, w: u + w, e)
    lb = jnp.zeros((2, HGRN_DIM), F32)
    for l in range(1, layer + 1):
        lb = lb + e[l] / den

    tb = qf_ref.shape[0]
    stack = lambda f_ref, r_ref: jnp.concatenate([f_ref[...], r_ref[...]], axis=0)
    lb_rows = jnp.concatenate([jnp.broadcast_to(lb[0:1], (tb, HGRN_DIM)),
                               jnp.broadcast_to(lb[1:2], (tb, HGRN_DIM))], axis=0)
    out = _hgrn_block(stack(qf_ref, qr_ref), stack(xf_ref, xr_ref), stack(vf_ref, vr_ref),
                      lb_rows, tri_ref[...], (sf_ref, sr_ref))
    of_ref[...] = out[:tb]
    or_ref[...] = out[tb:]


def _hgrn(pa3, hgrn_lb, tri, layer, tb):
    bsz, seq, _ = pa3.shape
    nb = seq // tb
    hw = HGRN_HEADS

    def fwd(col):
        return pl.BlockSpec((None, tb, HGRN_DIM), lambda b, h, i: (b, i, col * hw + h))

    def rev(col):
        return pl.BlockSpec((None, tb, HGRN_DIM), lambda b, h, i: (b, nb - 1 - i, col * hw + h))

    out_f = pl.BlockSpec((None, tb, HGRN_DIM), lambda b, h, i: (b, i, h))
    out_r = pl.BlockSpec((None, tb, HGRN_DIM), lambda b, h, i: (b, nb - 1 - i, h))
    shape = jax.ShapeDtypeStruct((bsz, seq, HGRN_WIDTH), F32)
    return pl.pallas_call(
        functools.partial(_hgrn_kernel, layer=layer),
        grid=(bsz, hw, nb),
        in_specs=[fwd(0), fwd(1), fwd(3), rev(0), rev(2), rev(3),
                  pl.BlockSpec((DEPTH, 2, HGRN_DIM), lambda b, h, i: (0, 0, h)),
                  _const_spec((2 * tb, 2 * tb))],
        out_specs=[out_f, out_r],
        out_shape=[shape, shape],
        scratch_shapes=[pltpu.VMEM((HGRN_DIM, HGRN_DIM), F32),
                        pltpu.VMEM((HGRN_DIM, HGRN_DIM), F32)],
        compiler_params=_params(("parallel", "parallel", "arbitrary")),
        name="hgrn2",
    )(pa3, pa3, pa3, pa3, pa3, pa3, hgrn_lb, tri)


def _flash_kernel(*refs, maps, n_v, n_out, tk):
    n_maps = len(maps)
    qt_ref, k_ref = refs[0], refs[1]
    v_refs = refs[2:2 + n_v]
    out_refs = refs[2 + n_v:2 + n_v + n_out]
    scratch = refs[2 + n_v + n_out:]
    acc_ref, m_ref, gap_ref, qv_ref = scratch[:4]
    per_kind = 2 * n_maps
    s_ref, p_ref = (
        [scratch[4 + kind * per_kind + slot * n_maps:4 + kind * per_kind + (slot + 1) * n_maps]
         for slot in range(2)] for kind in range(2))
    tq = qt_ref.shape[1]
    n_chunks = k_ref.shape[0] // tk
    sub = m_ref.shape[1]

    def scores(ci, m):
        off = pl.multiple_of(ci * tk, tk)
        return jnp.dot(k_ref[pl.ds(off, tk), :], qv_ref[m], preferred_element_type=F32)

    def column_max(s):
        top = jnp.max(s.reshape(tk // sub, sub, tq), axis=0)
        return jnp.broadcast_to(jnp.max(top, axis=0, keepdims=True), (sub, tq))

    def lagged_chunk(ci, slot):
        for m, (_, _, vi, _, _) in enumerate(maps):
            shift = m_ref[m]
            s = scores(ci, m)
            p_ref[slot][m][...] = jnp.exp2((s - shift[0:1, :]).astype(BF16))
            top = column_max(s)
            gap_ref[m] = jnp.maximum(gap_ref[m], top - shift)
            m_next = jnp.maximum(shift, top)
            m_ref[m] = m_next
            acc_ref[m] = (acc_ref[m] + jnp.dot(v_refs[vi][ci], p_ref[slot][m][...],
                                               preferred_element_type=F32)
                          ) * jnp.exp2(shift - m_next)[0:1, :]

    def exact_chunk(ci, slot):
        for m, (_, _, vi, _, _) in enumerate(maps):
            s_ref[slot][m][...] = scores(ci, m)
        for m, (_, _, vi, _, _) in enumerate(maps):
            m_prev = m_ref[m]
            m_next = jnp.maximum(m_prev, column_max(s_ref[slot][m][...]))
            m_ref[m] = m_next
            for c0 in range(0, tq, COL_BLOCK):
                cols = slice(c0, c0 + COL_BLOCK)
                p_ref[slot][m][:, cols] = jnp.exp2(
                    s_ref[slot][m][:, cols] - m_next[0:1, cols]).astype(BF16)
            acc_ref[m] = acc_ref[m] * jnp.exp2(m_prev - m_next)[0:1, :] + jnp.dot(
                v_refs[vi][ci], p_ref[slot][m][...], preferred_element_type=F32)

    def key_loop(chunk_fn):
        def body(t, carry):
            for u in range(CHUNKS_PER_TRIP):
                chunk_fn(t * CHUNKS_PER_TRIP + u, u % 2)
            return carry
        lax.fori_loop(0, n_chunks // CHUNKS_PER_TRIP, body, 0)

    qt = qt_ref[...]
    row = lax.broadcasted_iota(jnp.int32, qt.shape, 0)
    for m, (lo, hi, _, _, _) in enumerate(maps):
        qv_ref[m] = jnp.where((row >= lo) & (row < hi), qt, jnp.zeros_like(qt))

    acc_ref[...] = jnp.zeros_like(acc_ref)
    gap_ref[...] = jnp.zeros_like(gap_ref)
    for m in range(n_maps):
        m_ref[m] = column_max(scores(0, m))
    key_loop(lagged_chunk)

    @pl.when(jnp.logical_not(jnp.max(gap_ref[...]) <= MAX_LAG_LOG2))
    def _():
        acc_ref[...] = jnp.zeros_like(acc_ref)
        m_ref[...] = jnp.full_like(m_ref, NEG_BIG)
        key_loop(exact_chunk)

    for oi, out_ref in enumerate(out_refs):
        for m, (_, _, _, mo, half) in enumerate(maps):
            if mo == oi:
                acc = acc_ref[m]
                out_ref[half * GQA_HEAD_DIM:(half + 1) * GQA_HEAD_DIM, :] = (
                    acc[0:GQA_HEAD_DIM, :] * (1.0 / acc[GQA_HEAD_DIM:GQA_HEAD_DIM + 1, :]))


def _flash(q3, k4, k_index, vt5, v_indices, maps, n_out, tq, tk, name):
    bsz, seq, qw = q3.shape
    nqb = qw // LANES
    n_maps = len(maps)
    n_v = len(v_indices(0))
    per_kind = 2 * n_maps
    once = pl.Buffered(1)
    qt3 = jnp.swapaxes(q3, 1, 2)

    def kmap(b, j, qi):
        n, lb = k_index(b, j)
        return (n, b, 0, lb)

    kspec = pl.BlockSpec((None, None, seq, LANES), kmap, pipeline_mode=once)
    vspecs = [pl.BlockSpec((None, None, seq // tk, VALUE_ROWS, tk),
                           lambda b, j, qi, n=n: (v_indices(j)[n], b, 0, 0, 0), pipeline_mode=once)
              for n in range(n_v)]
    qspec = pl.BlockSpec((None, LANES, tq), lambda b, j, qi: (b, j, qi))
    outs = pl.pallas_call(
        functools.partial(_flash_kernel, maps=maps, n_v=n_v, n_out=n_out, tk=tk),
        grid=(bsz, nqb, seq // tq),
        in_specs=[qspec, kspec] + vspecs,
        out_specs=[qspec] * n_out,
        out_shape=[jax.ShapeDtypeStruct((bsz, qw, seq), F32)] * n_out,
        scratch_shapes=([pltpu.VMEM((n_maps, VALUE_ROWS, tq), F32),
                         pltpu.VMEM((n_maps, SUBLANES, tq), F32),
                         pltpu.VMEM((n_maps, SUBLANES, tq), F32),
                         pltpu.VMEM((n_maps, LANES, tq), BF16)]
                        + [pltpu.VMEM((tk, tq), F32)] * per_kind
                        + [pltpu.VMEM((tk, tq), BF16)] * per_kind),
        compiler_params=_params(("parallel", "parallel", "parallel")),
        name=name,
    )(qt3, k4, *([vt5] * n_v))
    return [jnp.swapaxes(o, 1, 2) for o in outs]


def _augment_values(v3, n_heads, tk):
    bsz, seq, _ = v3.shape
    ones = jnp.ones((bsz, seq, 1), v3.dtype)
    zeros = jnp.zeros((bsz, seq, VALUE_ROWS - GQA_HEAD_DIM - 1), v3.dtype)
    heads = []
    for h in range(n_heads):
        va = jnp.concatenate([v3[..., h * GQA_HEAD_DIM:(h + 1) * GQA_HEAD_DIM], ones, zeros], axis=-1)
        heads.append(va.reshape(bsz, seq // tk, tk, VALUE_ROWS).swapaxes(2, 3))
    return jnp.stack(heads)


GQA_MAPS = ((0, GQA_HEAD_DIM, 0, 0, 0), (GQA_HEAD_DIM, LANES, 0, 0, 1))
DIFF_MAPS = tuple((m * DIFF_HEAD_DIM, (m + 1) * DIFF_HEAD_DIM, m // 2, m % 2, m // 2)
                  for m in range(4))


def _outproj_kernel(x_ref, of_ref, or_ref, ob_ref, oc1_ref, oc2_ref, sg_ref, wo_ref,
                    hnw_ref, dnw_ref, lam_ref, postw_ref, g128_ref, g64_ref, out_ref, *, lam_init):
    sg = sg_ref[...]
    a = of_ref[...] + or_ref[...]
    ms = jnp.dot((a * a).astype(BF16), g128_ref[...], preferred_element_type=F32)
    mix_a = a * lax.rsqrt(ms + NORM_EPS) * hnw_ref[...] * sg[:, :HGRN_WIDTH]
    mix_b = ob_ref[...] * sg[:, HGRN_WIDTH:HGRN_WIDTH + GQA_WIDTH]

    lp = lam_ref[...]
    lam = (jnp.exp(jnp.sum(lp[0:1] * lp[1:2], axis=-1, keepdims=True))
           - jnp.exp(jnp.sum(lp[2:3] * lp[3:4], axis=-1, keepdims=True)) + lam_init)
    c = oc1_ref[...] - lam * oc2_ref[...]
    ms = jnp.dot((c * c).astype(BF16), g64_ref[...], preferred_element_type=F32)
    mix_c = (c * lax.rsqrt(ms + NORM_EPS) * dnw_ref[...] * (1.0 - lam_init)
             * sg[:, HGRN_WIDTH + GQA_WIDTH:])

    y = (jnp.dot(mix_a.astype(BF16), wo_ref[0:HGRN_WIDTH, :], preferred_element_type=F32)
         + jnp.dot(mix_b.astype(BF16), wo_ref[HGRN_WIDTH:HGRN_WIDTH + GQA_WIDTH, :],
                   preferred_element_type=F32)
         + jnp.dot(mix_c.astype(BF16), wo_ref[HGRN_WIDTH + GQA_WIDTH:, :],
                   preferred_element_type=F32))
    out_ref[...] = x_ref[...] + (y * lax.rsqrt(jnp.mean(y * y, axis=-1, keepdims=True) + NORM_EPS)
                                 * postw_ref[...])


def _outproj(x2, of2, or2, ob2, oc1, oc2, sg, wo_bf, hnw, dnw, lam_p, postw, g128, g64, lam_init, tm):
    t = x2.shape[0]
    tok = lambda w: pl.BlockSpec((tm, w), lambda i: (i, 0))
    return pl.pallas_call(
        functools.partial(_outproj_kernel, lam_init=lam_init),
        grid=(t // tm,),
        in_specs=[tok(D_MODEL), tok(HGRN_WIDTH), tok(HGRN_WIDTH), tok(GQA_WIDTH), tok(DIFF_WIDTH),
                  tok(DIFF_WIDTH), tok(MIX_WIDTH), _const_spec((MIX_WIDTH, D_MODEL)),
                  _const_spec((1, HGRN_WIDTH)), _const_spec((1, DIFF_WIDTH)),
                  _const_spec((4, DIFF_HEAD_DIM)), _const_spec((1, D_MODEL)),
                  _const_spec((HGRN_WIDTH, HGRN_WIDTH)), _const_spec((DIFF_WIDTH, DIFF_WIDTH))],
        out_specs=tok(D_MODEL),
        out_shape=jax.ShapeDtypeStruct((t, D_MODEL), F32),
        compiler_params=_params(("parallel",)),
        name="outproj",
    )(x2, of2, or2, ob2, oc1, oc2, sg, wo_bf, hnw, dnw, lam_p, postw, g128, g64)


def _group_mean_matrix(width, group):
    idx = jnp.arange(width) // group
    return ((idx[:, None] == idx[None, :]).astype(F32) / group).astype(BF16)


def _rope_tables(seq_len):
    half = ROPE_DIM // 2
    inv = jnp.power(ROPE_THETA, -jnp.arange(0, ROPE_DIM, 2, dtype=F32) / ROPE_DIM)
    pos = jnp.arange(seq_len, dtype=F32)
    rows = seq_len // GRID_W
    row_pos = jnp.repeat(jnp.arange(rows, dtype=F32), GRID_W)
    col_pos = jnp.tile(jnp.arange(GRID_W, dtype=F32), rows)
    sign = jnp.concatenate([-jnp.ones((half,), F32), jnp.ones((half,), F32)])

    def tab(p):
        ang = p[:, None] * inv[None, :]
        ang = jnp.concatenate([ang, ang], axis=-1)
        return jnp.cos(ang), jnp.sin(ang) * sign

    cr, sr = tab(row_pos)
    cc, sc = tab(col_pos)
    c1, s1 = tab(pos)
    rep_b = GQA_WIDTH // (2 * ROPE_DIM)
    rep_c = DIFF_WIDTH // ROPE_DIM
    return (jnp.tile(jnp.concatenate([cr, cc], axis=-1), (1, rep_b)),
            jnp.tile(jnp.concatenate([sr, sc], axis=-1), (1, rep_b)),
            jnp.tile(c1, (1, rep_c)), jnp.tile(s1, (1, rep_c)))


def _chunk_tri(tb):
    r = jnp.arange(2 * tb)
    same = (r[:, None] // HGRN_CHUNK) == (r[None, :] // HGRN_CHUNK)
    tri = jnp.where(r[:, None] < tb, r[None, :] <= r[:, None], r[None, :] >= r[:, None])
    return (same & tri).astype(BF16)


def _trunk(x, pre_norm_w, w_in_bf, hgrn_lb, hgrn_norm_w, gqa_q_norm_w, gqa_k_norm_w,
           diff_lambda, diff_norm_w, w_out_bf, post_norm_w, *, tm, tb, gqa_tile, diff_tile):
    bsz, seq, _ = x.shape
    t = bsz * seq
    tabs = _rope_tables(seq)
    g64 = _group_mean_matrix(GQA_WIDTH, GQA_HEAD_DIM)
    g128 = _group_mean_matrix(HGRN_WIDTH, HGRN_DIM)
    tri = _chunk_tri(tb)
    x2 = x.reshape(t, D_MODEL)
    for layer in range(DEPTH):
        qnw = jnp.tile(gqa_q_norm_w[layer], GQA_HEADS)[None, :]
        knw = jnp.tile(gqa_k_norm_w[layer], GQA_KV_HEADS)[None, :]
        pa, qb, kb, vb, qc, kc, vc, sg = _inproj(
            x2, seq, pre_norm_w[layer][None, :], w_in_bf[layer], tabs, qnw, knw, g64, tm)

        o_f, o_r = _hgrn(pa.reshape(bsz, seq, A_COLS), hgrn_lb, tri, layer, tb)

        kb3 = kb.reshape(bsz, seq, -1)
        kdup = jnp.stack([jnp.concatenate([kb3[..., h * GQA_HEAD_DIM:(h + 1) * GQA_HEAD_DIM]] * 2, axis=-1)
                          for h in range(GQA_KV_HEADS)])
        (ob,) = _flash(qb.reshape(bsz, seq, -1), kdup, lambda b, j: (j, 0),
                       _augment_values(vb.reshape(bsz, seq, -1), GQA_KV_HEADS, gqa_tile[1]), lambda j: (j,),
                       GQA_MAPS, 1, *gqa_tile, "flash_gqa")

        oc1, oc2 = _flash(qc.reshape(bsz, seq, -1), kc.reshape(1, bsz, seq, -1), lambda b, j: (0, j),
                          _augment_values(vc.reshape(bsz, seq, -1), DIFF_HEADS, diff_tile[1]),
                          lambda j: (2 * j, 2 * j + 1), DIFF_MAPS, 2, *diff_tile, "flash_diff")

        lam_init = 0.8 - 0.6 * math.exp(-0.3 * layer)
        x2 = _outproj(x2, o_f.reshape(t, -1), o_r.reshape(t, -1), ob.reshape(t, -1),
                      oc1.reshape(t, -1), oc2.reshape(t, -1), sg, w_out_bf[layer],
                      hgrn_norm_w[layer][None, :],
                      jnp.tile(diff_norm_w[layer], DIFF_HEADS)[None, :],
                      diff_lambda[layer], post_norm_w[layer][None, :], g128, g64, lam_init, tm)
    return x2.reshape(bsz, seq, D_MODEL)


def kernel(x_prompt, x_sample, pre_norm_w, w_in, hgrn_lb, hgrn_norm_w, gqa_q_norm_w, gqa_k_norm_w,
           diff_lambda, diff_norm_w, w_out, post_norm_w):
    w_in_bf = w_in.astype(BF16)
    w_out_bf = w_out.astype(BF16)
    run = functools.partial(_trunk, pre_norm_w=pre_norm_w, w_in_bf=w_in_bf, hgrn_lb=hgrn_lb,
                            hgrn_norm_w=hgrn_norm_w, gqa_q_norm_w=gqa_q_norm_w,
                            gqa_k_norm_w=gqa_k_norm_w, diff_lambda=diff_lambda,
                            diff_norm_w=diff_norm_w, w_out_bf=w_out_bf, post_norm_w=post_norm_w,
                            tm=512, tb=256, gqa_tile=(1024, 512), diff_tile=(1024, 512))
    return (run(x_prompt), run(x_sample))
```

```python
import functools
import math

import jax
import jax.numpy as jnp
from jax import lax
from jax.experimental import pallas as pl
from jax.experimental.pallas import tpu as pltpu

F32 = jnp.float32
BF16 = jnp.bfloat16

D_MODEL = 1024
DEPTH = 2
GRID_W = 64
HGRN_HEADS = 4
HGRN_DIM = 128
HGRN_WIDTH = HGRN_HEADS * HGRN_DIM
HGRN_CHUNK = 32
GQA_HEADS = 4
GQA_KV_HEADS = 2
GQA_HEAD_DIM = 64
GQA_WIDTH = GQA_HEADS * GQA_HEAD_DIM
GQA_KV_WIDTH = GQA_KV_HEADS * GQA_HEAD_DIM
DIFF_HEADS = 4
DIFF_HEAD_DIM = 32
DIFF_WIDTH = DIFF_HEADS * 2 * DIFF_HEAD_DIM
MIX_WIDTH = HGRN_WIDTH + GQA_WIDTH + DIFF_WIDTH
ROPE_THETA = 10000.0
ROPE_DIM = 32
NORM_EPS = 1e-6

A_COLS = 4 * HGRN_WIDTH
OFF_AG = A_COLS
OFF_B = OFF_AG + HGRN_WIDTH
OFF_BG = OFF_B + GQA_WIDTH + 2 * GQA_KV_WIDTH
OFF_C = OFF_BG + GQA_WIDTH
IN_COLS = OFF_C + 4 * DIFF_WIDTH

LANES = 128
NEG_BIG = -1e30
LOG2E = math.log2(math.e)
VMEM_LIMIT = 56 * 1024 * 1024

SUBLANES = 8
VALUE_ROWS = 80
COL_BLOCK = 128
CHUNKS_PER_TRIP = 4
MAX_LAG_LOG2 = 100.0

NT_DIMS = (((1,), (1,)), ((), ()))
TN_DIMS = (((0,), (0,)), ((), ()))


def _params(sem, flags=None):
    return pltpu.CompilerParams(dimension_semantics=sem, vmem_limit_bytes=VMEM_LIMIT, flags=flags)


def _const_spec(shape):
    nd = len(shape)
    return pl.BlockSpec(shape, lambda *_: (0,) * nd)


def _rot_half16(x):
    outs = []
    for s in range(x.shape[1] // LANES):
        xs = x[:, s * LANES:(s + 1) * LANES]
        up = pltpu.roll(xs, ROPE_DIM // 2, 1)
        dn = pltpu.roll(xs, LANES - ROPE_DIM // 2, 1)
        lane = lax.broadcasted_iota(jnp.int32, xs.shape, 1)
        outs.append(jnp.where((lane & (ROPE_DIM - 1)) < ROPE_DIM // 2, dn, up))
    return outs[0] if len(outs) == 1 else jnp.concatenate(outs, axis=1)


def _silu(x):
    return x * (1.0 / (1.0 + jnp.exp(-x)))


def _inproj_kernel(x_ref, pw_ref, w_ref, cosb_ref, sinb_ref, cosc_ref, sinc_ref,
                   qnw_ref, knw_ref, g64_ref,
                   pa_ref, qb_ref, kb_ref, vb_ref, qc_ref, kc_ref, vc_ref, sg_ref):
    x = x_ref[...]
    h = x * lax.rsqrt(jnp.mean(x * x, axis=-1, keepdims=True) + NORM_EPS) * pw_ref[...]
    hb = h.astype(BF16)

    def proj(lo, hi):
        return jnp.dot(hb, w_ref[:, lo:hi], preferred_element_type=F32)

    pa_ref[...] = proj(0, A_COLS)
    sg_ref[:, 0:HGRN_WIDTH] = _silu(proj(OFF_AG, OFF_B))

    def head_rms(t, w, g):
        ms = jnp.dot((t * t).astype(BF16), g, preferred_element_type=F32)
        return t * lax.rsqrt(ms + NORM_EPS) * w

    def rope(t, cos, sin):
        return t * cos + _rot_half16(t) * sin

    g64 = g64_ref[...]
    cosb = cosb_ref[...]
    sinb = sinb_ref[...]
    bq = head_rms(proj(OFF_B, OFF_B + GQA_WIDTH), qnw_ref[...], g64)
    qb_ref[...] = (rope(bq, cosb, sinb) * (LOG2E / math.sqrt(GQA_HEAD_DIM))).astype(BF16)
    off_k = OFF_B + GQA_WIDTH
    bk = head_rms(proj(off_k, off_k + GQA_KV_WIDTH), knw_ref[...],
                  g64[:GQA_KV_WIDTH, :GQA_KV_WIDTH])
    kb_ref[...] = rope(bk, cosb[:, :GQA_KV_WIDTH], sinb[:, :GQA_KV_WIDTH]).astype(BF16)
    off_v = off_k + GQA_KV_WIDTH
    vb_ref[...] = proj(off_v, OFF_BG).astype(BF16)
    sg_ref[:, HGRN_WIDTH:HGRN_WIDTH + GQA_WIDTH] = _silu(proj(OFF_BG, OFF_C))

    cosc = cosc_ref[...]
    sinc = sinc_ref[...]
    cq = proj(OFF_C, OFF_C + DIFF_WIDTH)
    qc_ref[...] = (rope(cq, cosc, sinc) * (LOG2E / math.sqrt(DIFF_HEAD_DIM))).astype(BF16)
    ck = proj(OFF_C + DIFF_WIDTH, OFF_C + 2 * DIFF_WIDTH)
    kc_ref[...] = rope(ck, cosc, sinc).astype(BF16)
    vc_ref[...] = proj(OFF_C + 2 * DIFF_WIDTH, OFF_C + 3 * DIFF_WIDTH).astype(BF16)
    sg_ref[:, HGRN_WIDTH + GQA_WIDTH:] = _silu(proj(OFF_C + 3 * DIFF_WIDTH, IN_COLS))


def _inproj(x2, seq_len, pre_w, w_in_bf, tabs, qnw, knw, g64, tm):
    t = x2.shape[0]
    tiles_per_seq = seq_len // tm
    tok = lambda w: pl.BlockSpec((tm, w), lambda i: (i, 0))
    tab = pl.BlockSpec((tm, GQA_WIDTH), lambda i: (i % tiles_per_seq, 0))
    out_w = (A_COLS, GQA_WIDTH, GQA_KV_WIDTH, GQA_KV_WIDTH, DIFF_WIDTH, DIFF_WIDTH, DIFF_WIDTH, MIX_WIDTH)
    out_dt = (F32, BF16, BF16, BF16, BF16, BF16, BF16, F32)
    return pl.pallas_call(
        _inproj_kernel,
        grid=(t // tm,),
        in_specs=[tok(D_MODEL), _const_spec((1, D_MODEL)), _const_spec((D_MODEL, IN_COLS)),
                  tab, tab, tab, tab,
                  _const_spec((1, GQA_WIDTH)), _const_spec((1, GQA_KV_WIDTH)),
                  _const_spec((GQA_WIDTH, GQA_WIDTH))],
        out_specs=[tok(w) for w in out_w],
        out_shape=[jax.ShapeDtypeStruct((t, w), d) for w, d in zip(out_w, out_dt)],
        compiler_params=_params(("parallel",)),
        name="inproj",
    )(x2, pre_w, w_in_bf, *tabs, qnw, knw, g64)


def _log1p(x):
    return jnp.log(1.0 + x)


def _hgrn_block(q, xf, v, lb, tri, state_refs):
    tb = q.shape[0] // 2
    nc = tb // HGRN_CHUNK
    log_sig = jnp.minimum(xf, 0.0) - _log1p(jnp.exp(-jnp.abs(xf)))
    c = _log1p(-lb) + log_sig
    a = jnp.log(lb)
    g = jnp.maximum(a, c) + _log1p(jnp.exp(-jnp.abs(a - c)))
    k = (1.0 - lb) * (1.0 / (1.0 + jnp.exp(xf)))
    qs = _silu(q)

    g_hi = g.astype(BF16)
    g_lo = (g - g_hi.astype(F32)).astype(BF16)
    b2 = jnp.dot(tri, jnp.concatenate([g_hi, g_lo], axis=1), preferred_element_type=F32)
    b = b2[:, :HGRN_DIM] + b2[:, HGRN_DIM:]

    mid = (HGRN_CHUNK // 2, HGRN_CHUNK // 2 - 1)
    last = (HGRN_CHUNK - 1, 0)
    b_mid, b_last = [], []
    for d in range(2):
        for ci in range(nc):
            r0 = d * tb + ci * HGRN_CHUNK
            b_mid.append(jnp.broadcast_to(b[r0 + mid[d]:r0 + mid[d] + 1, :], (HGRN_CHUNK, HGRN_DIM)))
            b_last.append(jnp.broadcast_to(b[r0 + last[d]:r0 + last[d] + 1, :],
                                           (HGRN_CHUNK, HGRN_DIM)))
    b_mid = jnp.concatenate(b_mid, axis=0)
    b_last = jnp.concatenate(b_last, axis=0)

    qm = (qs * jnp.exp(b - b_mid)).astype(BF16)
    km = (k * jnp.exp(b_mid - b)).astype(BF16)
    kp = (k * jnp.exp(b_last - b)).astype(BF16)
    qd = (qs * jnp.exp(b)).astype(BF16)
    vb = v.astype(BF16)

    scores = lax.dot_general(qm, km, NT_DIMS, preferred_element_type=F32)
    scores = jnp.where(tri > 0, scores, 0.0).astype(BF16)
    o_intra = jnp.dot(scores, vb, preferred_element_type=F32)

    chunk_rows = [slice(r0, r0 + HGRN_CHUNK) for r0 in range(0, 2 * tb, HGRN_CHUNK)]
    d_state = [lax.dot_general(vb[sl], kp[sl], TN_DIMS, preferred_element_type=F32)
               for sl in chunk_rows]

    states = [ref[...] for ref in state_refs]
    o_inter = [None] * (2 * nc)
    for step in range(nc):
        for d in range(2):
            ci = d * nc + (step if d == 0 else nc - 1 - step)
            sl = chunk_rows[ci]
            o_inter[ci] = lax.dot_general(qd[sl], states[d].astype(BF16), NT_DIMS,
                                          preferred_element_type=F32)
            dec = jnp.exp(b[sl.start + last[d]:sl.start + last[d] + 1, :])
            states[d] = states[d] * dec + d_state[ci]
    for ref, st in zip(state_refs, states):
        ref[...] = st
    return o_intra + jnp.concatenate(o_inter, axis=0)


def _hgrn_kernel(qf_ref, xf_ref, vf_ref, qr_ref, xr_ref, vr_ref, lbp_ref, tri_ref,
                 of_ref, or_ref, sf_ref, sr_ref, *, layer):
    @pl.when(pl.program_id(2) == 0)
    def _():
        sf_ref[...] = jnp.zeros_like(sf_ref)
        sr_ref[...] = jnp.zeros_like(sr_ref)

    rows = [lbp_ref[l] for l in range(DEPTH)]
    top = functools.reduce(jnp.maximum, rows)
    e = [jnp.exp(r - top) for r in rows]
    den = functools.reduce(lambda u, w: u + w, e)
    lb = jnp.zeros((2, HGRN_DIM), F32)
    for l in range(1, layer + 1):
        lb = lb + e[l] / den

    tb = qf_ref.shape[0]
    stack = lambda f_ref, r_ref: jnp.concatenate([f_ref[...], r_ref[...]], axis=0)
    lb_rows = jnp.concatenate([jnp.broadcast_to(lb[0:1], (tb, HGRN_DIM)),
                               jnp.broadcast_to(lb[1:2], (tb, HGRN_DIM))], axis=0)
    out = _hgrn_block(stack(qf_ref, qr_ref), stack(xf_ref, xr_ref), stack(vf_ref, vr_ref),
                      lb_rows, tri_ref[...], (sf_ref, sr_ref))
    of_ref[...] = out[:tb]
    or_ref[...] = out[tb:]


def _hgrn(pa3, hgrn_lb, tri, layer, tb):
    bsz, seq, _ = pa3.shape
    nb = seq // tb
    hw = HGRN_HEADS

    def fwd(col):
        return pl.BlockSpec((None, tb, HGRN_DIM), lambda b, h, i: (b, i, col * hw + h))

    def rev(col):
        return pl.BlockSpec((None, tb, HGRN_DIM), lambda b, h, i: (b, nb - 1 - i, col * hw + h))

    out_f = pl.BlockSpec((None, tb, HGRN_DIM), lambda b, h, i: (b, i, h))
    out_r = pl.BlockSpec((None, tb, HGRN_DIM), lambda b, h, i: (b, nb - 1 - i, h))
    shape = jax.ShapeDtypeStruct((bsz, seq, HGRN_WIDTH), F32)
    return pl.pallas_call(
        functools.partial(_hgrn_kernel, layer=layer),
        grid=(bsz, hw, nb),
        in_specs=[fwd(0), fwd(1), fwd(3), rev(0), rev(2), rev(3),
                  pl.BlockSpec((DEPTH, 2, HGRN_DIM), lambda b, h, i: (0, 0, h)),
                  _const_spec((2 * tb, 2 * tb))],
        out_specs=[out_f, out_r],
        out_shape=[shape, shape],
        scratch_shapes=[pltpu.VMEM((HGRN_DIM, HGRN_DIM), F32),
                        pltpu.VMEM((HGRN_DIM, HGRN_DIM), F32)],
        compiler_params=_params(("parallel", "parallel", "arbitrary")),
        name="hgrn2",
    )(pa3, pa3, pa3, pa3, pa3, pa3, hgrn_lb, tri)


def _flash_kernel(*refs, maps, n_v, n_out, tk):
    n_maps = len(maps)
    qt_ref, k_ref = refs[0], refs[1]
    v_refs = refs[2:2 + n_v]
    out_refs = refs[2 + n_v:2 + n_v + n_out]
    scratch = refs[2 + n_v + n_out:]
    acc_ref, m_ref, gap_ref, qv_ref = scratch[:4]
    per_kind = 2 * n_maps
    s_ref, p_ref, al_ref = (
        [scratch[4 + kind * per_kind + slot * n_maps:4 + kind * per_kind + (slot + 1) * n_maps]
         for slot in range(2)] for kind in range(3))
    tq = qt_ref.shape[1]
    n_chunks = k_ref.shape[0] // tk
    sub = m_ref.shape[1]

    def scores(ci, m):
        off = pl.multiple_of(ci * tk, tk)
        return jnp.dot(k_ref[pl.ds(off, tk), :], qv_ref[m], preferred_element_type=F32)

    def column_max(s):
        top = jnp.max(s.reshape(tk // sub, sub, tq), axis=0)
        return jnp.broadcast_to(jnp.max(top, axis=0, keepdims=True), (sub, tq))

    def lagged_softmax(ci, slot):
        for m in range(n_maps):
            shift = m_ref[m]
            s = scores(ci, m)
            p_ref[slot][m][...] = jnp.exp2(s - shift[0:1, :]).astype(BF16)
            top = column_max(s)
            gap_ref[m] = jnp.maximum(gap_ref[m], top - shift)
            m_next = jnp.maximum(shift, top)
            m_ref[m] = m_next
            al_ref[slot][m][...] = jnp.exp2(shift - m_next)

    def lagged_pv(ci, slot):
        for m, (_, _, vi, _, _) in enumerate(maps):
            acc_ref[m] = (acc_ref[m] + jnp.dot(v_refs[vi][ci], p_ref[slot][m][...],
                                               preferred_element_type=F32)
                          ) * al_ref[slot][m][0:1, :]

    def lagged_trip(t, carry):
        first = t * CHUNKS_PER_TRIP
        lagged_softmax(first, 0)
        for u in range(1, CHUNKS_PER_TRIP):
            lagged_softmax(first + u, u % 2)
            lagged_pv(first + u - 1, (u - 1) % 2)
        lagged_pv(first + CHUNKS_PER_TRIP - 1, (CHUNKS_PER_TRIP - 1) % 2)
        return carry

    def exact_chunk(ci, slot):
        for m, (_, _, vi, _, _) in enumerate(maps):
            s_ref[slot][m][...] = scores(ci, m)
        for m, (_, _, vi, _, _) in enumerate(maps):
            m_prev = m_ref[m]
            m_next = jnp.maximum(m_prev, column_max(s_ref[slot][m][...]))
            m_ref[m] = m_next
            for c0 in range(0, tq, COL_BLOCK):
                cols = slice(c0, c0 + COL_BLOCK)
                p_ref[slot][m][:, cols] = jnp.exp2(
                    s_ref[slot][m][:, cols] - m_next[0:1, cols]).astype(BF16)
            acc_ref[m] = acc_ref[m] * jnp.exp2(m_prev - m_next)[0:1, :] + jnp.dot(
                v_refs[vi][ci], p_ref[slot][m][...], preferred_element_type=F32)

    def key_loop(chunk_fn):
        def body(t, carry):
            for u in range(CHUNKS_PER_TRIP):
                chunk_fn(t * CHUNKS_PER_TRIP + u, u % 2)
            return carry
        lax.fori_loop(0, n_chunks // CHUNKS_PER_TRIP, body, 0)

    qt = qt_ref[...]
    row = lax.broadcasted_iota(jnp.int32, qt.shape, 0)
    for m, (lo, hi, _, _, _) in enumerate(maps):
        qv_ref[m] = jnp.where((row >= lo) & (row < hi), qt, jnp.zeros_like(qt))

    acc_ref[...] = jnp.zeros_like(acc_ref)
    gap_ref[...] = jnp.zeros_like(gap_ref)
    for m in range(n_maps):
        m_ref[m] = column_max(scores(0, m))
    lax.fori_loop(0, n_chunks // CHUNKS_PER_TRIP, lagged_trip, 0)

    @pl.when(jnp.logical_not(jnp.max(gap_ref[...]) <= MAX_LAG_LOG2))
    def _():
        acc_ref[...] = jnp.zeros_like(acc_ref)
        m_ref[...] = jnp.full_like(m_ref, NEG_BIG)
        key_loop(exact_chunk)

    for oi, out_ref in enumerate(out_refs):
        for m, (_, _, _, mo, half) in enumerate(maps):
            if mo == oi:
                acc = acc_ref[m]
                out_ref[half * GQA_HEAD_DIM:(half + 1) * GQA_HEAD_DIM, :] = (
                    acc[0:GQA_HEAD_DIM, :] * (1.0 / acc[GQA_HEAD_DIM:GQA_HEAD_DIM + 1, :]))


def _flash(q3, k4, k_index, vt5, v_indices, maps, n_out, tq, tk, name):
    bsz, seq, qw = q3.shape
    nqb = qw // LANES
    n_maps = len(maps)
    n_v = len(v_indices(0))
    per_kind = 2 * n_maps
    once = pl.Buffered(1)
    qt3 = jnp.swapaxes(q3, 1, 2)

    def kmap(b, j, qi):
        n, lb = k_index(b, j)
        return (n, b, 0, lb)

    kspec = pl.BlockSpec((None, None, seq, LANES), kmap, pipeline_mode=once)
    vspecs = [pl.BlockSpec((None, None, seq // tk, VALUE_ROWS, tk),
                           lambda b, j, qi, n=n: (v_indices(j)[n], b, 0, 0, 0), pipeline_mode=once)
              for n in range(n_v)]
    qspec = pl.BlockSpec((None, LANES, tq), lambda b, j, qi: (b, j, qi))
    outs = pl.pallas_call(
        functools.partial(_flash_kernel, maps=maps, n_v=n_v, n_out=n_out, tk=tk),
        grid=(bsz, nqb, seq // tq),
        in_specs=[qspec, kspec] + vspecs,
        out_specs=[qspec] * n_out,
        out_shape=[jax.ShapeDtypeStruct((bsz, qw, seq), F32)] * n_out,
        scratch_shapes=([pltpu.VMEM((n_maps, VALUE_ROWS, tq), F32),
                         pltpu.VMEM((n_maps, SUBLANES, tq), F32),
                         pltpu.VMEM((n_maps, SUBLANES, tq), F32),
                         pltpu.VMEM((n_maps, LANES, tq), BF16)]
                        + [pltpu.VMEM((tk, tq), F32)] * per_kind
                        + [pltpu.VMEM((tk, tq), BF16)] * per_kind
                        + [pltpu.VMEM((SUBLANES, tq), F32)] * per_kind),
        compiler_params=_params(("parallel", "parallel", "parallel")),
        name=name,
    )(qt3, k4, *([vt5] * n_v))
    return [jnp.swapaxes(o, 1, 2) for o in outs]


def _augment_values(v3, n_heads, tk):
    bsz, seq, _ = v3.shape
    ones = jnp.ones((bsz, seq, 1), v3.dtype)
    zeros = jnp.zeros((bsz, seq, VALUE_ROWS - GQA_HEAD_DIM - 1), v3.dtype)
    heads = []
    for h in range(n_heads):
        va = jnp.concatenate([v3[..., h * GQA_HEAD_DIM:(h + 1) * GQA_HEAD_DIM], ones, zeros], axis=-1)
        heads.append(va.reshape(bsz, seq // tk, tk, VALUE_ROWS).swapaxes(2, 3))
    return jnp.stack(heads)


GQA_MAPS = ((0, GQA_HEAD_DIM, 0, 0, 0), (GQA_HEAD_DIM, LANES, 0, 0, 1))
DIFF_MAPS = tuple((m * DIFF_HEAD_DIM, (m + 1) * DIFF_HEAD_DIM, m // 2, m % 2, m // 2)
                  for m in range(4))


def _outproj_kernel(x_ref, of_ref, or_ref, ob_ref, oc1_ref, oc2_ref, sg_ref, wo_ref,
                    hnw_ref, dnw_ref, lam_ref, postw_ref, g128_ref, g64_ref, out_ref, *, lam_init):
    sg = sg_ref[...]
    a = of_ref[...] + or_ref[...]
    ms = jnp.dot((a * a).astype(BF16), g128_ref[...], preferred_element_type=F32)
    mix_a = a * lax.rsqrt(ms + NORM_EPS) * hnw_ref[...] * sg[:, :HGRN_WIDTH]
    mix_b = ob_ref[...] * sg[:, HGRN_WIDTH:HGRN_WIDTH + GQA_WIDTH]

    lp = lam_ref[...]
    lam = (jnp.exp(jnp.sum(lp[0:1] * lp[1:2], axis=-1, keepdims=True))
           - jnp.exp(jnp.sum(lp[2:3] * lp[3:4], axis=-1, keepdims=True)) + lam_init)
    c = oc1_ref[...] - lam * oc2_ref[...]
    ms = jnp.dot((c * c).astype(BF16), g64_ref[...], preferred_element_type=F32)
    mix_c = (c * lax.rsqrt(ms + NORM_EPS) * dnw_ref[...] * (1.0 - lam_init)
             * sg[:, HGRN_WIDTH + GQA_WIDTH:])

    y = (jnp.dot(mix_a.astype(BF16), wo_ref[0:HGRN_WIDTH, :], preferred_element_type=F32)
         + jnp.dot(mix_b.astype(BF16), wo_ref[HGRN_WIDTH:HGRN_WIDTH + GQA_WIDTH, :],
                   preferred_element_type=F32)
         + jnp.dot(mix_c.astype(BF16), wo_ref[HGRN_WIDTH + GQA_WIDTH:, :],
                   preferred_element_type=F32))
    out_ref[...] = x_ref[...] + (y * lax.rsqrt(jnp.mean(y * y, axis=-1, keepdims=True) + NORM_EPS)
                                 * postw_ref[...])


def _outproj(x2, of2, or2, ob2, oc1, oc2, sg, wo_bf, hnw, dnw, lam_p, postw, g128, g64, lam_init, tm):
    t = x2.shape[0]
    tok = lambda w: pl.BlockSpec((tm, w), lambda i: (i, 0))
    return pl.pallas_call(
        functools.partial(_outproj_kernel, lam_init=lam_init),
        grid=(t // tm,),
        in_specs=[tok(D_MODEL), tok(HGRN_WIDTH), tok(HGRN_WIDTH), tok(GQA_WIDTH), tok(DIFF_WIDTH),
                  tok(DIFF_WIDTH), tok(MIX_WIDTH), _const_spec((MIX_WIDTH, D_MODEL)),
                  _const_spec((1, HGRN_WIDTH)), _const_spec((1, DIFF_WIDTH)),
                  _const_spec((4, DIFF_HEAD_DIM)), _const_spec((1, D_MODEL)),
                  _const_spec((HGRN_WIDTH, HGRN_WIDTH)), _const_spec((DIFF_WIDTH, DIFF_WIDTH))],
        out_specs=tok(D_MODEL),
        out_shape=jax.ShapeDtypeStruct((t, D_MODEL), F32),
        compiler_params=_params(("parallel",)),
        name="outproj",
    )(x2, of2, or2, ob2, oc1, oc2, sg, wo_bf, hnw, dnw, lam_p, postw, g128, g64)


def _group_mean_matrix(width, group):
    idx = jnp.arange(width) // group
    return ((idx[:, None] == idx[None, :]).astype(F32) / group).astype(BF16)


def _rope_tables(seq_len):
    half = ROPE_DIM // 2
    inv = jnp.power(ROPE_THETA, -jnp.arange(0, ROPE_DIM, 2, dtype=F32) / ROPE_DIM)
    pos = jnp.arange(seq_len, dtype=F32)
    rows = seq_len // GRID_W
    row_pos = jnp.repeat(jnp.arange(rows, dtype=F32), GRID_W)
    col_pos = jnp.tile(jnp.arange(GRID_W, dtype=F32), rows)
    sign = jnp.concatenate([-jnp.ones((half,), F32), jnp.ones((half,), F32)])

    def tab(p):
        ang = p[:, None] * inv[None, :]
        ang = jnp.concatenate([ang, ang], axis=-1)
        return jnp.cos(ang), jnp.sin(ang) * sign

    cr, sr = tab(row_pos)
    cc, sc = tab(col_pos)
    c1, s1 = tab(pos)
    rep_b = GQA_WIDTH // (2 * ROPE_DIM)
    rep_c = DIFF_WIDTH // ROPE_DIM
    return (jnp.tile(jnp.concatenate([cr, cc], axis=-1), (1, rep_b)),
            jnp.tile(jnp.concatenate([sr, sc], axis=-1), (1, rep_b)),
            jnp.tile(c1, (1, rep_c)), jnp.tile(s1, (1, rep_c)))


def _chunk_tri(tb):
    r = jnp.arange(2 * tb)
    same = (r[:, None] // HGRN_CHUNK) == (r[None, :] // HGRN_CHUNK)
    tri = jnp.where(r[:, None] < tb, r[None, :] <= r[:, None], r[None, :] >= r[:, None])
    return (same & tri).astype(BF16)


def _trunk(x, pre_norm_w, w_in_bf, hgrn_lb, hgrn_norm_w, gqa_q_norm_w, gqa_k_norm_w,
           diff_lambda, diff_norm_w, w_out_bf, post_norm_w, *, tm, tb, gqa_tile, diff_tile):
    bsz, seq, _ = x.shape
    t = bsz * seq
    tabs = _rope_tables(seq)
    g64 = _group_mean_matrix(GQA_WIDTH, GQA_HEAD_DIM)
    g128 = _group_mean_matrix(HGRN_WIDTH, HGRN_DIM)
    tri = _chunk_tri(tb)
    x2 = x.reshape(t, D_MODEL)
    for layer in range(DEPTH):
        qnw = jnp.tile(gqa_q_norm_w[layer], GQA_HEADS)[None, :]
        knw = jnp.tile(gqa_k_norm_w[layer], GQA_KV_HEADS)[None, :]
        pa, qb, kb, vb, qc, kc, vc, sg = _inproj(
            x2, seq, pre_norm_w[layer][None, :], w_in_bf[layer], tabs, qnw, knw, g64, tm)

        o_f, o_r = _hgrn(pa.reshape(bsz, seq, A_COLS), hgrn_lb, tri, layer, tb)

        kb3 = kb.reshape(bsz, seq, -1)
        kdup = jnp.stack([jnp.concatenate([kb3[..., h * GQA_HEAD_DIM:(h + 1) * GQA_HEAD_DIM]] * 2, axis=-1)
                          for h in range(GQA_KV_HEADS)])
        (ob,) = _flash(qb.reshape(bsz, seq, -1), kdup, lambda b, j: (j, 0),
                       _augment_values(vb.reshape(bsz, seq, -1), GQA_KV_HEADS, gqa_tile[1]), lambda j: (j,),
                       GQA_MAPS, 1, *gqa_tile, "flash_gqa")

        oc1, oc2 = _flash(qc.reshape(bsz, seq, -1), kc.reshape(1, bsz, seq, -1), lambda b, j: (0, j),
                          _augment_values(vc.reshape(bsz, seq, -1), DIFF_HEADS, diff_tile[1]),
                          lambda j: (2 * j, 2 * j + 1), DIFF_MAPS, 2, *diff_tile, "flash_diff")

        lam_init = 0.8 - 0.6 * math.exp(-0.3 * layer)
        x2 = _outproj(x2, o_f.reshape(t, -1), o_r.reshape(t, -1), ob.reshape(t, -1),
                      oc1.reshape(t, -1), oc2.reshape(t, -1), sg, w_out_bf[layer],
                      hgrn_norm_w[layer][None, :],
                      jnp.tile(diff_norm_w[layer], DIFF_HEADS)[None, :],
                      diff_lambda[layer], post_norm_w[layer][None, :], g128, g64, lam_init, tm)
    return x2.reshape(bsz, seq, D_MODEL)


def kernel(x_prompt, x_sample, pre_norm_w, w_in, hgrn_lb, hgrn_norm_w, gqa_q_norm_w, gqa_k_norm_w,
           diff_lambda, diff_norm_w, w_out, post_norm_w):
    w_in_bf = w_in.astype(BF16)
    w_out_bf = w_out.astype(BF16)
    run = functools.partial(_trunk, pre_norm_w=pre_norm_w, w_in_bf=w_in_bf, hgrn_lb=hgrn_lb,
                            hgrn_norm_w=hgrn_norm_w, gqa_q_norm_w=gqa_q_norm_w,
                            gqa_k_norm_w=gqa_k_norm_w, diff_lambda=diff_lambda,
                            diff_norm_w=diff_norm_w, w_out_bf=w_out_bf, post_norm_w=post_norm_w,
                            tm=512, tb=256, gqa_tile=(1024, 512), diff_tile=(1024, 512))
    return (run(x_prompt), run(x_sample))
```

```python
import functools
import math

import jax
import jax.numpy as jnp
from jax import lax
from jax.experimental import pallas as pl
from jax.experimental.pallas import tpu as pltpu

F32 = jnp.float32
BF16 = jnp.bfloat16

D_MODEL = 1024
DEPTH = 2
GRID_W = 64
HGRN_HEADS = 4
HGRN_DIM = 128
HGRN_WIDTH = HGRN_HEADS * HGRN_DIM
HGRN_CHUNK = 32
HGRN_HEADS_PER_STEP = 4
GQA_HEADS = 4
GQA_KV_HEADS = 2
GQA_HEAD_DIM = 64
GQA_WIDTH = GQA_HEADS * GQA_HEAD_DIM
GQA_KV_WIDTH = GQA_KV_HEADS * GQA_HEAD_DIM
DIFF_HEADS = 4
DIFF_HEAD_DIM = 32
DIFF_WIDTH = DIFF_HEADS * 2 * DIFF_HEAD_DIM
MIX_WIDTH = HGRN_WIDTH + GQA_WIDTH + DIFF_WIDTH
ROPE_THETA = 10000.0
ROPE_DIM = 32
NORM_EPS = 1e-6

A_COLS = 4 * HGRN_WIDTH
OFF_AG = A_COLS
OFF_B = OFF_AG + HGRN_WIDTH
OFF_BG = OFF_B + GQA_WIDTH + 2 * GQA_KV_WIDTH
OFF_C = OFF_BG + GQA_WIDTH
IN_COLS = OFF_C + 4 * DIFF_WIDTH

LANES = 128
NEG_BIG = -1e30
LOG2E = math.log2(math.e)
VMEM_LIMIT = 56 * 1024 * 1024

SUBLANES = 8
VALUE_ROWS = 80
COL_BLOCK = 128
CHUNKS_PER_TRIP = 4
MAX_LAG_LOG2 = 100.0

NT_DIMS = (((1,), (1,)), ((), ()))
TN_DIMS = (((0,), (0,)), ((), ()))


def _params(sem, flags=None):
    return pltpu.CompilerParams(dimension_semantics=sem, vmem_limit_bytes=VMEM_LIMIT, flags=flags)


def _const_spec(shape):
    nd = len(shape)
    return pl.BlockSpec(shape, lambda *_: (0,) * nd)


def _rot_half16(x):
    outs = []
    for s in range(x.shape[1] // LANES):
        xs = x[:, s * LANES:(s + 1) * LANES]
        up = pltpu.roll(xs, ROPE_DIM // 2, 1)
        dn = pltpu.roll(xs, LANES - ROPE_DIM // 2, 1)
        lane = lax.broadcasted_iota(jnp.int32, xs.shape, 1)
        outs.append(jnp.where((lane & (ROPE_DIM - 1)) < ROPE_DIM // 2, dn, up))
    return outs[0] if len(outs) == 1 else jnp.concatenate(outs, axis=1)


def _silu(x):
    return x * (1.0 / (1.0 + jnp.exp(-x)))


def _inproj_kernel(x_ref, pw_ref, w_ref, cosb_ref, sinb_ref, cosc_ref, sinc_ref,
                   qnw_ref, knw_ref, g64_ref,
                   pa_ref, qb_ref, kb_ref, vb_ref, qc_ref, kc_ref, vc_ref, sg_ref):
    x = x_ref[...]
    h = x * lax.rsqrt(jnp.mean(x * x, axis=-1, keepdims=True) + NORM_EPS) * pw_ref[...]
    hb = h.astype(BF16)

    def proj(lo, hi):
        return jnp.dot(hb, w_ref[:, lo:hi], preferred_element_type=F32)

    pa_ref[...] = proj(0, A_COLS)
    sg_ref[:, 0:HGRN_WIDTH] = _silu(proj(OFF_AG, OFF_B))

    def head_rms(t, w, g):
        ms = jnp.dot((t * t).astype(BF16), g, preferred_element_type=F32)
        return t * lax.rsqrt(ms + NORM_EPS) * w

    def rope(t, cos, sin):
        return t * cos + _rot_half16(t) * sin

    g64 = g64_ref[...]
    cosb = cosb_ref[...]
    sinb = sinb_ref[...]
    bq = head_rms(proj(OFF_B, OFF_B + GQA_WIDTH), qnw_ref[...], g64)
    qb_ref[...] = (rope(bq, cosb, sinb) * (LOG2E / math.sqrt(GQA_HEAD_DIM))).astype(BF16)
    off_k = OFF_B + GQA_WIDTH
    bk = head_rms(proj(off_k, off_k + GQA_KV_WIDTH), knw_ref[...],
                  g64[:GQA_KV_WIDTH, :GQA_KV_WIDTH])
    kb_ref[...] = rope(bk, cosb[:, :GQA_KV_WIDTH], sinb[:, :GQA_KV_WIDTH]).astype(BF16)
    off_v = off_k + GQA_KV_WIDTH
    vb_ref[...] = proj(off_v, OFF_BG).astype(BF16)
    sg_ref[:, HGRN_WIDTH:HGRN_WIDTH + GQA_WIDTH] = _silu(proj(OFF_BG, OFF_C))

    cosc = cosc_ref[...]
    sinc = sinc_ref[...]
    cq = proj(OFF_C, OFF_C + DIFF_WIDTH)
    qc_ref[...] = (rope(cq, cosc, sinc) * (LOG2E / math.sqrt(DIFF_HEAD_DIM))).astype(BF16)
    ck = proj(OFF_C + DIFF_WIDTH, OFF_C + 2 * DIFF_WIDTH)
    kc_ref[...] = rope(ck, cosc, sinc).astype(BF16)
    vc_ref[...] = proj(OFF_C + 2 * DIFF_WIDTH, OFF_C + 3 * DIFF_WIDTH).astype(BF16)
    sg_ref[:, HGRN_WIDTH + GQA_WIDTH:] = _silu(proj(OFF_C + 3 * DIFF_WIDTH, IN_COLS))


def _inproj(x2, seq_len, pre_w, w_in_bf, tabs, qnw, knw, g64, tm):
    t = x2.shape[0]
    tiles_per_seq = seq_len // tm
    tok = lambda w: pl.BlockSpec((tm, w), lambda i: (i, 0))
    tab = pl.BlockSpec((tm, GQA_WIDTH), lambda i: (i % tiles_per_seq, 0))
    out_w = (A_COLS, GQA_WIDTH, GQA_KV_WIDTH, GQA_KV_WIDTH, DIFF_WIDTH, DIFF_WIDTH, DIFF_WIDTH, MIX_WIDTH)
    out_dt = (F32, BF16, BF16, BF16, BF16, BF16, BF16, F32)
    return pl.pallas_call(
        _inproj_kernel,
        grid=(t // tm,),
        in_specs=[tok(D_MODEL), _const_spec((1, D_MODEL)), _const_spec((D_MODEL, IN_COLS)),
                  tab, tab, tab, tab,
                  _const_spec((1, GQA_WIDTH)), _const_spec((1, GQA_KV_WIDTH)),
                  _const_spec((GQA_WIDTH, GQA_WIDTH))],
        out_specs=[tok(w) for w in out_w],
        out_shape=[jax.ShapeDtypeStruct((t, w), d) for w, d in zip(out_w, out_dt)],
        compiler_params=_params(("parallel",)),
        name="inproj",
    )(x2, pre_w, w_in_bf, *tabs, qnw, knw, g64)


def _log1p(x):
    return jnp.log(1.0 + x)


def _hgrn_prepare(q, xf, v, lb, tri):
    tb = q.shape[0] // 2
    nc = tb // HGRN_CHUNK
    log_sig = jnp.minimum(xf, 0.0) - _log1p(jnp.exp(-jnp.abs(xf)))
    c = _log1p(-lb) + log_sig
    a = jnp.log(lb)
    g = jnp.maximum(a, c) + _log1p(jnp.exp(-jnp.abs(a - c)))
    k = (1.0 - lb) * (1.0 / (1.0 + jnp.exp(xf)))
    qs = _silu(q)

    g_hi = g.astype(BF16)
    g_lo = (g - g_hi.astype(F32)).astype(BF16)
    b2 = jnp.dot(tri, jnp.concatenate([g_hi, g_lo], axis=1), preferred_element_type=F32)
    b = b2[:, :HGRN_DIM] + b2[:, HGRN_DIM:]

    mid = (HGRN_CHUNK // 2, HGRN_CHUNK // 2 - 1)
    last = (HGRN_CHUNK - 1, 0)
    chunk_rows = [slice(r0, r0 + HGRN_CHUNK) for r0 in range(0, 2 * tb, HGRN_CHUNK)]
    b_mid, b_last = [], []
    for ci, sl in enumerate(chunk_rows):
        d = ci // nc
        b_mid.append(jnp.broadcast_to(b[sl.start + mid[d]:sl.start + mid[d] + 1, :],
                                      (HGRN_CHUNK, HGRN_DIM)))
        b_last.append(jnp.broadcast_to(b[sl.start + last[d]:sl.start + last[d] + 1, :],
                                       (HGRN_CHUNK, HGRN_DIM)))
    b_mid = jnp.concatenate(b_mid, axis=0)
    b_last = jnp.concatenate(b_last, axis=0)

    qm = (qs * jnp.exp(b - b_mid)).astype(BF16)
    km = (k * jnp.exp(b_mid - b)).astype(BF16)
    kp = (k * jnp.exp(b_last - b)).astype(BF16)
    qd = (qs * jnp.exp(b)).astype(BF16)
    vb = v.astype(BF16)

    scores = lax.dot_general(qm, km, NT_DIMS, preferred_element_type=F32)
    scores = jnp.where(tri > 0, scores, 0.0).astype(BF16)
    o_intra = jnp.dot(scores, vb, preferred_element_type=F32)

    chunks = []
    for ci, sl in enumerate(chunk_rows):
        d = ci // nc
        dec = jnp.exp(b[sl.start + last[d]:sl.start + last[d] + 1, :])
        inc = lax.dot_general(vb[sl], kp[sl], TN_DIMS, preferred_element_type=F32)
        chunks.append((qd[sl], dec, inc))
    return o_intra, chunks


def _hgrn_scan(prepared, state_refs):
    heads = len(prepared)
    nc = len(prepared[0][1]) // 2
    states = [[state_refs[d][h] for h in range(heads)] for d in range(2)]
    o_inter = [[None] * (2 * nc) for _ in range(heads)]
    for step in range(nc):
        for h in range(heads):
            for d in range(2):
                ci = d * nc + (step if d == 0 else nc - 1 - step)
                qd, dec, inc = prepared[h][1][ci]
                o_inter[h][ci] = lax.dot_general(qd, states[d][h].astype(BF16), NT_DIMS,
                                                 preferred_element_type=F32)
                states[d][h] = states[d][h] * dec + inc
    for d in range(2):
        for h in range(heads):
            state_refs[d][h] = states[d][h]
    return [prepared[h][0] + jnp.concatenate(o_inter[h], axis=0) for h in range(heads)]


def _hgrn_kernel(qf_ref, xf_ref, vf_ref, qr_ref, xr_ref, vr_ref, lbp_ref, tri_ref,
                 of_ref, or_ref, sf_ref, sr_ref, *, layer):
    @pl.when(pl.program_id(2) == 0)
    def _():
        sf_ref[...] = jnp.zeros_like(sf_ref)
        sr_ref[...] = jnp.zeros_like(sr_ref)

    rows = [lbp_ref[l] for l in range(DEPTH)]
    top = functools.reduce(jnp.maximum, rows)
    e = [jnp.exp(r - top) for r in rows]
    den = functools.reduce(lambda u, w: u + w, e)
    lb = jnp.zeros(rows[0].shape, F32)
    for l in range(1, layer + 1):
        lb = lb + e[l] / den

    tb = qf_ref.shape[0]
    heads = qf_ref.shape[1] // HGRN_DIM
    tri = tri_ref[...]
    prepared = []
    for h in range(heads):
        lanes = slice(h * HGRN_DIM, (h + 1) * HGRN_DIM)
        stack = lambda f_ref, r_ref: jnp.concatenate([f_ref[:, lanes], r_ref[:, lanes]], axis=0)
        lb_rows = jnp.concatenate([jnp.broadcast_to(lb[0:1, lanes], (tb, HGRN_DIM)),
                                   jnp.broadcast_to(lb[1:2, lanes], (tb, HGRN_DIM))], axis=0)
        prepared.append(_hgrn_prepare(stack(qf_ref, qr_ref), stack(xf_ref, xr_ref),
                                      stack(vf_ref, vr_ref), lb_rows, tri))
    outs = _hgrn_scan(prepared, (sf_ref, sr_ref))
    for h, out in enumerate(outs):
        lanes = slice(h * HGRN_DIM, (h + 1) * HGRN_DIM)
        of_ref[:, lanes] = out[:tb]
        or_ref[:, lanes] = out[tb:]


def _hgrn(pa3, hgrn_lb, tri, layer, tb):
    bsz, seq, _ = pa3.shape
    nb = seq // tb
    groups = HGRN_HEADS // HGRN_HEADS_PER_STEP
    width = HGRN_HEADS_PER_STEP * HGRN_DIM

    def fwd(col):
        return pl.BlockSpec((None, tb, width), lambda b, h, i: (b, i, col * groups + h))

    def rev(col):
        return pl.BlockSpec((None, tb, width), lambda b, h, i: (b, nb - 1 - i, col * groups + h))

    out_f = pl.BlockSpec((None, tb, width), lambda b, h, i: (b, i, h))
    out_r = pl.BlockSpec((None, tb, width), lambda b, h, i: (b, nb - 1 - i, h))
    shape = jax.ShapeDtypeStruct((bsz, seq, HGRN_WIDTH), F32)
    state = pltpu.VMEM((HGRN_HEADS_PER_STEP, HGRN_DIM, HGRN_DIM), F32)
    return pl.pallas_call(
        functools.partial(_hgrn_kernel, layer=layer),
        grid=(bsz, groups, nb),
        in_specs=[fwd(0), fwd(1), fwd(3), rev(0), rev(2), rev(3),
                  pl.BlockSpec((DEPTH, 2, width), lambda b, h, i: (0, 0, h)),
                  _const_spec((2 * tb, 2 * tb))],
        out_specs=[out_f, out_r],
        out_shape=[shape, shape],
        scratch_shapes=[state, state],
        compiler_params=_params(("parallel", "parallel", "arbitrary")),
        name="hgrn2",
    )(pa3, pa3, pa3, pa3, pa3, pa3, hgrn_lb, tri)


def _flash_kernel(*refs, maps, n_v, n_out, tk):
    n_maps = len(maps)
    qt_ref, k_ref = refs[0], refs[1]
    v_refs = refs[2:2 + n_v]
    out_refs = refs[2 + n_v:2 + n_v + n_out]
    scratch = refs[2 + n_v + n_out:]
    acc_ref, m_ref, gap_ref, qv_ref = scratch[:4]
    per_kind = 2 * n_maps
    s_ref, p_ref, al_ref = (
        [scratch[4 + kind * per_kind + slot * n_maps:4 + kind * per_kind + (slot + 1) * n_maps]
         for slot in range(2)] for kind in range(3))
    tq = qt_ref.shape[1]
    n_chunks = k_ref.shape[0] // tk
    sub = m_ref.shape[1]

    def scores(ci, m):
        off = pl.multiple_of(ci * tk, tk)
        return jnp.dot(k_ref[pl.ds(off, tk), :], qv_ref[m], preferred_element_type=F32)

    def column_max(s):
        top = jnp.max(s.reshape(tk // sub, sub, tq), axis=0)
        return jnp.broadcast_to(jnp.max(top, axis=0, keepdims=True), (sub, tq))

    def lagged_softmax(ci, slot):
        for m in range(n_maps):
            shift = m_ref[m]
            s = scores(ci, m)
            p_ref[slot][m][...] = jnp.exp2(s - shift[0:1, :]).astype(BF16)
            top = column_max(s)
            gap_ref[m] = jnp.maximum(gap_ref[m], top - shift)
            m_next = jnp.maximum(shift, top)
            m_ref[m] = m_next
            al_ref[slot][m][...] = jnp.exp2(shift - m_next)

    def lagged_pv(ci, slot):
        for m, (_, _, vi, _, _) in enumerate(maps):
            acc_ref[m] = (acc_ref[m] + jnp.dot(v_refs[vi][ci], p_ref[slot][m][...],
                                               preferred_element_type=F32)
                          ) * al_ref[slot][m][0:1, :]

    def lagged_trip(t, carry):
        first = t * CHUNKS_PER_TRIP
        lagged_softmax(first, 0)
        for u in range(1, CHUNKS_PER_TRIP):
            lagged_softmax(first + u, u % 2)
            lagged_pv(first + u - 1, (u - 1) % 2)
        lagged_pv(first + CHUNKS_PER_TRIP - 1, (CHUNKS_PER_TRIP - 1) % 2)
        return carry

    def exact_chunk(ci, slot):
        for m, (_, _, vi, _, _) in enumerate(maps):
            s_ref[slot][m][...] = scores(ci, m)
        for m, (_, _, vi, _, _) in enumerate(maps):
            m_prev = m_ref[m]
            m_next = jnp.maximum(m_prev, column_max(s_ref[slot][m][...]))
            m_ref[m] = m_next
            for c0 in range(0, tq, COL_BLOCK):
                cols = slice(c0, c0 + COL_BLOCK)
                p_ref[slot][m][:, cols] = jnp.exp2(
                    s_ref[slot][m][:, cols] - m_next[0:1, cols]).astype(BF16)
            acc_ref[m] = acc_ref[m] * jnp.exp2(m_prev - m_next)[0:1, :] + jnp.dot(
                v_refs[vi][ci], p_ref[slot][m][...], preferred_element_type=F32)

    def key_loop(chunk_fn):
        def body(t, carry):
            for u in range(CHUNKS_PER_TRIP):
                chunk_fn(t * CHUNKS_PER_TRIP + u, u % 2)
            return carry
        lax.fori_loop(0, n_chunks // CHUNKS_PER_TRIP, body, 0)

    qt = qt_ref[...]
    row = lax.broadcasted_iota(jnp.int32, qt.shape, 0)
    for m, (lo, hi, _, _, _) in enumerate(maps):
        qv_ref[m] = jnp.where((row >= lo) & (row < hi), qt, jnp.zeros_like(qt))

    acc_ref[...] = jnp.zeros_like(acc_ref)
    gap_ref[...] = jnp.zeros_like(gap_ref)
    for m in range(n_maps):
        m_ref[m] = column_max(scores(0, m))
    lax.fori_loop(0, n_chunks // CHUNKS_PER_TRIP, lagged_trip, 0)

    @pl.when(jnp.logical_not(jnp.max(gap_ref[...]) <= MAX_LAG_LOG2))
    def _():
        acc_ref[...] = jnp.zeros_like(acc_ref)
        m_ref[...] = jnp.full_like(m_ref, NEG_BIG)
        key_loop(exact_chunk)

    for oi, out_ref in enumerate(out_refs):
        for m, (_, _, _, mo, half) in enumerate(maps):
            if mo == oi:
                acc = acc_ref[m]
                out_ref[half * GQA_HEAD_DIM:(half + 1) * GQA_HEAD_DIM, :] = (
                    acc[0:GQA_HEAD_DIM, :] * (1.0 / acc[GQA_HEAD_DIM:GQA_HEAD_DIM + 1, :]))


def _flash(q3, k4, k_index, vt5, v_indices, maps, n_out, tq, tk, name):
    bsz, seq, qw = q3.shape
    nqb = qw // LANES
    n_maps = len(maps)
    n_v = len(v_indices(0))
    per_kind = 2 * n_maps
    once = pl.Buffered(1)
    qt3 = jnp.swapaxes(q3, 1, 2)

    def kmap(b, j, qi):
        n, lb = k_index(b, j)
        return (n, b, 0, lb)

    kspec = pl.BlockSpec((None, None, seq, LANES), kmap, pipeline_mode=once)
    vspecs = [pl.BlockSpec((None, None, seq // tk, VALUE_ROWS, tk),
                           lambda b, j, qi, n=n: (v_indices(j)[n], b, 0, 0, 0), pipeline_mode=once)
              for n in range(n_v)]
    qspec = pl.BlockSpec((None, LANES, tq), lambda b, j, qi: (b, j, qi))
    outs = pl.pallas_call(
        functools.partial(_flash_kernel, maps=maps, n_v=n_v, n_out=n_out, tk=tk),
        grid=(bsz, nqb, seq // tq),
        in_specs=[qspec, kspec] + vspecs,
        out_specs=[qspec] * n_out,
        out_shape=[jax.ShapeDtypeStruct((bsz, qw, seq), F32)] * n_out,
        scratch_shapes=([pltpu.VMEM((n_maps, VALUE_ROWS, tq), F32),
                         pltpu.VMEM((n_maps, SUBLANES, tq), F32),
                         pltpu.VMEM((n_maps, SUBLANES, tq), F32),
                         pltpu.VMEM((n_maps, LANES, tq), BF16)]
                        + [pltpu.VMEM((tk, tq), F32)] * per_kind
                        + [pltpu.VMEM((tk, tq), BF16)] * per_kind
                        + [pltpu.VMEM((SUBLANES, tq), F32)] * per_kind),
        compiler_params=_params(("parallel", "parallel", "parallel")),
        name=name,
    )(qt3, k4, *([vt5] * n_v))
    return [jnp.swapaxes(o, 1, 2) for o in outs]


def _augment_values(v3, n_heads, tk):
    bsz, seq, _ = v3.shape
    ones = jnp.ones((bsz, seq, 1), v3.dtype)
    zeros = jnp.zeros((bsz, seq, VALUE_ROWS - GQA_HEAD_DIM - 1), v3.dtype)
    heads = []
    for h in range(n_heads):
        va = jnp.concatenate([v3[..., h * GQA_HEAD_DIM:(h + 1) * GQA_HEAD_DIM], ones, zeros], axis=-1)
        heads.append(va.reshape(bsz, seq // tk, tk, VALUE_ROWS).swapaxes(2, 3))
    return jnp.stack(heads)


GQA_MAPS = ((0, GQA_HEAD_DIM, 0, 0, 0), (GQA_HEAD_DIM, LANES, 0, 0, 1))
DIFF_MAPS = tuple((m * DIFF_HEAD_DIM, (m + 1) * DIFF_HEAD_DIM, m // 2, m % 2, m // 2)
                  for m in range(4))


def _outproj_kernel(x_ref, of_ref, or_ref, ob_ref, oc1_ref, oc2_ref, sg_ref, wo_ref,
                    hnw_ref, dnw_ref, lam_ref, postw_ref, g128_ref, g64_ref, out_ref, *, lam_init):
    sg = sg_ref[...]
    a = of_ref[...] + or_ref[...]
    ms = jnp.dot((a * a).astype(BF16), g128_ref[...], preferred_element_type=F32)
    mix_a = a * lax.rsqrt(ms + NORM_EPS) * hnw_ref[...] * sg[:, :HGRN_WIDTH]
    mix_b = ob_ref[...] * sg[:, HGRN_WIDTH:HGRN_WIDTH + GQA_WIDTH]

    lp = lam_ref[...]
    lam = (jnp.exp(jnp.sum(lp[0:1] * lp[1:2], axis=-1, keepdims=True))
           - jnp.exp(jnp.sum(lp[2:3] * lp[3:4], axis=-1, keepdims=True)) + lam_init)
    c = oc1_ref[...] - lam * oc2_ref[...]
    ms = jnp.dot((c * c).astype(BF16), g64_ref[...], preferred_element_type=F32)
    mix_c = (c * lax.rsqrt(ms + NORM_EPS) * dnw_ref[...] * (1.0 - lam_init)
             * sg[:, HGRN_WIDTH + GQA_WIDTH:])

    y = (jnp.dot(mix_a.astype(BF16), wo_ref[0:HGRN_WIDTH, :], preferred_element_type=F32)
         + jnp.dot(mix_b.astype(BF16), wo_ref[HGRN_WIDTH:HGRN_WIDTH + GQA_WIDTH, :],
                   preferred_element_type=F32)
         + jnp.dot(mix_c.astype(BF16), wo_ref[HGRN_WIDTH + GQA_WIDTH:, :],
                   preferred_element_type=F32))
    out_ref[...] = x_ref[...] + (y * lax.rsqrt(jnp.mean(y * y, axis=-1, keepdims=True) + NORM_EPS)
                                 * postw_ref[...])


def _outproj(x2, of2, or2, ob2, oc1, oc2, sg, wo_bf, hnw, dnw, lam_p, postw, g128, g64, lam_init, tm):
    t = x2.shape[0]
    tok = lambda w: pl.BlockSpec((tm, w), lambda i: (i, 0))
    return pl.pallas_call(
        functools.partial(_outproj_kernel, lam_init=lam_init),
        grid=(t // tm,),
        in_specs=[tok(D_MODEL), tok(HGRN_WIDTH), tok(HGRN_WIDTH), tok(GQA_WIDTH), tok(DIFF_WIDTH),
                  tok(DIFF_WIDTH), tok(MIX_WIDTH), _const_spec((MIX_WIDTH, D_MODEL)),
                  _const_spec((1, HGRN_WIDTH)), _const_spec((1, DIFF_WIDTH)),
                  _const_spec((4, DIFF_HEAD_DIM)), _const_spec((1, D_MODEL)),
                  _const_spec((HGRN_WIDTH, HGRN_WIDTH)), _const_spec((DIFF_WIDTH, DIFF_WIDTH))],
        out_specs=tok(D_MODEL),
        out_shape=jax.ShapeDtypeStruct((t, D_MODEL), F32),
        compiler_params=_params(("parallel",)),
        name="outproj",
    )(x2, of2, or2, ob2, oc1, oc2, sg, wo_bf, hnw, dnw, lam_p, postw, g128, g64)


def _group_mean_matrix(width, group):
    idx = jnp.arange(width) // group
    return ((idx[:, None] == idx[None, :]).astype(F32) / group).astype(BF16)


def _rope_tables(seq_len):
    half = ROPE_DIM // 2
    inv = jnp.power(ROPE_THETA, -jnp.arange(0, ROPE_DIM, 2, dtype=F32) / ROPE_DIM)
    pos = jnp.arange(seq_len, dtype=F32)
    rows = seq_len // GRID_W
    row_pos = jnp.repeat(jnp.arange(rows, dtype=F32), GRID_W)
    col_pos = jnp.tile(jnp.arange(GRID_W, dtype=F32), rows)
    sign = jnp.concatenate([-jnp.ones((half,), F32), jnp.ones((half,), F32)])

    def tab(p):
        ang = p[:, None] * inv[None, :]
        ang = jnp.concatenate([ang, ang], axis=-1)
        return jnp.cos(ang), jnp.sin(ang) * sign

    cr, sr = tab(row_pos)
    cc, sc = tab(col_pos)
    c1, s1 = tab(pos)
    rep_b = GQA_WIDTH // (2 * ROPE_DIM)
    rep_c = DIFF_WIDTH // ROPE_DIM
    return (jnp.tile(jnp.concatenate([cr, cc], axis=-1), (1, rep_b)),
            jnp.tile(jnp.concatenate([sr, sc], axis=-1), (1, rep_b)),
            jnp.tile(c1, (1, rep_c)), jnp.tile(s1, (1, rep_c)))


def _chunk_tri(tb):
    r = jnp.arange(2 * tb)
    same = (r[:, None] // HGRN_CHUNK) == (r[None, :] // HGRN_CHUNK)
    tri = jnp.where(r[:, None] < tb, r[None, :] <= r[:, None], r[None, :] >= r[:, None])
    return (same & tri).astype(BF16)


def _trunk(x, pre_norm_w, w_in_bf, hgrn_lb, hgrn_norm_w, gqa_q_norm_w, gqa_k_norm_w,
           diff_lambda, diff_norm_w, w_out_bf, post_norm_w, *, tm, tb, gqa_tile, diff_tile):
    bsz, seq, _ = x.shape
    t = bsz * seq
    tabs = _rope_tables(seq)
    g64 = _group_mean_matrix(GQA_WIDTH, GQA_HEAD_DIM)
    g128 = _group_mean_matrix(HGRN_WIDTH, HGRN_DIM)
    tri = _chunk_tri(tb)
    x2 = x.reshape(t, D_MODEL)
    for layer in range(DEPTH):
        qnw = jnp.tile(gqa_q_norm_w[layer], GQA_HEADS)[None, :]
        knw = jnp.tile(gqa_k_norm_w[layer], GQA_KV_HEADS)[None, :]
        pa, qb, kb, vb, qc, kc, vc, sg = _inproj(
            x2, seq, pre_norm_w[layer][None, :], w_in_bf[layer], tabs, qnw, knw, g64, tm)

        o_f, o_r = _hgrn(pa.reshape(bsz, seq, A_COLS), hgrn_lb, tri, layer, tb)

        kb3 = kb.reshape(bsz, seq, -1)
        kdup = jnp.stack([jnp.concatenate([kb3[..., h * GQA_HEAD_DIM:(h + 1) * GQA_HEAD_DIM]] * 2, axis=-1)
                          for h in range(GQA_KV_HEADS)])
        (ob,) = _flash(qb.reshape(bsz, seq, -1), kdup, lambda b, j: (j, 0),
                       _augment_values(vb.reshape(bsz, seq, -1), GQA_KV_HEADS, gqa_tile[1]), lambda j: (j,),
                       GQA_MAPS, 1, *gqa_tile, "flash_gqa")

        oc1, oc2 = _flash(qc.reshape(bsz, seq, -1), kc.reshape(1, bsz, seq, -1), lambda b, j: (0, j),
                          _augment_values(vc.reshape(bsz, seq, -1), DIFF_HEADS, diff_tile[1]),
                          lambda j: (2 * j, 2 * j + 1), DIFF_MAPS, 2, *diff_tile, "flash_diff")

        lam_init = 0.8 - 0.6 * math.exp(-0.3 * layer)
        x2 = _outproj(x2, o_f.reshape(t, -1), o_r.reshape(t, -1), ob.reshape(t, -1),
                      oc1.reshape(t, -1), oc2.reshape(t, -1), sg, w_out_bf[layer],
                      hgrn_norm_w[layer][None, :],
                      jnp.tile(diff_norm_w[layer], DIFF_HEADS)[None, :],
                      diff_lambda[layer], post_norm_w[layer][None, :], g128, g64, lam_init, tm)
    return x2.reshape(bsz, seq, D_MODEL)


def kernel(x_prompt, x_sample, pre_norm_w, w_in, hgrn_lb, hgrn_norm_w, gqa_q_norm_w, gqa_k_norm_w,
           diff_lambda, diff_norm_w, w_out, post_norm_w):
    w_in_bf = w_in.astype(BF16)
    w_out_bf = w_out.astype(BF16)
    run = functools.partial(_trunk, pre_norm_w=pre_norm_w, w_in_bf=w_in_bf, hgrn_lb=hgrn_lb,
                            hgrn_norm_w=hgrn_norm_w, gqa_q_norm_w=gqa_q_norm_w,
                            gqa_k_norm_w=gqa_k_norm_w, diff_lambda=diff_lambda,
                            diff_norm_w=diff_norm_w, w_out_bf=w_out_bf, post_norm_w=post_norm_w,
                            tm=512, tb=256, gqa_tile=(1024, 512), diff_tile=(1024, 512))
    return (run(x_prompt), run(x_sample))
```

```python
import functools
import math

import jax
import jax.numpy as jnp
from jax import lax
from jax.experimental import pallas as pl
from jax.experimental.pallas import tpu as pltpu

F32 = jnp.float32
BF16 = jnp.bfloat16

D_MODEL = 1024
DEPTH = 2
GRID_W = 64
HGRN_HEADS = 4
HGRN_DIM = 128
HGRN_WIDTH = HGRN_HEADS * HGRN_DIM
HGRN_CHUNK = 32
HGRN_HEADS_PER_STEP = 4
GQA_HEADS = 4
GQA_KV_HEADS = 2
GQA_HEAD_DIM = 64
GQA_WIDTH = GQA_HEADS * GQA_HEAD_DIM
GQA_KV_WIDTH = GQA_KV_HEADS * GQA_HEAD_DIM
DIFF_HEADS = 4
DIFF_HEAD_DIM = 32
DIFF_WIDTH = DIFF_HEADS * 2 * DIFF_HEAD_DIM
MIX_WIDTH = HGRN_WIDTH + GQA_WIDTH + DIFF_WIDTH
ROPE_THETA = 10000.0
ROPE_DIM = 32
NORM_EPS = 1e-6

A_COLS = 4 * HGRN_WIDTH
OFF_AG = A_COLS
OFF_B = OFF_AG + HGRN_WIDTH
OFF_BG = OFF_B + GQA_WIDTH + 2 * GQA_KV_WIDTH
OFF_C = OFF_BG + GQA_WIDTH
IN_COLS = OFF_C + 4 * DIFF_WIDTH

LANES = 128
NEG_BIG = -1e30
LOG2E = math.log2(math.e)
VMEM_LIMIT = 56 * 1024 * 1024

SUBLANES = 8
VALUE_ROWS = 80
COL_BLOCK = 128
CHUNKS_PER_TRIP = 4
MAX_LAG_LOG2 = 100.0

NT_DIMS = (((1,), (1,)), ((), ()))
TN_DIMS = (((0,), (0,)), ((), ()))


def _params(sem, flags=None):
    return pltpu.CompilerParams(dimension_semantics=sem, vmem_limit_bytes=VMEM_LIMIT, flags=flags)


def _const_spec(shape):
    nd = len(shape)
    return pl.BlockSpec(shape, lambda *_: (0,) * nd)


def _rot_half16(x):
    outs = []
    for s in range(x.shape[1] // LANES):
        xs = x[:, s * LANES:(s + 1) * LANES]
        up = pltpu.roll(xs, ROPE_DIM // 2, 1)
        dn = pltpu.roll(xs, LANES - ROPE_DIM // 2, 1)
        lane = lax.broadcasted_iota(jnp.int32, xs.shape, 1)
        outs.append(jnp.where((lane & (ROPE_DIM - 1)) < ROPE_DIM // 2, dn, up))
    return outs[0] if len(outs) == 1 else jnp.concatenate(outs, axis=1)


def _silu(x):
    return x * (1.0 / (1.0 + jnp.exp(-x)))


def _inproj_kernel(x_ref, pw_ref, w_ref, cosb_ref, sinb_ref, cosc_ref, sinc_ref,
                   qnw_ref, knw_ref, g64_ref,
                   pa_ref, qb_ref, kb_ref, vb_ref, qc_ref, kc_ref, vc_ref, sg_ref):
    x = x_ref[...]
    h = x * lax.rsqrt(jnp.mean(x * x, axis=-1, keepdims=True) + NORM_EPS) * pw_ref[...]
    hb = h.astype(BF16)

    def proj(lo, hi):
        return jnp.dot(hb, w_ref[:, lo:hi], preferred_element_type=F32)

    pa_ref[...] = proj(0, A_COLS)
    sg_ref[:, 0:HGRN_WIDTH] = _silu(proj(OFF_AG, OFF_B)).astype(BF16)

    def head_rms(t, w, g):
        ms = jnp.dot((t * t).astype(BF16), g, preferred_element_type=F32)
        return t * lax.rsqrt(ms + NORM_EPS) * w

    def rope(t, cos, sin):
        return t * cos + _rot_half16(t) * sin

    g64 = g64_ref[...]
    cosb = cosb_ref[...]
    sinb = sinb_ref[...]
    bq = head_rms(proj(OFF_B, OFF_B + GQA_WIDTH), qnw_ref[...], g64)
    qb_ref[...] = (rope(bq, cosb, sinb) * (LOG2E / math.sqrt(GQA_HEAD_DIM))).astype(BF16)
    off_k = OFF_B + GQA_WIDTH
    bk = head_rms(proj(off_k, off_k + GQA_KV_WIDTH), knw_ref[...],
                  g64[:GQA_KV_WIDTH, :GQA_KV_WIDTH])
    kb_ref[...] = rope(bk, cosb[:, :GQA_KV_WIDTH], sinb[:, :GQA_KV_WIDTH]).astype(BF16)
    off_v = off_k + GQA_KV_WIDTH
    vb_ref[...] = proj(off_v, OFF_BG).astype(BF16)
    sg_ref[:, HGRN_WIDTH:HGRN_WIDTH + GQA_WIDTH] = _silu(proj(OFF_BG, OFF_C)).astype(BF16)

    cosc = cosc_ref[...]
    sinc = sinc_ref[...]
    cq = proj(OFF_C, OFF_C + DIFF_WIDTH)
    qc_ref[...] = (rope(cq, cosc, sinc) * (LOG2E / math.sqrt(DIFF_HEAD_DIM))).astype(BF16)
    ck = proj(OFF_C + DIFF_WIDTH, OFF_C + 2 * DIFF_WIDTH)
    kc_ref[...] = rope(ck, cosc, sinc).astype(BF16)
    vc_ref[...] = proj(OFF_C + 2 * DIFF_WIDTH, OFF_C + 3 * DIFF_WIDTH).astype(BF16)
    sg_ref[:, HGRN_WIDTH + GQA_WIDTH:] = _silu(proj(OFF_C + 3 * DIFF_WIDTH, IN_COLS)).astype(BF16)


def _inproj(x2, seq_len, pre_w, w_in_bf, tabs, qnw, knw, g64, tm):
    t = x2.shape[0]
    tiles_per_seq = seq_len // tm
    tok = lambda w: pl.BlockSpec((tm, w), lambda i: (i, 0))
    tab = pl.BlockSpec((tm, GQA_WIDTH), lambda i: (i % tiles_per_seq, 0))
    out_w = (A_COLS, GQA_WIDTH, GQA_KV_WIDTH, GQA_KV_WIDTH, DIFF_WIDTH, DIFF_WIDTH, DIFF_WIDTH, MIX_WIDTH)
    out_dt = (F32, BF16, BF16, BF16, BF16, BF16, BF16, BF16)
    return pl.pallas_call(
        _inproj_kernel,
        grid=(t // tm,),
        in_specs=[tok(D_MODEL), _const_spec((1, D_MODEL)), _const_spec((D_MODEL, IN_COLS)),
                  tab, tab, tab, tab,
                  _const_spec((1, GQA_WIDTH)), _const_spec((1, GQA_KV_WIDTH)),
                  _const_spec((GQA_WIDTH, GQA_WIDTH))],
        out_specs=[tok(w) for w in out_w],
        out_shape=[jax.ShapeDtypeStruct((t, w), d) for w, d in zip(out_w, out_dt)],
        compiler_params=_params(("parallel",)),
        name="inproj",
    )(x2, pre_w, w_in_bf, *tabs, qnw, knw, g64)


def _log1p(x):
    return jnp.log(1.0 + x)


def _hgrn_prepare(q, xf, v, lb, tri):
    tb = q.shape[0] // 2
    nc = tb // HGRN_CHUNK
    log_sig = jnp.minimum(xf, 0.0) - _log1p(jnp.exp(-jnp.abs(xf)))
    c = _log1p(-lb) + log_sig
    a = jnp.log(lb)
    g = jnp.maximum(a, c) + _log1p(jnp.exp(-jnp.abs(a - c)))
    k = (1.0 - lb) * (1.0 / (1.0 + jnp.exp(xf)))
    qs = _silu(q)

    g_hi = g.astype(BF16)
    g_lo = (g - g_hi.astype(F32)).astype(BF16)
    b2 = jnp.dot(tri, jnp.concatenate([g_hi, g_lo], axis=1), preferred_element_type=F32)
    b = b2[:, :HGRN_DIM] + b2[:, HGRN_DIM:]

    mid = (HGRN_CHUNK // 2, HGRN_CHUNK // 2 - 1)
    last = (HGRN_CHUNK - 1, 0)
    chunk_rows = [slice(r0, r0 + HGRN_CHUNK) for r0 in range(0, 2 * tb, HGRN_CHUNK)]
    b_mid, b_last = [], []
    for ci, sl in enumerate(chunk_rows):
        d = ci // nc
        b_mid.append(jnp.broadcast_to(b[sl.start + mid[d]:sl.start + mid[d] + 1, :],
                                      (HGRN_CHUNK, HGRN_DIM)))
        b_last.append(jnp.broadcast_to(b[sl.start + last[d]:sl.start + last[d] + 1, :],
                                       (HGRN_CHUNK, HGRN_DIM)))
    b_mid = jnp.concatenate(b_mid, axis=0)
    b_last = jnp.concatenate(b_last, axis=0)

    qm = (qs * jnp.exp(b - b_mid)).astype(BF16)
    km = (k * jnp.exp(b_mid - b)).astype(BF16)
    kp = (k * jnp.exp(b_last - b)).astype(BF16)
    qd = (qs * jnp.exp(b)).astype(BF16)
    vb = v.astype(BF16)

    scores = lax.dot_general(qm, km, NT_DIMS, preferred_element_type=F32)
    scores = jnp.where(tri > 0, scores, 0.0).astype(BF16)
    o_intra = jnp.dot(scores, vb, preferred_element_type=F32)

    chunks = []
    for ci, sl in enumerate(chunk_rows):
        d = ci // nc
        dec = jnp.exp(b[sl.start + last[d]:sl.start + last[d] + 1, :])
        inc = lax.dot_general(vb[sl], kp[sl], TN_DIMS, preferred_element_type=F32)
        chunks.append((qd[sl], dec, inc))
    return o_intra, chunks


def _hgrn_scan(prepared, state_refs):
    heads = len(prepared)
    nc = len(prepared[0][1]) // 2
    states = [[state_refs[d][h] for h in range(heads)] for d in range(2)]
    o_inter = [[None] * (2 * nc) for _ in range(heads)]
    for step in range(nc):
        for h in range(heads):
            for d in range(2):
                ci = d * nc + (step if d == 0 else nc - 1 - step)
                qd, dec, inc = prepared[h][1][ci]
                o_inter[h][ci] = lax.dot_general(qd, states[d][h].astype(BF16), NT_DIMS,
                                                 preferred_element_type=F32)
                states[d][h] = states[d][h] * dec + inc
    for d in range(2):
        for h in range(heads):
            state_refs[d][h] = states[d][h]
    return [prepared[h][0] + jnp.concatenate(o_inter[h], axis=0) for h in range(heads)]


def _hgrn_kernel(qf_ref, xf_ref, vf_ref, qr_ref, xr_ref, vr_ref, lbp_ref, tri_ref,
                 of_ref, or_ref, sf_ref, sr_ref, *, layer):
    @pl.when(pl.program_id(2) == 0)
    def _():
        sf_ref[...] = jnp.zeros_like(sf_ref)
        sr_ref[...] = jnp.zeros_like(sr_ref)

    rows = [lbp_ref[l] for l in range(DEPTH)]
    top = functools.reduce(jnp.maximum, rows)
    e = [jnp.exp(r - top) for r in rows]
    den = functools.reduce(lambda u, w: u + w, e)
    lb = jnp.zeros(rows[0].shape, F32)
    for l in range(1, layer + 1):
        lb = lb + e[l] / den

    tb = qf_ref.shape[0]
    heads = qf_ref.shape[1] // HGRN_DIM
    tri = tri_ref[...]
    prepared = []
    for h in range(heads):
        lanes = slice(h * HGRN_DIM, (h + 1) * HGRN_DIM)
        stack = lambda f_ref, r_ref: jnp.concatenate([f_ref[:, lanes], r_ref[:, lanes]], axis=0)
        lb_rows = jnp.concatenate([jnp.broadcast_to(lb[0:1, lanes], (tb, HGRN_DIM)),
                                   jnp.broadcast_to(lb[1:2, lanes], (tb, HGRN_DIM))], axis=0)
        prepared.append(_hgrn_prepare(stack(qf_ref, qr_ref), stack(xf_ref, xr_ref),
                                      stack(vf_ref, vr_ref), lb_rows, tri))
    outs = _hgrn_scan(prepared, (sf_ref, sr_ref))
    for h, out in enumerate(outs):
        lanes = slice(h * HGRN_DIM, (h + 1) * HGRN_DIM)
        of_ref[:, lanes] = out[:tb].astype(BF16)
        or_ref[:, lanes] = out[tb:].astype(BF16)


def _hgrn(pa3, hgrn_lb, tri, layer, tb):
    bsz, seq, _ = pa3.shape
    nb = seq // tb
    groups = HGRN_HEADS // HGRN_HEADS_PER_STEP
    width = HGRN_HEADS_PER_STEP * HGRN_DIM

    def fwd(col):
        return pl.BlockSpec((None, tb, width), lambda b, h, i: (b, i, col * groups + h))

    def rev(col):
        return pl.BlockSpec((None, tb, width), lambda b, h, i: (b, nb - 1 - i, col * groups + h))

    out_f = pl.BlockSpec((None, tb, width), lambda b, h, i: (b, i, h))
    out_r = pl.BlockSpec((None, tb, width), lambda b, h, i: (b, nb - 1 - i, h))
    shape = jax.ShapeDtypeStruct((bsz, seq, HGRN_WIDTH), BF16)
    state = pltpu.VMEM((HGRN_HEADS_PER_STEP, HGRN_DIM, HGRN_DIM), F32)
    return pl.pallas_call(
        functools.partial(_hgrn_kernel, layer=layer),
        grid=(bsz, groups, nb),
        in_specs=[fwd(0), fwd(1), fwd(3), rev(0), rev(2), rev(3),
                  pl.BlockSpec((DEPTH, 2, width), lambda b, h, i: (0, 0, h)),
                  _const_spec((2 * tb, 2 * tb))],
        out_specs=[out_f, out_r],
        out_shape=[shape, shape],
        scratch_shapes=[state, state],
        compiler_params=_params(("parallel", "parallel", "arbitrary")),
        name="hgrn2",
    )(pa3, pa3, pa3, pa3, pa3, pa3, hgrn_lb, tri)


def _flash_kernel(*refs, maps, n_v, n_out, tk):
    n_maps = len(maps)
    qt_ref, k_ref = refs[0], refs[1]
    v_refs = refs[2:2 + n_v]
    out_refs = refs[2 + n_v:2 + n_v + n_out]
    scratch = refs[2 + n_v + n_out:]
    acc_ref, m_ref, gap_ref, qv_ref = scratch[:4]
    per_kind = 2 * n_maps
    s_ref, p_ref, al_ref = (
        [scratch[4 + kind * per_kind + slot * n_maps:4 + kind * per_kind + (slot + 1) * n_maps]
         for slot in range(2)] for kind in range(3))
    tq = qt_ref.shape[1]
    n_chunks = k_ref.shape[0] // tk
    sub = m_ref.shape[1]

    def scores(ci, m):
        off = pl.multiple_of(ci * tk, tk)
        return jnp.dot(k_ref[pl.ds(off, tk), :], qv_ref[m], preferred_element_type=F32)

    def column_max(s):
        top = jnp.max(s.reshape(s.shape[0] // sub, sub, tq), axis=0)
        return jnp.broadcast_to(jnp.max(top, axis=0, keepdims=True), (sub, tq))

    def lagged_softmax(ci, slot):
        for m in range(n_maps):
            shift = m_ref[m]
            s = scores(ci, m)
            p_ref[slot][m][...] = jnp.exp2(s - shift[0:1, :]).astype(BF16)
            top = column_max(s)
            gap_ref[m] = jnp.maximum(gap_ref[m], top - shift)
            m_next = jnp.maximum(shift, top)
            m_ref[m] = m_next
            al_ref[slot][m][...] = jnp.exp2(shift - m_next)

    def lagged_pv(ci, slot):
        for m, (_, _, vi, _, _) in enumerate(maps):
            acc_ref[m] = (acc_ref[m] + jnp.dot(v_refs[vi][ci], p_ref[slot][m][...],
                                               preferred_element_type=F32)
                          ) * al_ref[slot][m][0:1, :]

    def lagged_trip(t, carry):
        first = t * CHUNKS_PER_TRIP
        lagged_softmax(first, 0)
        for u in range(1, CHUNKS_PER_TRIP):
            lagged_softmax(first + u, u % 2)
            lagged_pv(first + u - 1, (u - 1) % 2)
        lagged_pv(first + CHUNKS_PER_TRIP - 1, (CHUNKS_PER_TRIP - 1) % 2)
        return carry

    def exact_chunk(ci, slot):
        for m, (_, _, vi, _, _) in enumerate(maps):
            s_ref[slot][m][...] = scores(ci, m)
        for m, (_, _, vi, _, _) in enumerate(maps):
            m_prev = m_ref[m]
            m_next = jnp.maximum(m_prev, column_max(s_ref[slot][m][...]))
            m_ref[m] = m_next
            for c0 in range(0, tq, COL_BLOCK):
                cols = slice(c0, c0 + COL_BLOCK)
                p_ref[slot][m][:, cols] = jnp.exp2(
                    s_ref[slot][m][:, cols] - m_next[0:1, cols]).astype(BF16)
            acc_ref[m] = acc_ref[m] * jnp.exp2(m_prev - m_next)[0:1, :] + jnp.dot(
                v_refs[vi][ci], p_ref[slot][m][...], preferred_element_type=F32)

    def key_loop(chunk_fn):
        def body(t, carry):
            for u in range(CHUNKS_PER_TRIP):
                chunk_fn(t * CHUNKS_PER_TRIP + u, u % 2)
            return carry
        lax.fori_loop(0, n_chunks // CHUNKS_PER_TRIP, body, 0)

    qt = qt_ref[...]
    row = lax.broadcasted_iota(jnp.int32, qt.shape, 0)
    for m, (lo, hi, _, _, _) in enumerate(maps):
        qv_ref[m] = jnp.where((row >= lo) & (row < hi), qt, jnp.zeros_like(qt))

    acc_ref[...] = jnp.zeros_like(acc_ref)
    gap_ref[...] = jnp.zeros_like(gap_ref)
    for m in range(n_maps):
        m_ref[m] = column_max(jnp.dot(k_ref[0:LANES, :], qv_ref[m], preferred_element_type=F32))
    lax.fori_loop(0, n_chunks // CHUNKS_PER_TRIP, lagged_trip, 0)

    @pl.when(jnp.logical_not(jnp.max(gap_ref[...]) <= MAX_LAG_LOG2))
    def _():
        acc_ref[...] = jnp.zeros_like(acc_ref)
        m_ref[...] = jnp.full_like(m_ref, NEG_BIG)
        key_loop(exact_chunk)

    for oi, out_ref in enumerate(out_refs):
        for m, (_, _, _, mo, half) in enumerate(maps):
            if mo == oi:
                acc = acc_ref[m]
                out_ref[half * GQA_HEAD_DIM:(half + 1) * GQA_HEAD_DIM, :] = (
                    acc[0:GQA_HEAD_DIM, :] * (1.0 / acc[GQA_HEAD_DIM:GQA_HEAD_DIM + 1, :])
                ).astype(BF16)


def _flash(q3, k4, k_index, vt5, v_indices, maps, n_out, tq, tk, name):
    bsz, seq, qw = q3.shape
    nqb = qw // LANES
    n_maps = len(maps)
    n_v = len(v_indices(0))
    per_kind = 2 * n_maps
    once = pl.Buffered(1)
    qt3 = jnp.swapaxes(q3, 1, 2)

    def kmap(b, j, qi):
        n, lb = k_index(b, j)
        return (n, b, 0, lb)

    kspec = pl.BlockSpec((None, None, seq, LANES), kmap, pipeline_mode=once)
    vspecs = [pl.BlockSpec((None, None, seq // tk, VALUE_ROWS, tk),
                           lambda b, j, qi, n=n: (v_indices(j)[n], b, 0, 0, 0), pipeline_mode=once)
              for n in range(n_v)]
    qspec = pl.BlockSpec((None, LANES, tq), lambda b, j, qi: (b, j, qi))
    outs = pl.pallas_call(
        functools.partial(_flash_kernel, maps=maps, n_v=n_v, n_out=n_out, tk=tk),
        grid=(bsz, nqb, seq // tq),
        in_specs=[qspec, kspec] + vspecs,
        out_specs=[qspec] * n_out,
        out_shape=[jax.ShapeDtypeStruct((bsz, qw, seq), BF16)] * n_out,
        scratch_shapes=([pltpu.VMEM((n_maps, VALUE_ROWS, tq), F32),
                         pltpu.VMEM((n_maps, SUBLANES, tq), F32),
                         pltpu.VMEM((n_maps, SUBLANES, tq), F32),
                         pltpu.VMEM((n_maps, LANES, tq), BF16)]
                        + [pltpu.VMEM((tk, tq), F32)] * per_kind
                        + [pltpu.VMEM((tk, tq), BF16)] * per_kind
                        + [pltpu.VMEM((SUBLANES, tq), F32)] * per_kind),
        compiler_params=_params(("parallel", "parallel", "parallel")),
        name=name,
    )(qt3, k4, *([vt5] * n_v))
    return [jnp.swapaxes(o, 1, 2) for o in outs]


def _augment_values(v3, n_heads, tk):
    bsz, seq, _ = v3.shape
    ones = jnp.ones((bsz, seq, 1), v3.dtype)
    zeros = jnp.zeros((bsz, seq, VALUE_ROWS - GQA_HEAD_DIM - 1), v3.dtype)
    heads = []
    for h in range(n_heads):
        va = jnp.concatenate([v3[..., h * GQA_HEAD_DIM:(h + 1) * GQA_HEAD_DIM], ones, zeros], axis=-1)
        heads.append(va.reshape(bsz, seq // tk, tk, VALUE_ROWS).swapaxes(2, 3))
    return jnp.stack(heads)


GQA_MAPS = ((0, GQA_HEAD_DIM, 0, 0, 0), (GQA_HEAD_DIM, LANES, 0, 0, 1))
DIFF_MAPS = tuple((m * DIFF_HEAD_DIM, (m + 1) * DIFF_HEAD_DIM, m // 2, m % 2, m // 2)
                  for m in range(4))


def _outproj_kernel(x_ref, of_ref, or_ref, ob_ref, oc1_ref, oc2_ref, sg_ref, wo_ref,
                    hnw_ref, dnw_ref, lam_ref, postw_ref, g128_ref, g64_ref, out_ref, *, lam_init):
    sg = sg_ref[...].astype(F32)
    a = of_ref[...].astype(F32) + or_ref[...].astype(F32)
    ms = jnp.dot((a * a).astype(BF16), g128_ref[...], preferred_element_type=F32)
    mix_a = a * lax.rsqrt(ms + NORM_EPS) * hnw_ref[...] * sg[:, :HGRN_WIDTH]
    mix_b = ob_ref[...].astype(F32) * sg[:, HGRN_WIDTH:HGRN_WIDTH + GQA_WIDTH]

    lp = lam_ref[...]
    lam = (jnp.exp(jnp.sum(lp[0:1] * lp[1:2], axis=-1, keepdims=True))
           - jnp.exp(jnp.sum(lp[2:3] * lp[3:4], axis=-1, keepdims=True)) + lam_init)
    c = oc1_ref[...].astype(F32) - lam * oc2_ref[...].astype(F32)
    ms = jnp.dot((c * c).astype(BF16), g64_ref[...], preferred_element_type=F32)
    mix_c = (c * lax.rsqrt(ms + NORM_EPS) * dnw_ref[...] * (1.0 - lam_init)
             * sg[:, HGRN_WIDTH + GQA_WIDTH:])

    y = (jnp.dot(mix_a.astype(BF16), wo_ref[0:HGRN_WIDTH, :], preferred_element_type=F32)
         + jnp.dot(mix_b.astype(BF16), wo_ref[HGRN_WIDTH:HGRN_WIDTH + GQA_WIDTH, :],
                   preferred_element_type=F32)
         + jnp.dot(mix_c.astype(BF16), wo_ref[HGRN_WIDTH + GQA_WIDTH:, :],
                   preferred_element_type=F32))
    out_ref[...] = x_ref[...] + (y * lax.rsqrt(jnp.mean(y * y, axis=-1, keepdims=True) + NORM_EPS)
                                 * postw_ref[...])


def _outproj(x2, of2, or2, ob2, oc1, oc2, sg, wo_bf, hnw, dnw, lam_p, postw, g128, g64, lam_init, tm):
    t = x2.shape[0]
    tok = lambda w: pl.BlockSpec((tm, w), lambda i: (i, 0))
    return pl.pallas_call(
        functools.partial(_outproj_kernel, lam_init=lam_init),
        grid=(t // tm,),
        in_specs=[tok(D_MODEL), tok(HGRN_WIDTH), tok(HGRN_WIDTH), tok(GQA_WIDTH), tok(DIFF_WIDTH),
                  tok(DIFF_WIDTH), tok(MIX_WIDTH), _const_spec((MIX_WIDTH, D_MODEL)),
                  _const_spec((1, HGRN_WIDTH)), _const_spec((1, DIFF_WIDTH)),
                  _const_spec((4, DIFF_HEAD_DIM)), _const_spec((1, D_MODEL)),
                  _const_spec((HGRN_WIDTH, HGRN_WIDTH)), _const_spec((DIFF_WIDTH, DIFF_WIDTH))],
        out_specs=tok(D_MODEL),
        out_shape=jax.ShapeDtypeStruct((t, D_MODEL), F32),
        compiler_params=_params(("parallel",)),
        name="outproj",
    )(x2, of2, or2, ob2, oc1, oc2, sg, wo_bf, hnw, dnw, lam_p, postw, g128, g64)


def _group_mean_matrix(width, group):
    idx = jnp.arange(width) // group
    return ((idx[:, None] == idx[None, :]).astype(F32) / group).astype(BF16)


def _rope_tables(seq_len):
    half = ROPE_DIM // 2
    inv = jnp.power(ROPE_THETA, -jnp.arange(0, ROPE_DIM, 2, dtype=F32) / ROPE_DIM)
    pos = jnp.arange(seq_len, dtype=F32)
    rows = seq_len // GRID_W
    row_pos = jnp.repeat(jnp.arange(rows, dtype=F32), GRID_W)
    col_pos = jnp.tile(jnp.arange(GRID_W, dtype=F32), rows)
    sign = jnp.concatenate([-jnp.ones((half,), F32), jnp.ones((half,), F32)])

    def tab(p):
        ang = p[:, None] * inv[None, :]
        ang = jnp.concatenate([ang, ang], axis=-1)
        return jnp.cos(ang), jnp.sin(ang) * sign

    cr, sr = tab(row_pos)
    cc, sc = tab(col_pos)
    c1, s1 = tab(pos)
    rep_b = GQA_WIDTH // (2 * ROPE_DIM)
    rep_c = DIFF_WIDTH // ROPE_DIM
    return (jnp.tile(jnp.concatenate([cr, cc], axis=-1), (1, rep_b)),
            jnp.tile(jnp.concatenate([sr, sc], axis=-1), (1, rep_b)),
            jnp.tile(c1, (1, rep_c)), jnp.tile(s1, (1, rep_c)))


def _chunk_tri(tb):
    r = jnp.arange(2 * tb)
    same = (r[:, None] // HGRN_CHUNK) == (r[None, :] // HGRN_CHUNK)
    tri = jnp.where(r[:, None] < tb, r[None, :] <= r[:, None], r[None, :] >= r[:, None])
    return (same & tri).astype(BF16)


def _trunk(x, pre_norm_w, w_in_bf, hgrn_lb, hgrn_norm_w, gqa_q_norm_w, gqa_k_norm_w,
           diff_lambda, diff_norm_w, w_out_bf, post_norm_w, *, tm, tb, gqa_tile, diff_tile):
    bsz, seq, _ = x.shape
    t = bsz * seq
    tabs = _rope_tables(seq)
    g64 = _group_mean_matrix(GQA_WIDTH, GQA_HEAD_DIM)
    g128 = _group_mean_matrix(HGRN_WIDTH, HGRN_DIM)
    tri = _chunk_tri(tb)
    x2 = x.reshape(t, D_MODEL)
    for layer in range(DEPTH):
        qnw = jnp.tile(gqa_q_norm_w[layer], GQA_HEADS)[None, :]
        knw = jnp.tile(gqa_k_norm_w[layer], GQA_KV_HEADS)[None, :]
        pa, qb, kb, vb, qc, kc, vc, sg = _inproj(
            x2, seq, pre_norm_w[layer][None, :], w_in_bf[layer], tabs, qnw, knw, g64, tm)

        o_f, o_r = _hgrn(pa.reshape(bsz, seq, A_COLS), hgrn_lb, tri, layer, tb)

        kb3 = kb.reshape(bsz, seq, -1)
        kdup = jnp.stack([jnp.concatenate([kb3[..., h * GQA_HEAD_DIM:(h + 1) * GQA_HEAD_DIM]] * 2, axis=-1)
                          for h in range(GQA_KV_HEADS)])
        (ob,) = _flash(qb.reshape(bsz, seq, -1), kdup, lambda b, j: (j, 0),
                       _augment_values(vb.reshape(bsz, seq, -1), GQA_KV_HEADS, gqa_tile[1]), lambda j: (j,),
                       GQA_MAPS, 1, *gqa_tile, "flash_gqa")

        oc1, oc2 = _flash(qc.reshape(bsz, seq, -1), kc.reshape(1, bsz, seq, -1), lambda b, j: (0, j),
                          _augment_values(vc.reshape(bsz, seq, -1), DIFF_HEADS, diff_tile[1]),
                          lambda j: (2 * j, 2 * j + 1), DIFF_MAPS, 2, *diff_tile, "flash_diff")

        lam_init = 0.8 - 0.6 * math.exp(-0.3 * layer)
        x2 = _outproj(x2, o_f.reshape(t, -1), o_r.reshape(t, -1), ob.reshape(t, -1),
                      oc1.reshape(t, -1), oc2.reshape(t, -1), sg, w_out_bf[layer],
                      hgrn_norm_w[layer][None, :],
                      jnp.tile(diff_norm_w[layer], DIFF_HEADS)[None, :],
                      diff_lambda[layer], post_norm_w[layer][None, :], g128, g64, lam_init, tm)
    return x2.reshape(bsz, seq, D_MODEL)


def kernel(x_prompt, x_sample, pre_norm_w, w_in, hgrn_lb, hgrn_norm_w, gqa_q_norm_w, gqa_k_norm_w,
           diff_lambda, diff_norm_w, w_out, post_norm_w):
    w_in_bf = w_in.astype(BF16)
    w_out_bf = w_out.astype(BF16)
    run = functools.partial(_trunk, pre_norm_w=pre_norm_w, w_in_bf=w_in_bf, hgrn_lb=hgrn_lb,
                            hgrn_norm_w=hgrn_norm_w, gqa_q_norm_w=gqa_q_norm_w,
                            gqa_k_norm_w=gqa_k_norm_w, diff_lambda=diff_lambda,
                            diff_norm_w=diff_norm_w, w_out_bf=w_out_bf, post_norm_w=post_norm_w,
                            tm=512, tb=256, gqa_tile=(1024, 512), diff_tile=(1024, 512))
    return (run(x_prompt), run(x_sample))
```

```python
import functools
import math

import jax
import jax.numpy as jnp
from jax import lax
from jax.experimental import pallas as pl
from jax.experimental.pallas import tpu as pltpu

F32 = jnp.float32
BF16 = jnp.bfloat16

D_MODEL = 1024
DEPTH = 2
GRID_W = 64
HGRN_HEADS = 4
HGRN_DIM = 128
HGRN_WIDTH = HGRN_HEADS * HGRN_DIM
HGRN_CHUNK = 32
HGRN_HEADS_PER_STEP = 4
GQA_HEADS = 4
GQA_KV_HEADS = 2
GQA_HEAD_DIM = 64
GQA_WIDTH = GQA_HEADS * GQA_HEAD_DIM
GQA_KV_WIDTH = GQA_KV_HEADS * GQA_HEAD_DIM
DIFF_HEADS = 4
DIFF_HEAD_DIM = 32
DIFF_WIDTH = DIFF_HEADS * 2 * DIFF_HEAD_DIM
MIX_WIDTH = HGRN_WIDTH + GQA_WIDTH + DIFF_WIDTH
ROPE_THETA = 10000.0
ROPE_DIM = 32
NORM_EPS = 1e-6

A_COLS = 4 * HGRN_WIDTH
OFF_AG = A_COLS
OFF_B = OFF_AG + HGRN_WIDTH
OFF_BG = OFF_B + GQA_WIDTH + 2 * GQA_KV_WIDTH
OFF_C = OFF_BG + GQA_WIDTH
IN_COLS = OFF_C + 4 * DIFF_WIDTH

LANES = 128
NEG_BIG = -1e30
LOG2E = math.log2(math.e)
VMEM_LIMIT = 56 * 1024 * 1024

SUBLANES = 8
VALUE_ROWS = 80
COL_BLOCK = 128
CHUNKS_PER_TRIP = 4
MAX_LAG_LOG2 = 100.0

TILES = {"proj_rows": 512, "hgrn_block": 256, "gqa": (1024, 512), "diff": (1024, 512)}

NT_DIMS = (((1,), (1,)), ((), ()))
TN_DIMS = (((0,), (0,)), ((), ()))


def _params(sem, flags=None):
    return pltpu.CompilerParams(dimension_semantics=sem, vmem_limit_bytes=VMEM_LIMIT, flags=flags)


def _const_spec(shape):
    nd = len(shape)
    return pl.BlockSpec(shape, lambda *_: (0,) * nd)


def _rot_half16(x):
    outs = []
    for s in range(x.shape[1] // LANES):
        xs = x[:, s * LANES:(s + 1) * LANES]
        up = pltpu.roll(xs, ROPE_DIM // 2, 1)
        dn = pltpu.roll(xs, LANES - ROPE_DIM // 2, 1)
        lane = lax.broadcasted_iota(jnp.int32, xs.shape, 1)
        outs.append(jnp.where((lane & (ROPE_DIM - 1)) < ROPE_DIM // 2, dn, up))
    return outs[0] if len(outs) == 1 else jnp.concatenate(outs, axis=1)


def _silu(x):
    return x * (1.0 / (1.0 + jnp.exp(-x)))


def _inproj_kernel(x_ref, pw_ref, w_ref, cosb_ref, sinb_ref, cosc_ref, sinc_ref,
                   qnw_ref, knw_ref, g64_ref,
                   pa_ref, qb_ref, kb_ref, vb_ref, qc_ref, kc_ref, vc_ref, sg_ref):
    x = x_ref[...]
    h = x * lax.rsqrt(jnp.mean(x * x, axis=-1, keepdims=True) + NORM_EPS) * pw_ref[...]
    hb = h.astype(BF16)

    def proj(lo, hi):
        return jnp.dot(hb, w_ref[:, lo:hi], preferred_element_type=F32)

    pa_ref[...] = proj(0, A_COLS)
    sg_ref[:, 0:HGRN_WIDTH] = _silu(proj(OFF_AG, OFF_B)).astype(BF16)

    def head_rms(t, w, g):
        ms = jnp.dot((t * t).astype(BF16), g, preferred_element_type=F32)
        return t * lax.rsqrt(ms + NORM_EPS) * w

    def rope(t, cos, sin):
        return t * cos + _rot_half16(t) * sin

    g64 = g64_ref[...]
    cosb = cosb_ref[...]
    sinb = sinb_ref[...]
    bq = head_rms(proj(OFF_B, OFF_B + GQA_WIDTH), qnw_ref[...], g64)
    qb_ref[...] = (rope(bq, cosb, sinb) * (LOG2E / math.sqrt(GQA_HEAD_DIM))).astype(BF16)
    off_k = OFF_B + GQA_WIDTH
    bk = head_rms(proj(off_k, off_k + GQA_KV_WIDTH), knw_ref[...],
                  g64[:GQA_KV_WIDTH, :GQA_KV_WIDTH])
    kb_ref[...] = rope(bk, cosb[:, :GQA_KV_WIDTH], sinb[:, :GQA_KV_WIDTH]).astype(BF16)
    off_v = off_k + GQA_KV_WIDTH
    vb_ref[...] = proj(off_v, OFF_BG).astype(BF16)
    sg_ref[:, HGRN_WIDTH:HGRN_WIDTH + GQA_WIDTH] = _silu(proj(OFF_BG, OFF_C)).astype(BF16)

    cosc = cosc_ref[...]
    sinc = sinc_ref[...]
    cq = proj(OFF_C, OFF_C + DIFF_WIDTH)
    qc_ref[...] = (rope(cq, cosc, sinc) * (LOG2E / math.sqrt(DIFF_HEAD_DIM))).astype(BF16)
    ck = proj(OFF_C + DIFF_WIDTH, OFF_C + 2 * DIFF_WIDTH)
    kc_ref[...] = rope(ck, cosc, sinc).astype(BF16)
    vc_ref[...] = proj(OFF_C + 2 * DIFF_WIDTH, OFF_C + 3 * DIFF_WIDTH).astype(BF16)
    sg_ref[:, HGRN_WIDTH + GQA_WIDTH:] = _silu(proj(OFF_C + 3 * DIFF_WIDTH, IN_COLS)).astype(BF16)


def _inproj(x2, seq_len, pre_w, w_in_bf, tabs, qnw, knw, g64, tm):
    t = x2.shape[0]
    tiles_per_seq = seq_len // tm
    tok = lambda w: pl.BlockSpec((tm, w), lambda i: (i, 0))
    tab = pl.BlockSpec((tm, GQA_WIDTH), lambda i: (i % tiles_per_seq, 0))
    out_w = (A_COLS, GQA_WIDTH, GQA_KV_WIDTH, GQA_KV_WIDTH, DIFF_WIDTH, DIFF_WIDTH, DIFF_WIDTH, MIX_WIDTH)
    out_dt = (F32, BF16, BF16, BF16, BF16, BF16, BF16, BF16)
    return pl.pallas_call(
        _inproj_kernel,
        grid=(t // tm,),
        in_specs=[tok(D_MODEL), _const_spec((1, D_MODEL)), _const_spec((D_MODEL, IN_COLS)),
                  tab, tab, tab, tab,
                  _const_spec((1, GQA_WIDTH)), _const_spec((1, GQA_KV_WIDTH)),
                  _const_spec((GQA_WIDTH, GQA_WIDTH))],
        out_specs=[tok(w) for w in out_w],
        out_shape=[jax.ShapeDtypeStruct((t, w), d) for w, d in zip(out_w, out_dt)],
        compiler_params=_params(("parallel",)),
        name="inproj",
    )(x2, pre_w, w_in_bf, *tabs, qnw, knw, g64)


def _log1p(x):
    return jnp.log(1.0 + x)


def _hgrn_prepare(q, xf, v, lb, tri):
    tb = q.shape[0] // 2
    nc = tb // HGRN_CHUNK
    def rows(t):
        return jnp.concatenate([jnp.broadcast_to(t[0:1], (tb, HGRN_DIM)),
                                jnp.broadcast_to(t[1:2], (tb, HGRN_DIM))], axis=0)

    e = jnp.exp(-jnp.abs(xf))
    log_sig = jnp.minimum(xf, 0.0) - jnp.log(1.0 + e)
    c = rows(_log1p(-lb)) + log_sig
    a = rows(jnp.log(lb))
    g = jnp.maximum(a, c) + _log1p(jnp.exp(-jnp.abs(a - c)))
    k = rows(1.0 - lb) * jnp.where(xf > 0.0, e, 1.0) * (1.0 / (1.0 + e))
    qs = _silu(q)

    g_hi = g.astype(BF16)
    g_lo = (g - g_hi.astype(F32)).astype(BF16)
    b2 = jnp.dot(tri, jnp.concatenate([g_hi, g_lo], axis=1), preferred_element_type=F32)
    b = b2[:, :HGRN_DIM] + b2[:, HGRN_DIM:]

    mid = (HGRN_CHUNK // 2, HGRN_CHUNK // 2 - 1)
    last = (HGRN_CHUNK - 1, 0)
    chunk_rows = [slice(r0, r0 + HGRN_CHUNK) for r0 in range(0, 2 * tb, HGRN_CHUNK)]
    b_mid, from_start, to_end = [], [], []
    for ci, sl in enumerate(chunk_rows):
        d = ci // nc
        mid_row = b[sl.start + mid[d]:sl.start + mid[d] + 1, :]
        last_row = b[sl.start + last[d]:sl.start + last[d] + 1, :]
        b_mid.append(jnp.broadcast_to(mid_row, (HGRN_CHUNK, HGRN_DIM)))
        from_start.append(jnp.exp(mid_row))
        to_end.append(jnp.exp(last_row - mid_row))
    b_mid = jnp.concatenate(b_mid, axis=0)
    spread = lambda rows_: jnp.concatenate(
        [jnp.broadcast_to(r, (HGRN_CHUNK, HGRN_DIM)) for r in rows_], axis=0)

    q_up = qs * jnp.exp(b - b_mid)
    k_dn = k * jnp.exp(b_mid - b)
    qm = q_up.astype(BF16)
    km = k_dn.astype(BF16)
    kp = (k_dn * spread(to_end)).astype(BF16)
    qd = (q_up * spread(from_start)).astype(BF16)
    vb = v.astype(BF16)

    scores = lax.dot_general(qm, km, NT_DIMS, preferred_element_type=F32)
    scores = jnp.where(tri > 0, scores, 0.0).astype(BF16)
    o_intra = jnp.dot(scores, vb, preferred_element_type=F32)

    chunks = []
    for ci, sl in enumerate(chunk_rows):
        d = ci // nc
        dec = from_start[ci] * to_end[ci]
        inc = lax.dot_general(vb[sl], kp[sl], TN_DIMS, preferred_element_type=F32)
        chunks.append((qd[sl], dec, inc))
    return o_intra, chunks


def _hgrn_scan(prepared, state_refs):
    heads = len(prepared)
    nc = len(prepared[0][1]) // 2
    states = [[state_refs[d][h] for h in range(heads)] for d in range(2)]
    o_inter = [[None] * (2 * nc) for _ in range(heads)]
    for step in range(nc):
        for h in range(heads):
            for d in range(2):
                ci = d * nc + (step if d == 0 else nc - 1 - step)
                qd, dec, inc = prepared[h][1][ci]
                o_inter[h][ci] = lax.dot_general(qd, states[d][h].astype(BF16), NT_DIMS,
                                                 preferred_element_type=F32)
                states[d][h] = states[d][h] * dec + inc
    for d in range(2):
        for h in range(heads):
            state_refs[d][h] = states[d][h]
    return [prepared[h][0] + jnp.concatenate(o_inter[h], axis=0) for h in range(heads)]


def _hgrn_kernel(qf_ref, xf_ref, vf_ref, qr_ref, xr_ref, vr_ref, lbp_ref, tri_ref,
                 of_ref, or_ref, sf_ref, sr_ref, *, layer):
    @pl.when(pl.program_id(2) == 0)
    def _():
        sf_ref[...] = jnp.zeros_like(sf_ref)
        sr_ref[...] = jnp.zeros_like(sr_ref)

    rows = [lbp_ref[l] for l in range(DEPTH)]
    top = functools.reduce(jnp.maximum, rows)
    e = [jnp.exp(r - top) for r in rows]
    den = functools.reduce(lambda u, w: u + w, e)
    lb = jnp.zeros(rows[0].shape, F32)
    for l in range(1, layer + 1):
        lb = lb + e[l] / den

    tb = qf_ref.shape[0]
    heads = qf_ref.shape[1] // HGRN_DIM
    tri = tri_ref[...]
    prepared = []
    for h in range(heads):
        lanes = slice(h * HGRN_DIM, (h + 1) * HGRN_DIM)
        stack = lambda f_ref, r_ref: jnp.concatenate([f_ref[:, lanes], r_ref[:, lanes]], axis=0)
        prepared.append(_hgrn_prepare(stack(qf_ref, qr_ref), stack(xf_ref, xr_ref),
                                      stack(vf_ref, vr_ref), lb[:, lanes], tri))
    outs = _hgrn_scan(prepared, (sf_ref, sr_ref))
    for h, out in enumerate(outs):
        lanes = slice(h * HGRN_DIM, (h + 1) * HGRN_DIM)
        of_ref[:, lanes] = out[:tb].astype(BF16)
        or_ref[:, lanes] = out[tb:].astype(BF16)


def _hgrn(pa3, hgrn_lb, tri, layer, tb):
    bsz, seq, _ = pa3.shape
    nb = seq // tb
    groups = HGRN_HEADS // HGRN_HEADS_PER_STEP
    width = HGRN_HEADS_PER_STEP * HGRN_DIM

    def fwd(col):
        return pl.BlockSpec((None, tb, width), lambda b, h, i: (b, i, col * groups + h))

    def rev(col):
        return pl.BlockSpec((None, tb, width), lambda b, h, i: (b, nb - 1 - i, col * groups + h))

    out_f = pl.BlockSpec((None, tb, width), lambda b, h, i: (b, i, h))
    out_r = pl.BlockSpec((None, tb, width), lambda b, h, i: (b, nb - 1 - i, h))
    shape = jax.ShapeDtypeStruct((bsz, seq, HGRN_WIDTH), BF16)
    state = pltpu.VMEM((HGRN_HEADS_PER_STEP, HGRN_DIM, HGRN_DIM), F32)
    return pl.pallas_call(
        functools.partial(_hgrn_kernel, layer=layer),
        grid=(bsz, groups, nb),
        in_specs=[fwd(0), fwd(1), fwd(3), rev(0), rev(2), rev(3),
                  pl.BlockSpec((DEPTH, 2, width), lambda b, h, i: (0, 0, h)),
                  _const_spec((2 * tb, 2 * tb))],
        out_specs=[out_f, out_r],
        out_shape=[shape, shape],
        scratch_shapes=[state, state],
        compiler_params=_params(("parallel", "parallel", "arbitrary")),
        name="hgrn2",
    )(pa3, pa3, pa3, pa3, pa3, pa3, hgrn_lb, tri)


def _flash_kernel(*refs, maps, n_v, n_out, tk):
    n_maps = len(maps)
    qt_ref, k_ref = refs[0], refs[1]
    v_refs = refs[2:2 + n_v]
    out_refs = refs[2 + n_v:2 + n_v + n_out]
    scratch = refs[2 + n_v + n_out:]
    acc_ref, m_ref, gap_ref, qv_ref = scratch[:4]
    per_kind = 2 * n_maps
    s_ref, p_ref, al_ref = (
        [scratch[4 + kind * per_kind + slot * n_maps:4 + kind * per_kind + (slot + 1) * n_maps]
         for slot in range(2)] for kind in range(3))
    tq = qt_ref.shape[1]
    n_chunks = k_ref.shape[0] // tk
    sub = m_ref.shape[1]

    def scores(ci, m):
        off = pl.multiple_of(ci * tk, tk)
        return jnp.dot(k_ref[pl.ds(off, tk), :], qv_ref[m], preferred_element_type=F32)

    def column_max(s):
        top = jnp.max(s.reshape(s.shape[0] // sub, sub, tq), axis=0)
        return jnp.broadcast_to(jnp.max(top, axis=0, keepdims=True), (sub, tq))

    def lagged_softmax(ci, slot):
        for m in range(n_maps):
            shift = m_ref[m]
            s = scores(ci, m)
            p_ref[slot][m][...] = jnp.exp2(s - shift[0:1, :]).astype(BF16)
            top = column_max(s)
            gap_ref[m] = jnp.maximum(gap_ref[m], top - shift)
            m_next = jnp.maximum(shift, top)
            m_ref[m] = m_next
            al_ref[slot][m][...] = jnp.exp2(shift - m_next)

    def lagged_pv(ci, slot):
        for m, (_, _, vi, _, _) in enumerate(maps):
            acc_ref[m] = (acc_ref[m] + jnp.dot(v_refs[vi][ci], p_ref[slot][m][...],
                                               preferred_element_type=F32)
                          ) * al_ref[slot][m][0:1, :]

    def lagged_trip(t, carry):
        first = t * CHUNKS_PER_TRIP
        lagged_softmax(first, 0)
        for u in range(1, CHUNKS_PER_TRIP):
            lagged_softmax(first + u, u % 2)
            lagged_pv(first + u - 1, (u - 1) % 2)
        lagged_pv(first + CHUNKS_PER_TRIP - 1, (CHUNKS_PER_TRIP - 1) % 2)
        return carry

    def exact_chunk(ci, slot):
        for m, (_, _, vi, _, _) in enumerate(maps):
            s_ref[slot][m][...] = scores(ci, m)
        for m, (_, _, vi, _, _) in enumerate(maps):
            m_prev = m_ref[m]
            m_next = jnp.maximum(m_prev, column_max(s_ref[slot][m][...]))
            m_ref[m] = m_next
            for c0 in range(0, tq, COL_BLOCK):
                cols = slice(c0, c0 + COL_BLOCK)
                p_ref[slot][m][:, cols] = jnp.exp2(
                    s_ref[slot][m][:, cols] - m_next[0:1, cols]).astype(BF16)
            acc_ref[m] = acc_ref[m] * jnp.exp2(m_prev - m_next)[0:1, :] + jnp.dot(
                v_refs[vi][ci], p_ref[slot][m][...], preferred_element_type=F32)

    def key_loop(chunk_fn):
        def body(t, carry):
            for u in range(CHUNKS_PER_TRIP):
                chunk_fn(t * CHUNKS_PER_TRIP + u, u % 2)
            return carry
        lax.fori_loop(0, n_chunks // CHUNKS_PER_TRIP, body, 0)

    qt = qt_ref[...]
    row = lax.broadcasted_iota(jnp.int32, qt.shape, 0)
    for m, (lo, hi, _, _, _) in enumerate(maps):
        qv_ref[m] = jnp.where((row >= lo) & (row < hi), qt, jnp.zeros_like(qt))

    acc_ref[...] = jnp.zeros_like(acc_ref)
    gap_ref[...] = jnp.zeros_like(gap_ref)
    for m in range(n_maps):
        m_ref[m] = column_max(jnp.dot(k_ref[0:LANES, :], qv_ref[m], preferred_element_type=F32))
    lax.fori_loop(0, n_chunks // CHUNKS_PER_TRIP, lagged_trip, 0)

    @pl.when(jnp.logical_not(jnp.max(gap_ref[...]) <= MAX_LAG_LOG2))
    def _():
        acc_ref[...] = jnp.zeros_like(acc_ref)
        m_ref[...] = jnp.full_like(m_ref, NEG_BIG)
        key_loop(exact_chunk)

    for oi, out_ref in enumerate(out_refs):
        for m, (_, _, _, mo, half) in enumerate(maps):
            if mo == oi:
                acc = acc_ref[m]
                out_ref[half * GQA_HEAD_DIM:(half + 1) * GQA_HEAD_DIM, :] = (
                    acc[0:GQA_HEAD_DIM, :] * (1.0 / acc[GQA_HEAD_DIM:GQA_HEAD_DIM + 1, :])
                ).astype(BF16)


def _flash(q3, k4, k_index, vt5, v_indices, maps, n_out, tq, tk, name):
    bsz, seq, qw = q3.shape
    nqb = qw // LANES
    n_maps = len(maps)
    n_v = len(v_indices(0))
    per_kind = 2 * n_maps
    once = pl.Buffered(1)
    qt3 = jnp.swapaxes(q3, 1, 2)

    def kmap(b, j, qi):
        n, lb = k_index(b, j)
        return (n, b, 0, lb)

    kspec = pl.BlockSpec((None, None, seq, LANES), kmap, pipeline_mode=once)
    vspecs = [pl.BlockSpec((None, None, seq // tk, VALUE_ROWS, tk),
                           lambda b, j, qi, n=n: (v_indices(j)[n], b, 0, 0, 0), pipeline_mode=once)
              for n in range(n_v)]
    qspec = pl.BlockSpec((None, LANES, tq), lambda b, j, qi: (b, j, qi))
    outs = pl.pallas_call(
        functools.partial(_flash_kernel, maps=maps, n_v=n_v, n_out=n_out, tk=tk),
        grid=(bsz, nqb, seq // tq),
        in_specs=[qspec, kspec] + vspecs,
        out_specs=[qspec] * n_out,
        out_shape=[jax.ShapeDtypeStruct((bsz, qw, seq), BF16)] * n_out,
        scratch_shapes=([pltpu.VMEM((n_maps, VALUE_ROWS, tq), F32),
                         pltpu.VMEM((n_maps, SUBLANES, tq), F32),
                         pltpu.VMEM((n_maps, SUBLANES, tq), F32),
                         pltpu.VMEM((n_maps, LANES, tq), BF16)]
                        + [pltpu.VMEM((tk, tq), F32)] * per_kind
                        + [pltpu.VMEM((tk, tq), BF16)] * per_kind
                        + [pltpu.VMEM((SUBLANES, tq), F32)] * per_kind),
        compiler_params=_params(("parallel", "parallel", "parallel")),
        name=name,
    )(qt3, k4, *([vt5] * n_v))
    return [jnp.swapaxes(o, 1, 2) for o in outs]


def _augment_values(v3, n_heads, tk):
    bsz, seq, _ = v3.shape
    lead = (bsz, seq // tk, tk, n_heads)
    va = jnp.concatenate([v3.reshape(lead + (GQA_HEAD_DIM,)), jnp.ones(lead + (1,), v3.dtype),
                          jnp.zeros(lead + (VALUE_ROWS - GQA_HEAD_DIM - 1,), v3.dtype)], axis=-1)
    return va.transpose(3, 0, 1, 4, 2)


GQA_MAPS = ((0, GQA_HEAD_DIM, 0, 0, 0), (GQA_HEAD_DIM, LANES, 0, 0, 1))
DIFF_MAPS = tuple((m * DIFF_HEAD_DIM, (m + 1) * DIFF_HEAD_DIM, m // 2, m % 2, m // 2)
                  for m in range(4))


def _outproj_kernel(x_ref, of_ref, or_ref, ob_ref, oc1_ref, oc2_ref, sg_ref, wo_ref,
                    hnw_ref, dnw_ref, lam_ref, postw_ref, g128_ref, g64_ref, out_ref, *, lam_init):
    sg = sg_ref[...].astype(F32)
    a = of_ref[...].astype(F32) + or_ref[...].astype(F32)
    ms = jnp.dot((a * a).astype(BF16), g128_ref[...], preferred_element_type=F32)
    mix_a = a * lax.rsqrt(ms + NORM_EPS) * hnw_ref[...] * sg[:, :HGRN_WIDTH]
    mix_b = ob_ref[...].astype(F32) * sg[:, HGRN_WIDTH:HGRN_WIDTH + GQA_WIDTH]

    lp = lam_ref[...]
    lam = (jnp.exp(jnp.sum(lp[0:1] * lp[1:2], axis=-1, keepdims=True))
           - jnp.exp(jnp.sum(lp[2:3] * lp[3:4], axis=-1, keepdims=True)) + lam_init)
    c = oc1_ref[...].astype(F32) - lam * oc2_ref[...].astype(F32)
    ms = jnp.dot((c * c).astype(BF16), g64_ref[...], preferred_element_type=F32)
    mix_c = (c * lax.rsqrt(ms + NORM_EPS) * dnw_ref[...] * (1.0 - lam_init)
             * sg[:, HGRN_WIDTH + GQA_WIDTH:])

    y = (jnp.dot(mix_a.astype(BF16), wo_ref[0:HGRN_WIDTH, :], preferred_element_type=F32)
         + jnp.dot(mix_b.astype(BF16), wo_ref[HGRN_WIDTH:HGRN_WIDTH + GQA_WIDTH, :],
                   preferred_element_type=F32)
         + jnp.dot(mix_c.astype(BF16), wo_ref[HGRN_WIDTH + GQA_WIDTH:, :],
                   preferred_element_type=F32))
    out_ref[...] = x_ref[...] + (y * lax.rsqrt(jnp.mean(y * y, axis=-1, keepdims=True) + NORM_EPS)
                                 * postw_ref[...])


def _outproj(x2, of2, or2, ob2, oc1, oc2, sg, wo_bf, hnw, dnw, lam_p, postw, g128, g64, lam_init, tm):
    t = x2.shape[0]
    tok = lambda w: pl.BlockSpec((tm, w), lambda i: (i, 0))
    return pl.pallas_call(
        functools.partial(_outproj_kernel, lam_init=lam_init),
        grid=(t // tm,),
        in_specs=[tok(D_MODEL), tok(HGRN_WIDTH), tok(HGRN_WIDTH), tok(GQA_WIDTH), tok(DIFF_WIDTH),
                  tok(DIFF_WIDTH), tok(MIX_WIDTH), _const_spec((MIX_WIDTH, D_MODEL)),
                  _const_spec((1, HGRN_WIDTH)), _const_spec((1, DIFF_WIDTH)),
                  _const_spec((4, DIFF_HEAD_DIM)), _const_spec((1, D_MODEL)),
                  _const_spec((HGRN_WIDTH, HGRN_WIDTH)), _const_spec((DIFF_WIDTH, DIFF_WIDTH))],
        out_specs=tok(D_MODEL),
        out_shape=jax.ShapeDtypeStruct((t, D_MODEL), F32),
        compiler_params=_params(("parallel",)),
        name="outproj",
    )(x2, of2, or2, ob2, oc1, oc2, sg, wo_bf, hnw, dnw, lam_p, postw, g128, g64)


def _group_mean_matrix(width, group):
    idx = jnp.arange(width) // group
    return ((idx[:, None] == idx[None, :]).astype(F32) / group).astype(BF16)


def _rope_tables(seq_len):
    half = ROPE_DIM // 2
    inv = jnp.power(ROPE_THETA, -jnp.arange(0, ROPE_DIM, 2, dtype=F32) / ROPE_DIM)
    pos = jnp.arange(seq_len, dtype=F32)
    rows = seq_len // GRID_W
    row_pos = jnp.repeat(jnp.arange(rows, dtype=F32), GRID_W)
    col_pos = jnp.tile(jnp.arange(GRID_W, dtype=F32), rows)
    sign = jnp.concatenate([-jnp.ones((half,), F32), jnp.ones((half,), F32)])

    def tab(p):
        ang = p[:, None] * inv[None, :]
        ang = jnp.concatenate([ang, ang], axis=-1)
        return jnp.cos(ang), jnp.sin(ang) * sign

    cr, sr = tab(row_pos)
    cc, sc = tab(col_pos)
    c1, s1 = tab(pos)
    rep_b = GQA_WIDTH // (2 * ROPE_DIM)
    rep_c = DIFF_WIDTH // ROPE_DIM
    return (jnp.tile(jnp.concatenate([cr, cc], axis=-1), (1, rep_b)),
            jnp.tile(jnp.concatenate([sr, sc], axis=-1), (1, rep_b)),
            jnp.tile(c1, (1, rep_c)), jnp.tile(s1, (1, rep_c)))


def _chunk_tri(tb):
    r = jnp.arange(2 * tb)
    same = (r[:, None] // HGRN_CHUNK) == (r[None, :] // HGRN_CHUNK)
    tri = jnp.where(r[:, None] < tb, r[None, :] <= r[:, None], r[None, :] >= r[:, None])
    return (same & tri).astype(BF16)


def _trunk(x, pre_norm_w, w_in_bf, hgrn_lb, hgrn_norm_w, gqa_q_norm_w, gqa_k_norm_w,
           diff_lambda, diff_norm_w, w_out_bf, post_norm_w, *, tabs, g64, g128, tri,
           tm, tb, gqa_tile, diff_tile):
    bsz, seq, _ = x.shape
    t = bsz * seq
    x2 = x.reshape(t, D_MODEL)
    for layer in range(DEPTH):
        qnw = jnp.tile(gqa_q_norm_w[layer], GQA_HEADS)[None, :]
        knw = jnp.tile(gqa_k_norm_w[layer], GQA_KV_HEADS)[None, :]
        pa, qb, kb, vb, qc, kc, vc, sg = _inproj(
            x2, seq, pre_norm_w[layer][None, :], w_in_bf[layer], tabs, qnw, knw, g64, tm)

        o_f, o_r = _hgrn(pa.reshape(bsz, seq, A_COLS), hgrn_lb, tri, layer, tb)

        kb4 = kb.reshape(bsz, seq, GQA_KV_HEADS, GQA_HEAD_DIM)
        kdup = jnp.concatenate([kb4, kb4], axis=-1).transpose(2, 0, 1, 3)
        (ob,) = _flash(qb.reshape(bsz, seq, -1), kdup, lambda b, j: (j, 0),
                       _augment_values(vb.reshape(bsz, seq, -1), GQA_KV_HEADS, gqa_tile[1]), lambda j: (j,),
                       GQA_MAPS, 1, *gqa_tile, "flash_gqa")

        oc1, oc2 = _flash(qc.reshape(bsz, seq, -1), kc.reshape(1, bsz, seq, -1), lambda b, j: (0, j),
                          _augment_values(vc.reshape(bsz, seq, -1), DIFF_HEADS, diff_tile[1]),
                          lambda j: (2 * j, 2 * j + 1), DIFF_MAPS, 2, *diff_tile, "flash_diff")

        lam_init = 0.8 - 0.6 * math.exp(-0.3 * layer)
        x2 = _outproj(x2, o_f.reshape(t, -1), o_r.reshape(t, -1), ob.reshape(t, -1),
                      oc1.reshape(t, -1), oc2.reshape(t, -1), sg, w_out_bf[layer],
                      hgrn_norm_w[layer][None, :],
                      jnp.tile(diff_norm_w[layer], DIFF_HEADS)[None, :],
                      diff_lambda[layer], post_norm_w[layer][None, :], g128, g64, lam_init, tm)
    return x2.reshape(bsz, seq, D_MODEL)


def kernel(x_prompt, x_sample, pre_norm_w, w_in, hgrn_lb, hgrn_norm_w, gqa_q_norm_w, gqa_k_norm_w,
           diff_lambda, diff_norm_w, w_out, post_norm_w):
    w_in_bf = w_in.astype(BF16)
    w_out_bf = w_out.astype(BF16)
    longest = max(x_prompt.shape[1], x_sample.shape[1])
    run = functools.partial(_trunk, pre_norm_w=pre_norm_w, w_in_bf=w_in_bf, hgrn_lb=hgrn_lb,
                            hgrn_norm_w=hgrn_norm_w, gqa_q_norm_w=gqa_q_norm_w,
                            gqa_k_norm_w=gqa_k_norm_w, diff_lambda=diff_lambda,
                            diff_norm_w=diff_norm_w, w_out_bf=w_out_bf, post_norm_w=post_norm_w,
                            tabs=_rope_tables(longest),
                            g64=_group_mean_matrix(GQA_WIDTH, GQA_HEAD_DIM),
                            g128=_group_mean_matrix(HGRN_WIDTH, HGRN_DIM),
                            tri=_chunk_tri(TILES["hgrn_block"]),
                            tm=TILES["proj_rows"], tb=TILES["hgrn_block"],
                            gqa_tile=TILES["gqa"], diff_tile=TILES["diff"])
    return (run(x_prompt), run(x_sample))
```

```python
import functools
import math

import jax
import jax.numpy as jnp
from jax import lax
from jax.experimental import pallas as pl
from jax.experimental.pallas import tpu as pltpu

F32 = jnp.float32
BF16 = jnp.bfloat16

D_MODEL = 1024
DEPTH = 2
GRID_W = 64
HGRN_HEADS = 4
HGRN_DIM = 128
HGRN_WIDTH = HGRN_HEADS * HGRN_DIM
HGRN_CHUNK = 32
HGRN_HEADS_PER_STEP = 4
GQA_HEADS = 4
GQA_KV_HEADS = 2
GQA_HEAD_DIM = 64
GQA_WIDTH = GQA_HEADS * GQA_HEAD_DIM
GQA_KV_WIDTH = GQA_KV_HEADS * GQA_HEAD_DIM
DIFF_HEADS = 4
DIFF_HEAD_DIM = 32
DIFF_WIDTH = DIFF_HEADS * 2 * DIFF_HEAD_DIM
MIX_WIDTH = HGRN_WIDTH + GQA_WIDTH + DIFF_WIDTH
ROPE_THETA = 10000.0
ROPE_DIM = 32
NORM_EPS = 1e-6

A_COLS = 4 * HGRN_WIDTH
OFF_AG = A_COLS
OFF_B = OFF_AG + HGRN_WIDTH
OFF_BG = OFF_B + GQA_WIDTH + 2 * GQA_KV_WIDTH
OFF_C = OFF_BG + GQA_WIDTH
IN_COLS = OFF_C + 4 * DIFF_WIDTH

LANES = 128
NEG_BIG = -1e30
LOG2E = math.log2(math.e)
VMEM_LIMIT = 56 * 1024 * 1024

SUBLANES = 8
VALUE_ROWS = 80
COL_BLOCK = 128
CHUNKS_PER_TRIP = 4
MAX_LAG_LOG2 = 100.0

TILES = {"proj_rows": 512, "hgrn_block": 256, "gqa": (1024, 512), "diff": (1024, 512)}

NT_DIMS = (((1,), (1,)), ((), ()))
TN_DIMS = (((0,), (0,)), ((), ()))


def _params(sem, flags=None):
    return pltpu.CompilerParams(dimension_semantics=sem, vmem_limit_bytes=VMEM_LIMIT, flags=flags)


def _const_spec(shape):
    nd = len(shape)
    return pl.BlockSpec(shape, lambda *_: (0,) * nd)


def _rot_half16(x):
    outs = []
    for s in range(x.shape[1] // LANES):
        xs = x[:, s * LANES:(s + 1) * LANES]
        up = pltpu.roll(xs, ROPE_DIM // 2, 1)
        dn = pltpu.roll(xs, LANES - ROPE_DIM // 2, 1)
        lane = lax.broadcasted_iota(jnp.int32, xs.shape, 1)
        outs.append(jnp.where((lane & (ROPE_DIM - 1)) < ROPE_DIM // 2, dn, up))
    return outs[0] if len(outs) == 1 else jnp.concatenate(outs, axis=1)


def _silu(x):
    return x * (1.0 / (1.0 + jnp.exp(-x)))


def _inproj_kernel(x_ref, pw_ref, w_ref, cosb_ref, sinb_ref, cosc_ref, sinc_ref,
                   qnw_ref, knw_ref, g64_ref,
                   pa_ref, qb_ref, kb_ref, vb_ref, qc_ref, kc_ref, vc_ref, sg_ref):
    x = x_ref[...]
    h = x * lax.rsqrt(jnp.mean(x * x, axis=-1, keepdims=True) + NORM_EPS) * pw_ref[...]
    hb = h.astype(BF16)

    def proj(lo, hi):
        return jnp.dot(hb, w_ref[:, lo:hi], preferred_element_type=F32)

    pa_ref[...] = proj(0, A_COLS)
    sg_ref[:, 0:HGRN_WIDTH] = _silu(proj(OFF_AG, OFF_B)).astype(BF16)

    def head_rms(t, w, g):
        ms = jnp.dot((t * t).astype(BF16), g, preferred_element_type=F32)
        return t * lax.rsqrt(ms + NORM_EPS) * w

    def rope(t, cos, sin):
        return t * cos + _rot_half16(t) * sin

    g64 = g64_ref[...]
    cosb = cosb_ref[...]
    sinb = sinb_ref[...]
    bq = head_rms(proj(OFF_B, OFF_B + GQA_WIDTH), qnw_ref[...], g64)
    qb_ref[...] = (rope(bq, cosb, sinb) * (LOG2E / math.sqrt(GQA_HEAD_DIM))).astype(BF16)
    off_k = OFF_B + GQA_WIDTH
    bk = head_rms(proj(off_k, off_k + GQA_KV_WIDTH), knw_ref[...],
                  g64[:GQA_KV_WIDTH, :GQA_KV_WIDTH])
    kb_ref[...] = rope(bk, cosb[:, :GQA_KV_WIDTH], sinb[:, :GQA_KV_WIDTH]).astype(BF16)
    off_v = off_k + GQA_KV_WIDTH
    vb_ref[...] = proj(off_v, OFF_BG).astype(BF16)
    sg_ref[:, HGRN_WIDTH:HGRN_WIDTH + GQA_WIDTH] = _silu(proj(OFF_BG, OFF_C)).astype(BF16)

    cosc = cosc_ref[...]
    sinc = sinc_ref[...]
    cq = proj(OFF_C, OFF_C + DIFF_WIDTH)
    qc_ref[...] = (rope(cq, cosc, sinc) * (LOG2E / math.sqrt(DIFF_HEAD_DIM))).astype(BF16)
    ck = proj(OFF_C + DIFF_WIDTH, OFF_C + 2 * DIFF_WIDTH)
    kc_ref[...] = rope(ck, cosc, sinc).astype(BF16)
    vc_ref[...] = proj(OFF_C + 2 * DIFF_WIDTH, OFF_C + 3 * DIFF_WIDTH).astype(BF16)
    sg_ref[:, HGRN_WIDTH + GQA_WIDTH:] = _silu(proj(OFF_C + 3 * DIFF_WIDTH, IN_COLS)).astype(BF16)


def _inproj(x2, seq_len, pre_w, w_in_bf, tabs, qnw, knw, g64, tm):
    t = x2.shape[0]
    tiles_per_seq = seq_len // tm
    tok = lambda w: pl.BlockSpec((tm, w), lambda i: (i, 0))
    tab = pl.BlockSpec((tm, GQA_WIDTH), lambda i: (i % tiles_per_seq, 0))
    out_w = (A_COLS, GQA_WIDTH, GQA_KV_WIDTH, GQA_KV_WIDTH, DIFF_WIDTH, DIFF_WIDTH, DIFF_WIDTH, MIX_WIDTH)
    out_dt = (F32, BF16, BF16, BF16, BF16, BF16, BF16, BF16)
    return pl.pallas_call(
        _inproj_kernel,
        grid=(t // tm,),
        in_specs=[tok(D_MODEL), _const_spec((1, D_MODEL)), _const_spec((D_MODEL, IN_COLS)),
                  tab, tab, tab, tab,
                  _const_spec((1, GQA_WIDTH)), _const_spec((1, GQA_KV_WIDTH)),
                  _const_spec((GQA_WIDTH, GQA_WIDTH))],
        out_specs=[tok(w) for w in out_w],
        out_shape=[jax.ShapeDtypeStruct((t, w), d) for w, d in zip(out_w, out_dt)],
        compiler_params=_params(("parallel",)),
        name="inproj",
    )(x2, pre_w, w_in_bf, *tabs, qnw, knw, g64)


def _log1p(x):
    return jnp.log(1.0 + x)


def _hgrn_prepare(q, xf, v, lb, tri):
    tb = q.shape[0] // 2
    nc = tb // HGRN_CHUNK
    def rows(t):
        return jnp.concatenate([jnp.broadcast_to(t[0:1], (tb, HGRN_DIM)),
                                jnp.broadcast_to(t[1:2], (tb, HGRN_DIM))], axis=0)

    e = jnp.exp(-jnp.abs(xf))
    log_sig = jnp.minimum(xf, 0.0) - jnp.log(1.0 + e)
    c = rows(_log1p(-lb)) + log_sig
    a = rows(jnp.log(lb))
    g = jnp.maximum(a, c) + _log1p(jnp.exp(-jnp.abs(a - c)))
    k = rows(1.0 - lb) * jnp.where(xf > 0.0, e, 1.0) * (1.0 / (1.0 + e))
    qs = _silu(q)

    g_hi = g.astype(BF16)
    g_lo = (g - g_hi.astype(F32)).astype(BF16)
    b2 = jnp.dot(tri, jnp.concatenate([g_hi, g_lo], axis=1), preferred_element_type=F32)
    b = b2[:, :HGRN_DIM] + b2[:, HGRN_DIM:]

    mid = (HGRN_CHUNK // 2, HGRN_CHUNK // 2 - 1)
    last = (HGRN_CHUNK - 1, 0)
    chunk_rows = [slice(r0, r0 + HGRN_CHUNK) for r0 in range(0, 2 * tb, HGRN_CHUNK)]
    b_mid, from_start, to_end = [], [], []
    for ci, sl in enumerate(chunk_rows):
        d = ci // nc
        mid_row = b[sl.start + mid[d]:sl.start + mid[d] + 1, :]
        last_row = b[sl.start + last[d]:sl.start + last[d] + 1, :]
        b_mid.append(jnp.broadcast_to(mid_row, (HGRN_CHUNK, HGRN_DIM)))
        from_start.append(jnp.exp(mid_row))
        to_end.append(jnp.exp(last_row - mid_row))
    b_mid = jnp.concatenate(b_mid, axis=0)
    spread = lambda rows_: jnp.concatenate(
        [jnp.broadcast_to(r, (HGRN_CHUNK, HGRN_DIM)) for r in rows_], axis=0)

    q_up = qs * jnp.exp(b - b_mid)
    k_dn = k * jnp.exp(b_mid - b)
    qm = q_up.astype(BF16)
    km = k_dn.astype(BF16)
    kp = (k_dn * spread(to_end)).astype(BF16)
    qd = (q_up * spread(from_start)).astype(BF16)
    vb = v.astype(BF16)

    scores = lax.dot_general(qm, km, NT_DIMS, preferred_element_type=F32)
    scores = jnp.where(tri > 0, scores, 0.0).astype(BF16)
    o_intra = jnp.dot(scores, vb, preferred_element_type=F32)

    chunks = []
    for ci, sl in enumerate(chunk_rows):
        d = ci // nc
        dec = from_start[ci] * to_end[ci]
        inc = lax.dot_general(vb[sl], kp[sl], TN_DIMS, preferred_element_type=F32)
        chunks.append((qd[sl], dec, inc))
    return o_intra, chunks


def _hgrn_scan(prepared, state_refs):
    heads = len(prepared)
    nc = len(prepared[0][1]) // 2
    states = [[state_refs[d][h] for h in range(heads)] for d in range(2)]
    o_inter = [[None] * (2 * nc) for _ in range(heads)]
    for step in range(nc):
        for h in range(heads):
            for d in range(2):
                ci = d * nc + (step if d == 0 else nc - 1 - step)
                qd, dec, inc = prepared[h][1][ci]
                o_inter[h][ci] = lax.dot_general(qd, states[d][h].astype(BF16), NT_DIMS,
                                                 preferred_element_type=F32)
                states[d][h] = states[d][h] * dec + inc
    for d in range(2):
        for h in range(heads):
            state_refs[d][h] = states[d][h]
    return [prepared[h][0] + jnp.concatenate(o_inter[h], axis=0) for h in range(heads)]


def _hgrn_kernel(qf_ref, xf_ref, vf_ref, qr_ref, xr_ref, vr_ref, lbp_ref, tri_ref,
                 of_ref, or_ref, sf_ref, sr_ref, *, layer):
    @pl.when(pl.program_id(2) == 0)
    def _():
        sf_ref[...] = jnp.zeros_like(sf_ref)
        sr_ref[...] = jnp.zeros_like(sr_ref)

    rows = [lbp_ref[l] for l in range(DEPTH)]
    top = functools.reduce(jnp.maximum, rows)
    e = [jnp.exp(r - top) for r in rows]
    den = functools.reduce(lambda u, w: u + w, e)
    lb = jnp.zeros(rows[0].shape, F32)
    for l in range(1, layer + 1):
        lb = lb + e[l] / den

    tb = qf_ref.shape[0]
    heads = qf_ref.shape[1] // HGRN_DIM
    tri = tri_ref[...]
    prepared = []
    for h in range(heads):
        lanes = slice(h * HGRN_DIM, (h + 1) * HGRN_DIM)
        stack = lambda f_ref, r_ref: jnp.concatenate([f_ref[:, lanes], r_ref[:, lanes]], axis=0)
        prepared.append(_hgrn_prepare(stack(qf_ref, qr_ref), stack(xf_ref, xr_ref),
                                      stack(vf_ref, vr_ref), lb[:, lanes], tri))
    outs = _hgrn_scan(prepared, (sf_ref, sr_ref))
    for h, out in enumerate(outs):
        lanes = slice(h * HGRN_DIM, (h + 1) * HGRN_DIM)
        of_ref[:, lanes] = out[:tb].astype(BF16)
        or_ref[:, lanes] = out[tb:].astype(BF16)


def _hgrn(pa3, hgrn_lb, tri, layer, tb):
    bsz, seq, _ = pa3.shape
    nb = seq // tb
    groups = HGRN_HEADS // HGRN_HEADS_PER_STEP
    width = HGRN_HEADS_PER_STEP * HGRN_DIM

    def fwd(col):
        return pl.BlockSpec((None, tb, width), lambda b, h, i: (b, i, col * groups + h))

    def rev(col):
        return pl.BlockSpec((None, tb, width), lambda b, h, i: (b, nb - 1 - i, col * groups + h))

    out_f = pl.BlockSpec((None, tb, width), lambda b, h, i: (b, i, h))
    out_r = pl.BlockSpec((None, tb, width), lambda b, h, i: (b, nb - 1 - i, h))
    shape = jax.ShapeDtypeStruct((bsz, seq, HGRN_WIDTH), BF16)
    state = pltpu.VMEM((HGRN_HEADS_PER_STEP, HGRN_DIM, HGRN_DIM), F32)
    return pl.pallas_call(
        functools.partial(_hgrn_kernel, layer=layer),
        grid=(bsz, groups, nb),
        in_specs=[fwd(0), fwd(1), fwd(3), rev(0), rev(2), rev(3),
                  pl.BlockSpec((DEPTH, 2, width), lambda b, h, i: (0, 0, h)),
                  _const_spec((2 * tb, 2 * tb))],
        out_specs=[out_f, out_r],
        out_shape=[shape, shape],
        scratch_shapes=[state, state],
        compiler_params=_params(("parallel", "parallel", "arbitrary")),
        name="hgrn2",
    )(pa3, pa3, pa3, pa3, pa3, pa3, hgrn_lb, tri)


def _flash_kernel(*refs, maps, n_v, n_out, tk):
    n_maps = len(maps)
    qt_ref, k_ref = refs[0], refs[1]
    v_refs = refs[2:2 + n_v]
    out_refs = refs[2 + n_v:2 + n_v + n_out]
    scratch = refs[2 + n_v + n_out:]
    acc_ref, m_ref, gap_ref, qv_ref = scratch[:4]
    per_kind = 2 * n_maps
    s_ref, p_ref, al_ref = (
        [scratch[4 + kind * per_kind + slot * n_maps:4 + kind * per_kind + (slot + 1) * n_maps]
         for slot in range(2)] for kind in range(3))
    tq = qt_ref.shape[1]
    n_chunks = k_ref.shape[0] // tk
    sub = m_ref.shape[1]

    def scores(ci, m):
        off = pl.multiple_of(ci * tk, tk)
        return jnp.dot(k_ref[pl.ds(off, tk), :], qv_ref[m], preferred_element_type=F32)

    def column_max(s):
        top = jnp.max(s.reshape(s.shape[0] // sub, sub, tq), axis=0)
        return jnp.broadcast_to(jnp.max(top, axis=0, keepdims=True), (sub, tq))

    def lagged_softmax(ci, slot):
        for m in range(n_maps):
            shift = m_ref[m]
            s = scores(ci, m)
            p_ref[slot][m][...] = jnp.exp2(s - shift[0:1, :]).astype(BF16)
            top = column_max(s)
            gap_ref[m] = jnp.maximum(gap_ref[m], top - shift)
            m_next = jnp.maximum(shift, top)
            m_ref[m] = m_next
            al_ref[slot][m][...] = jnp.exp2(shift - m_next)

    def lagged_pv(ci, slot):
        for m, (_, _, vi, _, _) in enumerate(maps):
            acc_ref[m] = (acc_ref[m] + jnp.dot(v_refs[vi][ci], p_ref[slot][m][...],
                                               preferred_element_type=F32)
                          ) * al_ref[slot][m][0:1, :]

    def lagged_trip(t, carry):
        first = t * CHUNKS_PER_TRIP
        lagged_softmax(first, 0)
        for u in range(1, CHUNKS_PER_TRIP):
            lagged_softmax(first + u, u % 2)
            lagged_pv(first + u - 1, (u - 1) % 2)
        lagged_pv(first + CHUNKS_PER_TRIP - 1, (CHUNKS_PER_TRIP - 1) % 2)
        return carry

    def exact_chunk(ci, slot):
        for m, (_, _, vi, _, _) in enumerate(maps):
            s_ref[slot][m][...] = scores(ci, m)
        for m, (_, _, vi, _, _) in enumerate(maps):
            m_prev = m_ref[m]
            m_next = jnp.maximum(m_prev, column_max(s_ref[slot][m][...]))
            m_ref[m] = m_next
            for c0 in range(0, tq, COL_BLOCK):
                cols = slice(c0, c0 + COL_BLOCK)
                p_ref[slot][m][:, cols] = jnp.exp2(
                    s_ref[slot][m][:, cols] - m_next[0:1, cols]).astype(BF16)
            acc_ref[m] = acc_ref[m] * jnp.exp2(m_prev - m_next)[0:1, :] + jnp.dot(
                v_refs[vi][ci], p_ref[slot][m][...], preferred_element_type=F32)

    def key_loop(chunk_fn):
        def body(t, carry):
            for u in range(CHUNKS_PER_TRIP):
                chunk_fn(t * CHUNKS_PER_TRIP + u, u % 2)
            return carry
        lax.fori_loop(0, n_chunks // CHUNKS_PER_TRIP, body, 0)

    qt = qt_ref[...]
    row = lax.broadcasted_iota(jnp.int32, qt.shape, 0)
    for m, (lo, hi, _, _, _) in enumerate(maps):
        qv_ref[m] = jnp.where((row >= lo) & (row < hi), qt, jnp.zeros_like(qt))

    acc_ref[...] = jnp.zeros_like(acc_ref)
    gap_ref[...] = jnp.zeros_like(gap_ref)
    for m in range(n_maps):
        m_ref[m] = column_max(jnp.dot(k_ref[0:LANES, :], qv_ref[m], preferred_element_type=F32))
    lax.fori_loop(0, n_chunks // CHUNKS_PER_TRIP, lagged_trip, 0)

    @pl.when(jnp.logical_not(jnp.max(gap_ref[...]) <= MAX_LAG_LOG2))
    def _():
        acc_ref[...] = jnp.zeros_like(acc_ref)
        m_ref[...] = jnp.full_like(m_ref, NEG_BIG)
        key_loop(exact_chunk)

    for oi, out_ref in enumerate(out_refs):
        for m, (_, _, _, mo, half) in enumerate(maps):
            if mo == oi:
                acc = acc_ref[m]
                out_ref[half * GQA_HEAD_DIM:(half + 1) * GQA_HEAD_DIM, :] = (
                    acc[0:GQA_HEAD_DIM, :] * (1.0 / acc[GQA_HEAD_DIM:GQA_HEAD_DIM + 1, :])
                ).astype(BF16)


def _flash(q3, k4, k_index, vt5, v_indices, maps, n_out, tq, tk, name):
    bsz, seq, qw = q3.shape
    nqb = qw // LANES
    n_maps = len(maps)
    n_v = len(v_indices(0))
    per_kind = 2 * n_maps
    once = pl.Buffered(1)
    qt3 = jnp.swapaxes(q3, 1, 2)

    def kmap(b, j, qi):
        n, lb = k_index(b, j)
        return (n, b, 0, lb)

    kspec = pl.BlockSpec((None, None, seq, LANES), kmap, pipeline_mode=once)
    vspecs = [pl.BlockSpec((None, None, seq // tk, VALUE_ROWS, tk),
                           lambda b, j, qi, n=n: (v_indices(j)[n], b, 0, 0, 0), pipeline_mode=once)
              for n in range(n_v)]
    qspec = pl.BlockSpec((None, LANES, tq), lambda b, j, qi: (b, j, qi))
    outs = pl.pallas_call(
        functools.partial(_flash_kernel, maps=maps, n_v=n_v, n_out=n_out, tk=tk),
        grid=(bsz, nqb, seq // tq),
        in_specs=[qspec, kspec] + vspecs,
        out_specs=[qspec] * n_out,
        out_shape=[jax.ShapeDtypeStruct((bsz, qw, seq), BF16)] * n_out,
        scratch_shapes=([pltpu.VMEM((n_maps, VALUE_ROWS, tq), F32),
                         pltpu.VMEM((n_maps, SUBLANES, tq), F32),
                         pltpu.VMEM((n_maps, SUBLANES, tq), F32),
                         pltpu.VMEM((n_maps, LANES, tq), BF16)]
                        + [pltpu.VMEM((tk, tq), F32)] * per_kind
                        + [pltpu.VMEM((tk, tq), BF16)] * per_kind
                        + [pltpu.VMEM((SUBLANES, tq), F32)] * per_kind),
        compiler_params=_params(("parallel", "parallel", "parallel")),
        name=name,
    )(qt3, k4, *([vt5] * n_v))
    return [jnp.swapaxes(o, 1, 2) for o in outs]


def _augment_values(v3, n_heads, tk):
    bsz, seq, _ = v3.shape
    ones = jnp.ones((bsz, seq, 1), v3.dtype)
    zeros = jnp.zeros((bsz, seq, VALUE_ROWS - GQA_HEAD_DIM - 1), v3.dtype)
    heads = []
    for h in range(n_heads):
        va = jnp.concatenate([v3[..., h * GQA_HEAD_DIM:(h + 1) * GQA_HEAD_DIM], ones, zeros], axis=-1)
        heads.append(va.reshape(bsz, seq // tk, tk, VALUE_ROWS).swapaxes(2, 3))
    return jnp.stack(heads)


GQA_MAPS = ((0, GQA_HEAD_DIM, 0, 0, 0), (GQA_HEAD_DIM, LANES, 0, 0, 1))
DIFF_MAPS = tuple((m * DIFF_HEAD_DIM, (m + 1) * DIFF_HEAD_DIM, m // 2, m % 2, m // 2)
                  for m in range(4))


def _outproj_kernel(x_ref, of_ref, or_ref, ob_ref, oc1_ref, oc2_ref, sg_ref, wo_ref,
                    hnw_ref, dnw_ref, lam_ref, postw_ref, g128_ref, g64_ref, out_ref, *, lam_init):
    sg = sg_ref[...].astype(F32)
    a = of_ref[...].astype(F32) + or_ref[...].astype(F32)
    ms = jnp.dot((a * a).astype(BF16), g128_ref[...], preferred_element_type=F32)
    mix_a = a * lax.rsqrt(ms + NORM_EPS) * hnw_ref[...] * sg[:, :HGRN_WIDTH]
    mix_b = ob_ref[...].astype(F32) * sg[:, HGRN_WIDTH:HGRN_WIDTH + GQA_WIDTH]

    lp = lam_ref[...]
    lam = (jnp.exp(jnp.sum(lp[0:1] * lp[1:2], axis=-1, keepdims=True))
           - jnp.exp(jnp.sum(lp[2:3] * lp[3:4], axis=-1, keepdims=True)) + lam_init)
    c = oc1_ref[...].astype(F32) - lam * oc2_ref[...].astype(F32)
    ms = jnp.dot((c * c).astype(BF16), g64_ref[...], preferred_element_type=F32)
    mix_c = (c * lax.rsqrt(ms + NORM_EPS) * dnw_ref[...] * (1.0 - lam_init)
             * sg[:, HGRN_WIDTH + GQA_WIDTH:])

    y = (jnp.dot(mix_a.astype(BF16), wo_ref[0:HGRN_WIDTH, :], preferred_element_type=F32)
         + jnp.dot(mix_b.astype(BF16), wo_ref[HGRN_WIDTH:HGRN_WIDTH + GQA_WIDTH, :],
                   preferred_element_type=F32)
         + jnp.dot(mix_c.astype(BF16), wo_ref[HGRN_WIDTH + GQA_WIDTH:, :],
                   preferred_element_type=F32))
    out_ref[...] = x_ref[...] + (y * lax.rsqrt(jnp.mean(y * y, axis=-1, keepdims=True) + NORM_EPS)
                                 * postw_ref[...])


def _outproj(x2, of2, or2, ob2, oc1, oc2, sg, wo_bf, hnw, dnw, lam_p, postw, g128, g64, lam_init, tm):
    t = x2.shape[0]
    tok = lambda w: pl.BlockSpec((tm, w), lambda i: (i, 0))
    return pl.pallas_call(
        functools.partial(_outproj_kernel, lam_init=lam_init),
        grid=(t // tm,),
        in_specs=[tok(D_MODEL), tok(HGRN_WIDTH), tok(HGRN_WIDTH), tok(GQA_WIDTH), tok(DIFF_WIDTH),
                  tok(DIFF_WIDTH), tok(MIX_WIDTH), _const_spec((MIX_WIDTH, D_MODEL)),
                  _const_spec((1, HGRN_WIDTH)), _const_spec((1, DIFF_WIDTH)),
                  _const_spec((4, DIFF_HEAD_DIM)), _const_spec((1, D_MODEL)),
                  _const_spec((HGRN_WIDTH, HGRN_WIDTH)), _const_spec((DIFF_WIDTH, DIFF_WIDTH))],
        out_specs=tok(D_MODEL),
        out_shape=jax.ShapeDtypeStruct((t, D_MODEL), F32),
        compiler_params=_params(("parallel",)),
        name="outproj",
    )(x2, of2, or2, ob2, oc1, oc2, sg, wo_bf, hnw, dnw, lam_p, postw, g128, g64)


def _group_mean_matrix(width, group):
    idx = jnp.arange(width) // group
    return ((idx[:, None] == idx[None, :]).astype(F32) / group).astype(BF16)


def _rope_tables(seq_len):
    half = ROPE_DIM // 2
    inv = jnp.power(ROPE_THETA, -jnp.arange(0, ROPE_DIM, 2, dtype=F32) / ROPE_DIM)
    pos = jnp.arange(seq_len, dtype=F32)
    rows = seq_len // GRID_W
    row_pos = jnp.repeat(jnp.arange(rows, dtype=F32), GRID_W)
    col_pos = jnp.tile(jnp.arange(GRID_W, dtype=F32), rows)
    sign = jnp.concatenate([-jnp.ones((half,), F32), jnp.ones((half,), F32)])

    def tab(p):
        ang = p[:, None] * inv[None, :]
        ang = jnp.concatenate([ang, ang], axis=-1)
        return jnp.cos(ang), jnp.sin(ang) * sign

    cr, sr = tab(row_pos)
    cc, sc = tab(col_pos)
    c1, s1 = tab(pos)
    rep_b = GQA_WIDTH // (2 * ROPE_DIM)
    rep_c = DIFF_WIDTH // ROPE_DIM
    return (jnp.tile(jnp.concatenate([cr, cc], axis=-1), (1, rep_b)),
            jnp.tile(jnp.concatenate([sr, sc], axis=-1), (1, rep_b)),
            jnp.tile(c1, (1, rep_c)), jnp.tile(s1, (1, rep_c)))


def _chunk_tri(tb):
    r = jnp.arange(2 * tb)
    same = (r[:, None] // HGRN_CHUNK) == (r[None, :] // HGRN_CHUNK)
    tri = jnp.where(r[:, None] < tb, r[None, :] <= r[:, None], r[None, :] >= r[:, None])
    return (same & tri).astype(BF16)


def _trunk(x, pre_norm_w, w_in_bf, hgrn_lb, hgrn_norm_w, gqa_q_norm_w, gqa_k_norm_w,
           diff_lambda, diff_norm_w, w_out_bf, post_norm_w, *, tabs, g64, g128, tri,
           tm, tb, gqa_tile, diff_tile):
    bsz, seq, _ = x.shape
    t = bsz * seq
    x2 = x.reshape(t, D_MODEL)
    for layer in range(DEPTH):
        qnw = jnp.tile(gqa_q_norm_w[layer], GQA_HEADS)[None, :]
        knw = jnp.tile(gqa_k_norm_w[layer], GQA_KV_HEADS)[None, :]
        pa, qb, kb, vb, qc, kc, vc, sg = _inproj(
            x2, seq, pre_norm_w[layer][None, :], w_in_bf[layer], tabs, qnw, knw, g64, tm)

        o_f, o_r = _hgrn(pa.reshape(bsz, seq, A_COLS), hgrn_lb, tri, layer, tb)

        kb3 = kb.reshape(bsz, seq, -1)
        kdup = jnp.stack([jnp.concatenate([kb3[..., h * GQA_HEAD_DIM:(h + 1) * GQA_HEAD_DIM]] * 2, axis=-1)
                          for h in range(GQA_KV_HEADS)])
        (ob,) = _flash(qb.reshape(bsz, seq, -1), kdup, lambda b, j: (j, 0),
                       _augment_values(vb.reshape(bsz, seq, -1), GQA_KV_HEADS, gqa_tile[1]), lambda j: (j,),
                       GQA_MAPS, 1, *gqa_tile, "flash_gqa")

        oc1, oc2 = _flash(qc.reshape(bsz, seq, -1), kc.reshape(1, bsz, seq, -1), lambda b, j: (0, j),
                          _augment_values(vc.reshape(bsz, seq, -1), DIFF_HEADS, diff_tile[1]),
                          lambda j: (2 * j, 2 * j + 1), DIFF_MAPS, 2, *diff_tile, "flash_diff")

        lam_init = 0.8 - 0.6 * math.exp(-0.3 * layer)
        x2 = _outproj(x2, o_f.reshape(t, -1), o_r.reshape(t, -1), ob.reshape(t, -1),
                      oc1.reshape(t, -1), oc2.reshape(t, -1), sg, w_out_bf[layer],
                      hgrn_norm_w[layer][None, :],
                      jnp.tile(diff_norm_w[layer], DIFF_HEADS)[None, :],
                      diff_lambda[layer], post_norm_w[layer][None, :], g128, g64, lam_init, tm)
    return x2.reshape(bsz, seq, D_MODEL)


def kernel(x_prompt, x_sample, pre_norm_w, w_in, hgrn_lb, hgrn_norm_w, gqa_q_norm_w, gqa_k_norm_w,
           diff_lambda, diff_norm_w, w_out, post_norm_w):
    w_in_bf = w_in.astype(BF16)
    w_out_bf = w_out.astype(BF16)
    longest = max(x_prompt.shape[1], x_sample.shape[1])
    run = functools.partial(_trunk, pre_norm_w=pre_norm_w, w_in_bf=w_in_bf, hgrn_lb=hgrn_lb,
                            hgrn_norm_w=hgrn_norm_w, gqa_q_norm_w=gqa_q_norm_w,
                            gqa_k_norm_w=gqa_k_norm_w, diff_lambda=diff_lambda,
                            diff_norm_w=diff_norm_w, w_out_bf=w_out_bf, post_norm_w=post_norm_w,
                            tabs=_rope_tables(longest),
                            g64=_group_mean_matrix(GQA_WIDTH, GQA_HEAD_DIM),
                            g128=_group_mean_matrix(HGRN_WIDTH, HGRN_DIM),
                            tri=_chunk_tri(TILES["hgrn_block"]),
                            tm=TILES["proj_rows"], tb=TILES["hgrn_block"],
                            gqa_tile=TILES["gqa"], diff_tile=TILES["diff"])
    return (run(x_prompt), run(x_sample))
```

```python
import functools
import math

import jax
import jax.numpy as jnp
from jax import lax
from jax.experimental import pallas as pl
from jax.experimental.pallas import tpu as pltpu

F32 = jnp.float32
BF16 = jnp.bfloat16

D_MODEL = 1024
DEPTH = 2
GRID_W = 64
HGRN_HEADS = 4
HGRN_DIM = 128
HGRN_WIDTH = HGRN_HEADS * HGRN_DIM
HGRN_CHUNK = 32
HGRN_HEADS_PER_STEP = 4
GQA_HEADS = 4
GQA_KV_HEADS = 2
GQA_HEAD_DIM = 64
GQA_WIDTH = GQA_HEADS * GQA_HEAD_DIM
GQA_KV_WIDTH = GQA_KV_HEADS * GQA_HEAD_DIM
DIFF_HEADS = 4
DIFF_HEAD_DIM = 32
DIFF_WIDTH = DIFF_HEADS * 2 * DIFF_HEAD_DIM
MIX_WIDTH = HGRN_WIDTH + GQA_WIDTH + DIFF_WIDTH
ROPE_THETA = 10000.0
ROPE_DIM = 32
NORM_EPS = 1e-6

A_COLS = 4 * HGRN_WIDTH
OFF_AG = A_COLS
OFF_B = OFF_AG + HGRN_WIDTH
OFF_BG = OFF_B + GQA_WIDTH + 2 * GQA_KV_WIDTH
OFF_C = OFF_BG + GQA_WIDTH
IN_COLS = OFF_C + 4 * DIFF_WIDTH

LANES = 128
SUBLANES = 8
VMEM_LIMIT = 56 * 1024 * 1024
NEG_BIG = -1e30
LOG2E = math.log2(math.e)

VALUE_ROWS = 80
COL_BLOCK = 128
CHUNKS_PER_TRIP = 4
MAX_LAG_LOG2 = 100.0

TILES = {"proj_rows": 512, "hgrn_block": 256, "attn_queries": 1024}

NT_DIMS = (((1,), (1,)), ((), ()))
TN_DIMS = (((0,), (0,)), ((), ()))


def _params(sem):
    return pltpu.CompilerParams(dimension_semantics=sem, vmem_limit_bytes=VMEM_LIMIT)


def _const_spec(shape):
    nd = len(shape)
    return pl.BlockSpec(shape, lambda *_: (0,) * nd)


def _rot_half16(x):
    outs = []
    for s in range(x.shape[1] // LANES):
        xs = x[:, s * LANES:(s + 1) * LANES]
        up = pltpu.roll(xs, ROPE_DIM // 2, 1)
        dn = pltpu.roll(xs, LANES - ROPE_DIM // 2, 1)
        lane = lax.broadcasted_iota(jnp.int32, xs.shape, 1)
        outs.append(jnp.where((lane & (ROPE_DIM - 1)) < ROPE_DIM // 2, dn, up))
    return outs[0] if len(outs) == 1 else jnp.concatenate(outs, axis=1)


def _silu(x):
    return x * (1.0 / (1.0 + jnp.exp(-x)))


def _inproj_kernel(x_ref, pw_ref, w_ref, cosb_ref, sinb_ref, cosc_ref, sinc_ref,
                   qnw_ref, knw_ref, g64_ref,
                   pa_ref, qbt_ref, kd_ref, vbt_ref, qct_ref, kc_ref, vct_ref, sg_ref):
    x = x_ref[...]
    h = x * lax.rsqrt(jnp.mean(x * x, axis=-1, keepdims=True) + NORM_EPS) * pw_ref[...]
    hb = h.astype(BF16)

    def proj(lo, hi):
        return jnp.dot(hb, w_ref[:, lo:hi], preferred_element_type=F32)

    pa_ref[...] = proj(0, A_COLS)
    sg_ref[:, 0:HGRN_WIDTH] = _silu(proj(OFF_AG, OFF_B)).astype(BF16)

    def head_rms(t, w, g):
        ms = jnp.dot((t * t).astype(BF16), g, preferred_element_type=F32)
        return t * lax.rsqrt(ms + NORM_EPS) * w

    def rope(t, cos, sin):
        return t * cos + _rot_half16(t) * sin

    lane = lax.broadcasted_iota(jnp.int32, (x.shape[0], LANES), 1)
    low = lane < GQA_HEAD_DIM

    def store_values(v, vt_ref):
        for s in range(v.shape[1] // LANES):
            slab = v[:, s * LANES:(s + 1) * LANES]
            for half, src in enumerate((slab, pltpu.roll(slab, GQA_HEAD_DIM, 1))):
                aug = jnp.where(low, src, jnp.where(lane == GQA_HEAD_DIM, 1.0, 0.0))
                vt_ref[2 * s + half] = aug.T[0:VALUE_ROWS, :].astype(BF16)

    g64 = g64_ref[...]
    cosb = cosb_ref[...]
    sinb = sinb_ref[...]
    bq = head_rms(proj(OFF_B, OFF_B + GQA_WIDTH), qnw_ref[...], g64)
    qbt_ref[...] = (rope(bq, cosb, sinb) * (LOG2E / math.sqrt(GQA_HEAD_DIM))).T.astype(BF16)
    off_k = OFF_B + GQA_WIDTH
    bk = head_rms(proj(off_k, off_k + GQA_KV_WIDTH), knw_ref[...],
                  g64[:GQA_KV_WIDTH, :GQA_KV_WIDTH])
    k = rope(bk, cosb[:, :GQA_KV_WIDTH], sinb[:, :GQA_KV_WIDTH])
    k_swapped = pltpu.roll(k, GQA_HEAD_DIM, 1)
    kd_ref[0] = jnp.where(low, k, k_swapped).astype(BF16)
    kd_ref[1] = jnp.where(low, k_swapped, k).astype(BF16)
    off_v = off_k + GQA_KV_WIDTH
    store_values(proj(off_v, OFF_BG), vbt_ref)
    sg_ref[:, HGRN_WIDTH:HGRN_WIDTH + GQA_WIDTH] = _silu(proj(OFF_BG, OFF_C)).astype(BF16)

    cosc = cosc_ref[...]
    sinc = sinc_ref[...]
    cq = proj(OFF_C, OFF_C + DIFF_WIDTH)
    qct_ref[...] = (rope(cq, cosc, sinc) * (LOG2E / math.sqrt(DIFF_HEAD_DIM))).T.astype(BF16)
    ck = proj(OFF_C + DIFF_WIDTH, OFF_C + 2 * DIFF_WIDTH)
    kc_ref[...] = rope(ck, cosc, sinc).astype(BF16)
    store_values(proj(OFF_C + 2 * DIFF_WIDTH, OFF_C + 3 * DIFF_WIDTH), vct_ref)
    sg_ref[:, HGRN_WIDTH + GQA_WIDTH:] = _silu(proj(OFF_C + 3 * DIFF_WIDTH, IN_COLS)).astype(BF16)


def _inproj(x2, bsz, seq_len, pre_w, w_in_bf, tabs, qnw, knw, g64, tm):
    t = x2.shape[0]
    tiles_per_seq = seq_len // tm
    tok = lambda w: pl.BlockSpec((tm, w), lambda i: (i, 0))
    tab = pl.BlockSpec((tm, GQA_WIDTH), lambda i: (i % tiles_per_seq, 0))
    qt_spec = pl.BlockSpec((None, GQA_WIDTH, tm),
                           lambda i: (i // tiles_per_seq, 0, i % tiles_per_seq))
    qt_shape = jax.ShapeDtypeStruct((bsz, GQA_WIDTH, seq_len), BF16)

    def vt(heads):
        return (pl.BlockSpec((heads, None, None, VALUE_ROWS, tm),
                             lambda i: (0, i // tiles_per_seq, i % tiles_per_seq, 0, 0)),
                jax.ShapeDtypeStruct((heads, bsz, tiles_per_seq, VALUE_ROWS, tm), BF16))

    outs = [(tok(A_COLS), jax.ShapeDtypeStruct((t, A_COLS), F32)),
            (qt_spec, qt_shape),
            (pl.BlockSpec((GQA_KV_HEADS, tm, LANES), lambda i: (0, i, 0)),
             jax.ShapeDtypeStruct((GQA_KV_HEADS, t, LANES), BF16)),
            vt(GQA_KV_HEADS),
            (qt_spec, qt_shape),
            (tok(DIFF_WIDTH), jax.ShapeDtypeStruct((t, DIFF_WIDTH), BF16)),
            vt(DIFF_HEADS),
            (tok(MIX_WIDTH), jax.ShapeDtypeStruct((t, MIX_WIDTH), BF16))]
    return pl.pallas_call(
        _inproj_kernel,
        grid=(t // tm,),
        in_specs=[tok(D_MODEL), _const_spec((1, D_MODEL)), _const_spec((D_MODEL, IN_COLS)),
                  tab, tab, tab, tab,
                  _const_spec((1, GQA_WIDTH)), _const_spec((1, GQA_KV_WIDTH)),
                  _const_spec((GQA_WIDTH, GQA_WIDTH))],
        out_specs=[o[0] for o in outs],
        out_shape=[o[1] for o in outs],
        compiler_params=_params(("parallel",)),
        name="inproj",
    )(x2, pre_w, w_in_bf, *tabs, qnw, knw, g64)


def _log1p(x):
    return jnp.log(1.0 + x)


def _hgrn_prepare(q, xf, v, lb, tri):
    tb = q.shape[0] // 2
    nc = tb // HGRN_CHUNK

    def rows(t):
        return jnp.concatenate([jnp.broadcast_to(t[0:1], (tb, HGRN_DIM)),
                                jnp.broadcast_to(t[1:2], (tb, HGRN_DIM))], axis=0)

    e = jnp.exp(-jnp.abs(xf))
    log_sig = jnp.minimum(xf, 0.0) - jnp.log(1.0 + e)
    c = rows(_log1p(-lb)) + log_sig
    a = rows(jnp.log(lb))
    g = jnp.maximum(a, c) + _log1p(jnp.exp(-jnp.abs(a - c)))
    k = rows(1.0 - lb) * jnp.where(xf > 0.0, e, 1.0) * (1.0 / (1.0 + e))
    qs = _silu(q)

    g_hi = g.astype(BF16)
    g_lo = (g - g_hi.astype(F32)).astype(BF16)
    b2 = jnp.dot(tri, jnp.concatenate([g_hi, g_lo], axis=1), preferred_element_type=F32)
    b = b2[:, :HGRN_DIM] + b2[:, HGRN_DIM:]

    mid = (HGRN_CHUNK // 2, HGRN_CHUNK // 2 - 1)
    last = (HGRN_CHUNK - 1, 0)
    chunk_rows = [slice(r0, r0 + HGRN_CHUNK) for r0 in range(0, 2 * tb, HGRN_CHUNK)]
    b_mid, from_start, to_end = [], [], []
    for ci, sl in enumerate(chunk_rows):
        d = ci // nc
        mid_row = b[sl.start + mid[d]:sl.start + mid[d] + 1, :]
        last_row = b[sl.start + last[d]:sl.start + last[d] + 1, :]
        b_mid.append(jnp.broadcast_to(mid_row, (HGRN_CHUNK, HGRN_DIM)))
        from_start.append(jnp.exp(mid_row))
        to_end.append(jnp.exp(last_row - mid_row))
    b_mid = jnp.concatenate(b_mid, axis=0)
    spread = lambda rows_: jnp.concatenate(
        [jnp.broadcast_to(r, (HGRN_CHUNK, HGRN_DIM)) for r in rows_], axis=0)

    q_up = qs * jnp.exp(b - b_mid)
    k_dn = k * jnp.exp(b_mid - b)
    qm = q_up.astype(BF16)
    km = k_dn.astype(BF16)
    kp = (k_dn * spread(to_end)).astype(BF16)
    qd = (q_up * spread(from_start)).astype(BF16)
    vb = v.astype(BF16)

    scores = lax.dot_general(qm, km, NT_DIMS, preferred_element_type=F32)
    scores = jnp.where(tri > 0, scores, 0.0).astype(BF16)
    o_intra = jnp.dot(scores, vb, preferred_element_type=F32)

    chunks = []
    for ci, sl in enumerate(chunk_rows):
        dec = from_start[ci] * to_end[ci]
        inc = lax.dot_general(vb[sl], kp[sl], TN_DIMS, preferred_element_type=F32)
        chunks.append((qd[sl], dec, inc))
    return o_intra, chunks


def _hgrn_scan(prepared, state_refs):
    heads = len(prepared)
    nc = len(prepared[0][1]) // 2
    states = [[state_refs[d][h] for h in range(heads)] for d in range(2)]
    o_inter = [[None] * (2 * nc) for _ in range(heads)]
    for step in range(nc):
        for h in range(heads):
            for d in range(2):
                ci = d * nc + (step if d == 0 else nc - 1 - step)
                qd, dec, inc = prepared[h][1][ci]
                o_inter[h][ci] = lax.dot_general(qd, states[d][h].astype(BF16), NT_DIMS,
                                                 preferred_element_type=F32)
                states[d][h] = states[d][h] * dec + inc
    for d in range(2):
        for h in range(heads):
            state_refs[d][h] = states[d][h]
    return [prepared[h][0] + jnp.concatenate(o_inter[h], axis=0) for h in range(heads)]


def _hgrn_kernel(qf_ref, xf_ref, vf_ref, qr_ref, xr_ref, vr_ref, lbp_ref, tri_ref,
                 of_ref, or_ref, sf_ref, sr_ref, *, layer):
    @pl.when(pl.program_id(2) == 0)
    def _():
        sf_ref[...] = jnp.zeros_like(sf_ref)
        sr_ref[...] = jnp.zeros_like(sr_ref)

    rows = [lbp_ref[l] for l in range(DEPTH)]
    top = functools.reduce(jnp.maximum, rows)
    e = [jnp.exp(r - top) for r in rows]
    den = functools.reduce(lambda u, w: u + w, e)
    lb = jnp.zeros(rows[0].shape, F32)
    for l in range(1, layer + 1):
        lb = lb + e[l] / den

    tb = qf_ref.shape[0]
    heads = qf_ref.shape[1] // HGRN_DIM
    tri = tri_ref[...]
    prepared = []
    for h in range(heads):
        lanes = slice(h * HGRN_DIM, (h + 1) * HGRN_DIM)
        stack = lambda f_ref, r_ref: jnp.concatenate([f_ref[:, lanes], r_ref[:, lanes]], axis=0)
        prepared.append(_hgrn_prepare(stack(qf_ref, qr_ref), stack(xf_ref, xr_ref),
                                      stack(vf_ref, vr_ref), lb[:, lanes], tri))
    outs = _hgrn_scan(prepared, (sf_ref, sr_ref))
    for h, out in enumerate(outs):
        lanes = slice(h * HGRN_DIM, (h + 1) * HGRN_DIM)
        of_ref[:, lanes] = out[:tb].astype(BF16)
        or_ref[:, lanes] = out[tb:].astype(BF16)


def _hgrn(pa3, hgrn_lb, tri, layer, tb):
    bsz, seq, _ = pa3.shape
    nb = seq // tb
    groups = HGRN_HEADS // HGRN_HEADS_PER_STEP
    width = HGRN_HEADS_PER_STEP * HGRN_DIM

    def fwd(col):
        return pl.BlockSpec((None, tb, width), lambda b, h, i: (b, i, col * groups + h))

    def rev(col):
        return pl.BlockSpec((None, tb, width), lambda b, h, i: (b, nb - 1 - i, col * groups + h))

    out_f = pl.BlockSpec((None, tb, width), lambda b, h, i: (b, i, h))
    out_r = pl.BlockSpec((None, tb, width), lambda b, h, i: (b, nb - 1 - i, h))
    shape = jax.ShapeDtypeStruct((bsz, seq, HGRN_WIDTH), BF16)
    state = pltpu.VMEM((HGRN_HEADS_PER_STEP, HGRN_DIM, HGRN_DIM), F32)
    return pl.pallas_call(
        functools.partial(_hgrn_kernel, layer=layer),
        grid=(bsz, groups, nb),
        in_specs=[fwd(0), fwd(1), fwd(3), rev(0), rev(2), rev(3),
                  pl.BlockSpec((DEPTH, 2, width), lambda b, h, i: (0, 0, h)),
                  _const_spec((2 * tb, 2 * tb))],
        out_specs=[out_f, out_r],
        out_shape=[shape, shape],
        scratch_shapes=[state, state],
        compiler_params=_params(("parallel", "parallel", "arbitrary")),
        name="hgrn2",
    )(pa3, pa3, pa3, pa3, pa3, pa3, hgrn_lb, tri)


def _flash_kernel(*refs, maps, n_v, n_out, tk):
    n_maps = len(maps)
    qt_ref, k_ref = refs[0], refs[1]
    v_refs = refs[2:2 + n_v]
    out_refs = refs[2 + n_v:2 + n_v + n_out]
    scratch = refs[2 + n_v + n_out:]
    acc_ref, m_ref, gap_ref, qv_ref = scratch[:4]
    per_kind = 2 * n_maps
    s_ref, p_ref, al_ref = (
        [scratch[4 + kind * per_kind + slot * n_maps:4 + kind * per_kind + (slot + 1) * n_maps]
         for slot in range(2)] for kind in range(3))
    tq = qt_ref.shape[1]
    n_chunks = k_ref.shape[0] // tk
    sub = m_ref.shape[1]

    def scores(ci, m):
        off = pl.multiple_of(ci * tk, tk)
        return jnp.dot(k_ref[pl.ds(off, tk), :], qv_ref[m], preferred_element_type=F32)

    def column_max(s):
        top = jnp.max(s.reshape(s.shape[0] // sub, sub, tq), axis=0)
        return jnp.broadcast_to(jnp.max(top, axis=0, keepdims=True), (sub, tq))

    def lagged_softmax(ci, slot):
        for m in range(n_maps):
            shift = m_ref[m]
            s = scores(ci, m)
            p_ref[slot][m][...] = jnp.exp2(s - shift[0:1, :]).astype(BF16)
            top = column_max(s)
            gap_ref[m] = jnp.maximum(gap_ref[m], top - shift)
            m_next = jnp.maximum(shift, top)
            m_ref[m] = m_next
            al_ref[slot][m][...] = jnp.exp2(shift - m_next)

    def lagged_pv(ci, slot):
        for m, (_, _, vi, _, _) in enumerate(maps):
            acc_ref[m] = (acc_ref[m] + jnp.dot(v_refs[vi][ci], p_ref[slot][m][...],
                                               preferred_element_type=F32)
                          ) * al_ref[slot][m][0:1, :]

    def lagged_trip(t, carry):
        first = t * CHUNKS_PER_TRIP
        lagged_softmax(first, 0)
        for u in range(1, CHUNKS_PER_TRIP):
            lagged_softmax(first + u, u % 2)
            lagged_pv(first + u - 1, (u - 1) % 2)
        lagged_pv(first + CHUNKS_PER_TRIP - 1, (CHUNKS_PER_TRIP - 1) % 2)
        return carry

    def exact_chunk(ci, slot):
        for m in range(n_maps):
            s_ref[slot][m][...] = scores(ci, m)
        for m, (_, _, vi, _, _) in enumerate(maps):
            m_prev = m_ref[m]
            m_next = jnp.maximum(m_prev, column_max(s_ref[slot][m][...]))
            m_ref[m] = m_next
            for c0 in range(0, tq, COL_BLOCK):
                cols = slice(c0, c0 + COL_BLOCK)
                p_ref[slot][m][:, cols] = jnp.exp2(
                    s_ref[slot][m][:, cols] - m_next[0:1, cols]).astype(BF16)
            acc_ref[m] = acc_ref[m] * jnp.exp2(m_prev - m_next)[0:1, :] + jnp.dot(
                v_refs[vi][ci], p_ref[slot][m][...], preferred_element_type=F32)

    def exact_trip(t, carry):
        for u in range(CHUNKS_PER_TRIP):
            exact_chunk(t * CHUNKS_PER_TRIP + u, u % 2)
        return carry

    qt = qt_ref[...]
    row = lax.broadcasted_iota(jnp.int32, qt.shape, 0)
    for m, (lo, hi, _, _, _) in enumerate(maps):
        qv_ref[m] = jnp.where((row >= lo) & (row < hi), qt, jnp.zeros_like(qt))

    acc_ref[...] = jnp.zeros_like(acc_ref)
    gap_ref[...] = jnp.zeros_like(gap_ref)
    for m in range(n_maps):
        m_ref[m] = column_max(jnp.dot(k_ref[0:LANES, :], qv_ref[m], preferred_element_type=F32))
    lax.fori_loop(0, n_chunks // CHUNKS_PER_TRIP, lagged_trip, 0)

    @pl.when(jnp.logical_not(jnp.max(gap_ref[...]) <= MAX_LAG_LOG2))
    def _():
        acc_ref[...] = jnp.zeros_like(acc_ref)
        m_ref[...] = jnp.full_like(m_ref, NEG_BIG)
        lax.fori_loop(0, n_chunks // CHUNKS_PER_TRIP, exact_trip, 0)

    for oi, out_ref in enumerate(out_refs):
        for m, (_, _, _, mo, half) in enumerate(maps):
            if mo == oi:
                acc = acc_ref[m]
                out_ref[half * GQA_HEAD_DIM:(half + 1) * GQA_HEAD_DIM, :] = (
                    acc[0:GQA_HEAD_DIM, :] * (1.0 / acc[GQA_HEAD_DIM:GQA_HEAD_DIM + 1, :])
                ).astype(BF16)


def _flash(qt3, k4, k_index, vt5, v_indices, maps, n_out, tq, name):
    bsz, qw, seq = qt3.shape
    tk = vt5.shape[-1]
    assert seq % tq == 0 and (seq // tk) % CHUNKS_PER_TRIP == 0, (seq, tq, tk)
    nqb = qw // LANES
    n_maps = len(maps)
    n_v = len(v_indices(0))
    per_kind = 2 * n_maps
    once = pl.Buffered(1)

    def kmap(b, j, qi):
        n, lb = k_index(b, j)
        return (n, b, 0, lb)

    kspec = pl.BlockSpec((None, None, seq, LANES), kmap, pipeline_mode=once)
    vspecs = [pl.BlockSpec((None, None, seq // tk, VALUE_ROWS, tk),
                           lambda b, j, qi, n=n: (v_indices(j)[n], b, 0, 0, 0), pipeline_mode=once)
              for n in range(n_v)]
    qspec = pl.BlockSpec((None, LANES, tq), lambda b, j, qi: (b, j, qi))
    return pl.pallas_call(
        functools.partial(_flash_kernel, maps=maps, n_v=n_v, n_out=n_out, tk=tk),
        grid=(bsz, nqb, seq // tq),
        in_specs=[qspec, kspec] + vspecs,
        out_specs=[qspec] * n_out,
        out_shape=[jax.ShapeDtypeStruct((bsz, qw, seq), BF16)] * n_out,
        scratch_shapes=([pltpu.VMEM((n_maps, VALUE_ROWS, tq), F32),
                         pltpu.VMEM((n_maps, SUBLANES, tq), F32),
                         pltpu.VMEM((n_maps, SUBLANES, tq), F32),
                         pltpu.VMEM((n_maps, LANES, tq), BF16)]
                        + [pltpu.VMEM((tk, tq), F32)] * per_kind
                        + [pltpu.VMEM((tk, tq), BF16)] * per_kind
                        + [pltpu.VMEM((SUBLANES, tq), F32)] * per_kind),
        compiler_params=_params(("parallel", "parallel", "parallel")),
        name=name,
    )(qt3, k4, *([vt5] * n_v))


GQA_MAPS = ((0, GQA_HEAD_DIM, 0, 0, 0), (GQA_HEAD_DIM, LANES, 0, 0, 1))
DIFF_MAPS = tuple((m * DIFF_HEAD_DIM, (m + 1) * DIFF_HEAD_DIM, m // 2, m % 2, m // 2)
                  for m in range(4))


def _outproj_kernel(x_ref, of_ref, or_ref, ob_ref, oc1_ref, oc2_ref, sg_ref, wo_ref,
                    hnw_ref, dnw_ref, lam_ref, postw_ref, g128_ref, g64_ref, out_ref, *, lam_init):
    sg = sg_ref[...].astype(F32)
    a = of_ref[...].astype(F32) + or_ref[...].astype(F32)
    ms = jnp.dot((a * a).astype(BF16), g128_ref[...], preferred_element_type=F32)
    mix_a = a * lax.rsqrt(ms + NORM_EPS) * hnw_ref[...] * sg[:, :HGRN_WIDTH]
    mix_b = ob_ref[...].astype(F32).T * sg[:, HGRN_WIDTH:HGRN_WIDTH + GQA_WIDTH]

    lp = lam_ref[...]
    lam = (jnp.exp(jnp.sum(lp[0:1] * lp[1:2], axis=-1, keepdims=True))
           - jnp.exp(jnp.sum(lp[2:3] * lp[3:4], axis=-1, keepdims=True)) + lam_init)
    c = (oc1_ref[...].astype(F32) - lam * oc2_ref[...].astype(F32)).T
    ms = jnp.dot((c * c).astype(BF16), g64_ref[...], preferred_element_type=F32)
    mix_c = (c * lax.rsqrt(ms + NORM_EPS) * dnw_ref[...] * (1.0 - lam_init)
             * sg[:, HGRN_WIDTH + GQA_WIDTH:])

    y = (jnp.dot(mix_a.astype(BF16), wo_ref[0:HGRN_WIDTH, :], preferred_element_type=F32)
         + jnp.dot(mix_b.astype(BF16), wo_ref[HGRN_WIDTH:HGRN_WIDTH + GQA_WIDTH, :],
                   preferred_element_type=F32)
         + jnp.dot(mix_c.astype(BF16), wo_ref[HGRN_WIDTH + GQA_WIDTH:, :],
                   preferred_element_type=F32))
    out_ref[...] = x_ref[...] + (y * lax.rsqrt(jnp.mean(y * y, axis=-1, keepdims=True) + NORM_EPS)
                                 * postw_ref[...])


def _outproj(x2, of2, or2, obt, oc1t, oc2t, sg, wo_bf, hnw, dnw, lam_p, postw, g128, g64,
             lam_init, tm):
    t = x2.shape[0]
    tiles_per_seq = obt.shape[2] // tm
    tok = lambda w: pl.BlockSpec((tm, w), lambda i: (i, 0))
    att = pl.BlockSpec((None, GQA_WIDTH, tm), lambda i: (i // tiles_per_seq, 0, i % tiles_per_seq))
    return pl.pallas_call(
        functools.partial(_outproj_kernel, lam_init=lam_init),
        grid=(t // tm,),
        in_specs=[tok(D_MODEL), tok(HGRN_WIDTH), tok(HGRN_WIDTH), att, att, att, tok(MIX_WIDTH),
                  _const_spec((MIX_WIDTH, D_MODEL)),
                  _const_spec((1, HGRN_WIDTH)), _const_spec((1, DIFF_WIDTH)),
                  _const_spec((4, DIFF_HEAD_DIM)), _const_spec((1, D_MODEL)),
                  _const_spec((HGRN_WIDTH, HGRN_WIDTH)), _const_spec((DIFF_WIDTH, DIFF_WIDTH))],
        out_specs=tok(D_MODEL),
        out_shape=jax.ShapeDtypeStruct((t, D_MODEL), F32),
        compiler_params=_params(("parallel",)),
        name="outproj",
    )(x2, of2, or2, obt, oc1t, oc2t, sg, wo_bf, hnw, dnw, lam_p, postw, g128, g64)


def _group_mean_matrix(width, group):
    idx = jnp.arange(width) // group
    return ((idx[:, None] == idx[None, :]).astype(F32) / group).astype(BF16)


def _rope_tables(seq_len):
    half = ROPE_DIM // 2
    inv = jnp.power(ROPE_THETA, -jnp.arange(0, ROPE_DIM, 2, dtype=F32) / ROPE_DIM)
    pos = jnp.arange(seq_len, dtype=F32)
    rows = seq_len // GRID_W
    row_pos = jnp.repeat(jnp.arange(rows, dtype=F32), GRID_W)
    col_pos = jnp.tile(jnp.arange(GRID_W, dtype=F32), rows)
    sign = jnp.concatenate([-jnp.ones((half,), F32), jnp.ones((half,), F32)])

    def tab(p):
        ang = p[:, None] * inv[None, :]
        ang = jnp.concatenate([ang, ang], axis=-1)
        return jnp.cos(ang), jnp.sin(ang) * sign

    cr, sr = tab(row_pos)
    cc, sc = tab(col_pos)
    c1, s1 = tab(pos)
    rep_b = GQA_WIDTH // (2 * ROPE_DIM)
    rep_c = DIFF_WIDTH // ROPE_DIM
    return (jnp.tile(jnp.concatenate([cr, cc], axis=-1), (1, rep_b)),
            jnp.tile(jnp.concatenate([sr, sc], axis=-1), (1, rep_b)),
            jnp.tile(c1, (1, rep_c)), jnp.tile(s1, (1, rep_c)))


def _chunk_tri(tb):
    r = jnp.arange(2 * tb)
    same = (r[:, None] // HGRN_CHUNK) == (r[None, :] // HGRN_CHUNK)
    tri = jnp.where(r[:, None] < tb, r[None, :] <= r[:, None], r[None, :] >= r[:, None])
    return (same & tri).astype(BF16)


def _trunk(x, pre_norm_w, w_in_bf, hgrn_lb, hgrn_norm_w, gqa_q_norm_w, gqa_k_norm_w,
           diff_lambda, diff_norm_w, w_out_bf, post_norm_w, *, tabs, g64, g128, tri,
           tm, tb, tq):
    bsz, seq, _ = x.shape
    t = bsz * seq
    x2 = x.reshape(t, D_MODEL)
    for layer in range(DEPTH):
        qnw = jnp.tile(gqa_q_norm_w[layer], GQA_HEADS)[None, :]
        knw = jnp.tile(gqa_k_norm_w[layer], GQA_KV_HEADS)[None, :]
        pa, qbt, kdup, vbt, qct, kc, vct, sg = _inproj(
            x2, bsz, seq, pre_norm_w[layer][None, :], w_in_bf[layer], tabs, qnw, knw, g64, tm)

        o_f, o_r = _hgrn(pa.reshape(bsz, seq, A_COLS), hgrn_lb, tri, layer, tb)

        (obt,) = _flash(qbt, kdup.reshape(GQA_KV_HEADS, bsz, seq, LANES), lambda b, j: (j, 0),
                        vbt, lambda j: (j,), GQA_MAPS, 1, tq, "flash_gqa")

        oc1t, oc2t = _flash(qct, kc.reshape(1, bsz, seq, -1), lambda b, j: (0, j),
                            vct, lambda j: (2 * j, 2 * j + 1), DIFF_MAPS, 2, tq, "flash_diff")

        lam_init = 0.8 - 0.6 * math.exp(-0.3 * layer)
        x2 = _outproj(x2, o_f.reshape(t, -1), o_r.reshape(t, -1), obt, oc1t, oc2t, sg,
                      w_out_bf[layer], hgrn_norm_w[layer][None, :],
                      jnp.tile(diff_norm_w[layer], DIFF_HEADS)[None, :],
                      diff_lambda[layer], post_norm_w[layer][None, :], g128, g64, lam_init, tm)
    return x2.reshape(bsz, seq, D_MODEL)


def kernel(x_prompt, x_sample, pre_norm_w, w_in, hgrn_lb, hgrn_norm_w, gqa_q_norm_w, gqa_k_norm_w,
           diff_lambda, diff_norm_w, w_out, post_norm_w):
    w_in_bf = w_in.astype(BF16)
    w_out_bf = w_out.astype(BF16)
    longest = max(x_prompt.shape[1], x_sample.shape[1])
    run = functools.partial(_trunk, pre_norm_w=pre_norm_w, w_in_bf=w_in_bf, hgrn_lb=hgrn_lb,
                            hgrn_norm_w=hgrn_norm_w, gqa_q_norm_w=gqa_q_norm_w,
                            gqa_k_norm_w=gqa_k_norm_w, diff_lambda=diff_lambda,
                            diff_norm_w=diff_norm_w, w_out_bf=w_out_bf, post_norm_w=post_norm_w,
                            tabs=_rope_tables(longest),
                            g64=_group_mean_matrix(GQA_WIDTH, GQA_HEAD_DIM),
                            g128=_group_mean_matrix(HGRN_WIDTH, HGRN_DIM),
                            tri=_chunk_tri(TILES["hgrn_block"]),
                            tm=TILES["proj_rows"], tb=TILES["hgrn_block"],
                            tq=TILES["attn_queries"])
    return (run(x_prompt), run(x_sample))
```

```python
import functools
import math

import jax
import jax.numpy as jnp
from jax import lax
from jax.experimental import pallas as pl
from jax.experimental.pallas import tpu as pltpu

F32 = jnp.float32
BF16 = jnp.bfloat16

D_MODEL = 1024
DEPTH = 2
GRID_W = 64
HGRN_HEADS = 4
HGRN_DIM = 128
HGRN_WIDTH = HGRN_HEADS * HGRN_DIM
HGRN_CHUNK = 32
HGRN_HEADS_PER_STEP = 4
GQA_HEADS = 4
GQA_KV_HEADS = 2
GQA_HEAD_DIM = 64
GQA_WIDTH = GQA_HEADS * GQA_HEAD_DIM
GQA_KV_WIDTH = GQA_KV_HEADS * GQA_HEAD_DIM
DIFF_HEADS = 4
DIFF_HEAD_DIM = 32
DIFF_WIDTH = DIFF_HEADS * 2 * DIFF_HEAD_DIM
MIX_WIDTH = HGRN_WIDTH + GQA_WIDTH + DIFF_WIDTH
ROPE_THETA = 10000.0
ROPE_DIM = 32
NORM_EPS = 1e-6

A_COLS = 4 * HGRN_WIDTH
OFF_AG = A_COLS
OFF_B = OFF_AG + HGRN_WIDTH
OFF_BG = OFF_B + GQA_WIDTH + 2 * GQA_KV_WIDTH
OFF_C = OFF_BG + GQA_WIDTH
IN_COLS = OFF_C + 4 * DIFF_WIDTH

LANES = 128
SUBLANES = 8
VMEM_LIMIT = 56 * 1024 * 1024
NEG_BIG = -1e30
LOG2E = math.log2(math.e)

VALUE_ROWS = 80
COL_BLOCK = 128
CHUNKS_PER_TRIP = 4
MAX_LAG_LOG2 = 100.0

TILES = {"proj_rows": 512, "hgrn_block": 256, "attn_queries": 1024}

NT_DIMS = (((1,), (1,)), ((), ()))
TN_DIMS = (((0,), (0,)), ((), ()))


def _params(sem):
    return pltpu.CompilerParams(dimension_semantics=sem, vmem_limit_bytes=VMEM_LIMIT)


def _const_spec(shape):
    nd = len(shape)
    return pl.BlockSpec(shape, lambda *_: (0,) * nd)


def _rot_half16(x):
    outs = []
    for s in range(x.shape[1] // LANES):
        xs = x[:, s * LANES:(s + 1) * LANES]
        up = pltpu.roll(xs, ROPE_DIM // 2, 1)
        dn = pltpu.roll(xs, LANES - ROPE_DIM // 2, 1)
        lane = lax.broadcasted_iota(jnp.int32, xs.shape, 1)
        outs.append(jnp.where((lane & (ROPE_DIM - 1)) < ROPE_DIM // 2, dn, up))
    return outs[0] if len(outs) == 1 else jnp.concatenate(outs, axis=1)


def _silu(x):
    return x * (1.0 / (1.0 + jnp.exp(-x)))


def _inproj_kernel(x_ref, pw_ref, w_ref, cosb_ref, sinb_ref, cosc_ref, sinc_ref,
                   qnw_ref, knw_ref, g64_ref,
                   pa_ref, qbt_ref, kd_ref, vbt_ref, qct_ref, kc_ref, vct_ref, sg_ref):
    x = x_ref[...]
    h = x * lax.rsqrt(jnp.mean(x * x, axis=-1, keepdims=True) + NORM_EPS) * pw_ref[...]
    hb = h.astype(BF16)

    def proj(lo, hi):
        return jnp.dot(hb, w_ref[:, lo:hi], preferred_element_type=F32)

    pa_ref[...] = proj(0, A_COLS)
    sg_ref[:, 0:HGRN_WIDTH] = _silu(proj(OFF_AG, OFF_B)).astype(BF16)

    def head_rms(t, w, g):
        ms = jnp.dot((t * t).astype(BF16), g, preferred_element_type=F32)
        return t * lax.rsqrt(ms + NORM_EPS) * w

    def rope(t, cos, sin):
        return t * cos + _rot_half16(t) * sin

    lane = lax.broadcasted_iota(jnp.int32, (x.shape[0], LANES), 1)
    low = lane < GQA_HEAD_DIM

    def store_values(v, vt_ref):
        for s in range(v.shape[1] // LANES):
            slab = v[:, s * LANES:(s + 1) * LANES]
            for half, src in enumerate((slab, pltpu.roll(slab, GQA_HEAD_DIM, 1))):
                aug = jnp.where(low, src, jnp.where(lane == GQA_HEAD_DIM, 1.0, 0.0))
                vt_ref[2 * s + half] = aug.T[0:VALUE_ROWS, :].astype(BF16)

    g64 = g64_ref[...]
    cosb = cosb_ref[...]
    sinb = sinb_ref[...]
    bq = head_rms(proj(OFF_B, OFF_B + GQA_WIDTH), qnw_ref[...], g64)
    qbt_ref[...] = (rope(bq, cosb, sinb) * (LOG2E / math.sqrt(GQA_HEAD_DIM))).T.astype(BF16)
    off_k = OFF_B + GQA_WIDTH
    bk = head_rms(proj(off_k, off_k + GQA_KV_WIDTH), knw_ref[...],
                  g64[:GQA_KV_WIDTH, :GQA_KV_WIDTH])
    k = rope(bk, cosb[:, :GQA_KV_WIDTH], sinb[:, :GQA_KV_WIDTH])
    k_swapped = pltpu.roll(k, GQA_HEAD_DIM, 1)
    kd_ref[0] = jnp.where(low, k, k_swapped).astype(BF16)
    kd_ref[1] = jnp.where(low, k_swapped, k).astype(BF16)
    off_v = off_k + GQA_KV_WIDTH
    store_values(proj(off_v, OFF_BG), vbt_ref)
    sg_ref[:, HGRN_WIDTH:HGRN_WIDTH + GQA_WIDTH] = _silu(proj(OFF_BG, OFF_C)).astype(BF16)

    cosc = cosc_ref[...]
    sinc = sinc_ref[...]
    cq = proj(OFF_C, OFF_C + DIFF_WIDTH)
    qct_ref[...] = (rope(cq, cosc, sinc) * (LOG2E / math.sqrt(DIFF_HEAD_DIM))).T.astype(BF16)
    ck = proj(OFF_C + DIFF_WIDTH, OFF_C + 2 * DIFF_WIDTH)
    kc_ref[...] = rope(ck, cosc, sinc).astype(BF16)
    store_values(proj(OFF_C + 2 * DIFF_WIDTH, OFF_C + 3 * DIFF_WIDTH), vct_ref)
    sg_ref[:, HGRN_WIDTH + GQA_WIDTH:] = _silu(proj(OFF_C + 3 * DIFF_WIDTH, IN_COLS)).astype(BF16)


def _inproj(x2, bsz, seq_len, pre_w, w_in_bf, tabs, qnw, knw, g64, tm):
    t = x2.shape[0]
    tiles_per_seq = seq_len // tm
    tok = lambda w: pl.BlockSpec((tm, w), lambda i: (i, 0))
    tab = pl.BlockSpec((tm, GQA_WIDTH), lambda i: (i % tiles_per_seq, 0))
    qt_spec = pl.BlockSpec((None, GQA_WIDTH, tm),
                           lambda i: (i // tiles_per_seq, 0, i % tiles_per_seq))
    qt_shape = jax.ShapeDtypeStruct((bsz, GQA_WIDTH, seq_len), BF16)

    def vt(heads):
        return (pl.BlockSpec((heads, None, None, VALUE_ROWS, tm),
                             lambda i: (0, i // tiles_per_seq, i % tiles_per_seq, 0, 0)),
                jax.ShapeDtypeStruct((heads, bsz, tiles_per_seq, VALUE_ROWS, tm), BF16))

    outs = [(tok(A_COLS), jax.ShapeDtypeStruct((t, A_COLS), F32)),
            (qt_spec, qt_shape),
            (pl.BlockSpec((GQA_KV_HEADS, tm, LANES), lambda i: (0, i, 0)),
             jax.ShapeDtypeStruct((GQA_KV_HEADS, t, LANES), BF16)),
            vt(GQA_KV_HEADS),
            (qt_spec, qt_shape),
            (tok(DIFF_WIDTH), jax.ShapeDtypeStruct((t, DIFF_WIDTH), BF16)),
            vt(DIFF_HEADS),
            (tok(MIX_WIDTH), jax.ShapeDtypeStruct((t, MIX_WIDTH), BF16))]
    return pl.pallas_call(
        _inproj_kernel,
        grid=(t // tm,),
        in_specs=[tok(D_MODEL), _const_spec((1, D_MODEL)), _const_spec((D_MODEL, IN_COLS)),
                  tab, tab, tab, tab,
                  _const_spec((1, GQA_WIDTH)), _const_spec((1, GQA_KV_WIDTH)),
                  _const_spec((GQA_WIDTH, GQA_WIDTH))],
        out_specs=[o[0] for o in outs],
        out_shape=[o[1] for o in outs],
        compiler_params=_params(("parallel",)),
        name="inproj",
    )(x2, pre_w, w_in_bf, *tabs, qnw, knw, g64)


def _log1p(x):
    return jnp.log(1.0 + x)


def _hgrn_prepare(q, xf, v, lb, tri):
    tb = q.shape[0] // 2
    nc = tb // HGRN_CHUNK

    def rows(t):
        return jnp.concatenate([jnp.broadcast_to(t[0:1], (tb, HGRN_DIM)),
                                jnp.broadcast_to(t[1:2], (tb, HGRN_DIM))], axis=0)

    e = jnp.exp(-jnp.abs(xf))
    log_sig = jnp.minimum(xf, 0.0) - jnp.log(1.0 + e)
    c = rows(_log1p(-lb)) + log_sig
    a = rows(jnp.log(lb))
    g = jnp.maximum(a, c) + _log1p(jnp.exp(-jnp.abs(a - c)))
    k = rows(1.0 - lb) * jnp.where(xf > 0.0, e, 1.0) * (1.0 / (1.0 + e))
    qs = _silu(q)

    g_hi = g.astype(BF16)
    g_lo = (g - g_hi.astype(F32)).astype(BF16)
    b2 = jnp.dot(tri, jnp.concatenate([g_hi, g_lo], axis=1), preferred_element_type=F32)
    b = b2[:, :HGRN_DIM] + b2[:, HGRN_DIM:]

    mid = (HGRN_CHUNK // 2, HGRN_CHUNK // 2 - 1)
    last = (HGRN_CHUNK - 1, 0)
    chunk_rows = [slice(r0, r0 + HGRN_CHUNK) for r0 in range(0, 2 * tb, HGRN_CHUNK)]
    b_mid, from_start, to_end = [], [], []
    for ci, sl in enumerate(chunk_rows):
        d = ci // nc
        mid_row = b[sl.start + mid[d]:sl.start + mid[d] + 1, :]
        last_row = b[sl.start + last[d]:sl.start + last[d] + 1, :]
        b_mid.append(jnp.broadcast_to(mid_row, (HGRN_CHUNK, HGRN_DIM)))
        from_start.append(jnp.exp(mid_row))
        to_end.append(jnp.exp(last_row - mid_row))
    b_mid = jnp.concatenate(b_mid, axis=0)
    spread = lambda rows_: jnp.concatenate(
        [jnp.broadcast_to(r, (HGRN_CHUNK, HGRN_DIM)) for r in rows_], axis=0)

    q_up = qs * jnp.exp(b - b_mid)
    k_dn = k * jnp.exp(b_mid - b)
    qm = q_up.astype(BF16)
    km = k_dn.astype(BF16)
    kp = (k_dn * spread(to_end)).astype(BF16)
    qd = (q_up * spread(from_start)).astype(BF16)
    vb = v.astype(BF16)

    scores = lax.dot_general(qm, km, NT_DIMS, preferred_element_type=F32)
    scores = jnp.where(tri > 0, scores, 0.0).astype(BF16)
    o_intra = jnp.dot(scores, vb, preferred_element_type=F32)

    chunks = []
    for ci, sl in enumerate(chunk_rows):
        dec = from_start[ci] * to_end[ci]
        inc = lax.dot_general(vb[sl], kp[sl], TN_DIMS, preferred_element_type=F32)
        chunks.append((qd[sl], dec, inc))
    return o_intra, chunks


def _hgrn_scan(prepared, state_refs):
    heads = len(prepared)
    nc = len(prepared[0][1]) // 2
    states = [[state_refs[d][h] for h in range(heads)] for d in range(2)]
    o_inter = [[None] * (2 * nc) for _ in range(heads)]
    for step in range(nc):
        for h in range(heads):
            for d in range(2):
                ci = d * nc + (step if d == 0 else nc - 1 - step)
                qd, dec, inc = prepared[h][1][ci]
                o_inter[h][ci] = lax.dot_general(qd, states[d][h].astype(BF16), NT_DIMS,
                                                 preferred_element_type=F32)
                states[d][h] = states[d][h] * dec + inc
    for d in range(2):
        for h in range(heads):
            state_refs[d][h] = states[d][h]
    return [prepared[h][0] + jnp.concatenate(o_inter[h], axis=0) for h in range(heads)]


def _hgrn_kernel(qf_ref, xf_ref, vf_ref, qr_ref, xr_ref, vr_ref, lbp_ref, tri_ref,
                 of_ref, or_ref, sf_ref, sr_ref, *, layer):
    @pl.when(pl.program_id(2) == 0)
    def _():
        sf_ref[...] = jnp.zeros_like(sf_ref)
        sr_ref[...] = jnp.zeros_like(sr_ref)

    rows = [lbp_ref[l] for l in range(DEPTH)]
    top = functools.reduce(jnp.maximum, rows)
    e = [jnp.exp(r - top) for r in rows]
    den = functools.reduce(lambda u, w: u + w, e)
    lb = jnp.zeros(rows[0].shape, F32)
    for l in range(1, layer + 1):
        lb = lb + e[l] / den

    tb = qf_ref.shape[0]
    heads = qf_ref.shape[1] // HGRN_DIM
    tri = tri_ref[...]
    prepared = []
    for h in range(heads):
        lanes = slice(h * HGRN_DIM, (h + 1) * HGRN_DIM)
        stack = lambda f_ref, r_ref: jnp.concatenate([f_ref[:, lanes], r_ref[:, lanes]], axis=0)
        prepared.append(_hgrn_prepare(stack(qf_ref, qr_ref), stack(xf_ref, xr_ref),
                                      stack(vf_ref, vr_ref), lb[:, lanes], tri))
    outs = _hgrn_scan(prepared, (sf_ref, sr_ref))
    for h, out in enumerate(outs):
        lanes = slice(h * HGRN_DIM, (h + 1) * HGRN_DIM)
        of_ref[:, lanes] = out[:tb].astype(BF16)
        or_ref[:, lanes] = out[tb:].astype(BF16)


def _hgrn(pa3, hgrn_lb, tri, layer, tb):
    bsz, seq, _ = pa3.shape
    nb = seq // tb
    groups = HGRN_HEADS // HGRN_HEADS_PER_STEP
    width = HGRN_HEADS_PER_STEP * HGRN_DIM

    def fwd(col):
        return pl.BlockSpec((None, tb, width), lambda b, h, i: (b, i, col * groups + h))

    def rev(col):
        return pl.BlockSpec((None, tb, width), lambda b, h, i: (b, nb - 1 - i, col * groups + h))

    out_f = pl.BlockSpec((None, tb, width), lambda b, h, i: (b, i, h))
    out_r = pl.BlockSpec((None, tb, width), lambda b, h, i: (b, nb - 1 - i, h))
    shape = jax.ShapeDtypeStruct((bsz, seq, HGRN_WIDTH), BF16)
    state = pltpu.VMEM((HGRN_HEADS_PER_STEP, HGRN_DIM, HGRN_DIM), F32)
    return pl.pallas_call(
        functools.partial(_hgrn_kernel, layer=layer),
        grid=(bsz, groups, nb),
        in_specs=[fwd(0), fwd(1), fwd(3), rev(0), rev(2), rev(3),
                  pl.BlockSpec((DEPTH, 2, width), lambda b, h, i: (0, 0, h)),
                  _const_spec((2 * tb, 2 * tb))],
        out_specs=[out_f, out_r],
        out_shape=[shape, shape],
        scratch_shapes=[state, state],
        compiler_params=_params(("parallel", "parallel", "arbitrary")),
        name="hgrn2",
    )(pa3, pa3, pa3, pa3, pa3, pa3, hgrn_lb, tri)


def _flash_kernel(*refs, maps, n_v, n_out, tk):
    n_maps = len(maps)
    qt_ref, k_ref = refs[0], refs[1]
    v_refs = refs[2:2 + n_v]
    out_refs = refs[2 + n_v:2 + n_v + n_out]
    scratch = refs[2 + n_v + n_out:]
    acc_ref, m_ref, gap_ref, qv_ref = scratch[:4]
    per_kind = 2 * n_maps
    s_ref, p_ref, al_ref = (
        [scratch[4 + kind * per_kind + slot * n_maps:4 + kind * per_kind + (slot + 1) * n_maps]
         for slot in range(2)] for kind in range(3))
    tq = qt_ref.shape[1]
    n_chunks = k_ref.shape[0] // tk
    sub = m_ref.shape[1]

    def scores(ci, m):
        off = pl.multiple_of(ci * tk, tk)
        return jnp.dot(k_ref[pl.ds(off, tk), :], qv_ref[m], preferred_element_type=F32)

    def column_max(s):
        top = jnp.max(s.reshape(s.shape[0] // sub, sub, tq), axis=0)
        return jnp.broadcast_to(jnp.max(top, axis=0, keepdims=True), (sub, tq))

    def lagged_softmax(ci, slot, m):
        shift = m_ref[m]
        s = scores(ci, m)
        p_ref[slot][m][...] = jnp.exp2(s - shift[0:1, :]).astype(BF16)
        top = column_max(s)
        gap_ref[m] = jnp.maximum(gap_ref[m], top - shift)
        m_next = jnp.maximum(shift, top)
        m_ref[m] = m_next
        al_ref[slot][m][...] = jnp.exp2(shift - m_next)

    def lagged_pv(ci, slot, m):
        acc_ref[m] = (acc_ref[m] + jnp.dot(v_refs[maps[m][2]][ci], p_ref[slot][m][...],
                                           preferred_element_type=F32)
                      ) * al_ref[slot][m][0:1, :]

    def lagged_trip(t, carry):
        first = t * CHUNKS_PER_TRIP
        for m in range(n_maps):
            lagged_softmax(first, 0, m)
        for u in range(1, CHUNKS_PER_TRIP):
            for m in range(n_maps):
                lagged_softmax(first + u, u % 2, m)
                lagged_pv(first + u - 1, (u - 1) % 2, m)
        for m in range(n_maps):
            lagged_pv(first + CHUNKS_PER_TRIP - 1, (CHUNKS_PER_TRIP - 1) % 2, m)
        return carry

    def exact_chunk(ci, slot):
        for m in range(n_maps):
            s_ref[slot][m][...] = scores(ci, m)
        for m, (_, _, vi, _, _) in enumerate(maps):
            m_prev = m_ref[m]
            m_next = jnp.maximum(m_prev, column_max(s_ref[slot][m][...]))
            m_ref[m] = m_next
            for c0 in range(0, tq, COL_BLOCK):
                cols = slice(c0, c0 + COL_BLOCK)
                p_ref[slot][m][:, cols] = jnp.exp2(
                    s_ref[slot][m][:, cols] - m_next[0:1, cols]).astype(BF16)
            acc_ref[m] = acc_ref[m] * jnp.exp2(m_prev - m_next)[0:1, :] + jnp.dot(
                v_refs[vi][ci], p_ref[slot][m][...], preferred_element_type=F32)

    def exact_trip(t, carry):
        for u in range(CHUNKS_PER_TRIP):
            exact_chunk(t * CHUNKS_PER_TRIP + u, u % 2)
        return carry

    qt = qt_ref[...]
    row = lax.broadcasted_iota(jnp.int32, qt.shape, 0)
    for m, (lo, hi, _, _, _) in enumerate(maps):
        qv_ref[m] = jnp.where((row >= lo) & (row < hi), qt, jnp.zeros_like(qt))

    acc_ref[...] = jnp.zeros_like(acc_ref)
    gap_ref[...] = jnp.zeros_like(gap_ref)
    for m in range(n_maps):
        m_ref[m] = column_max(jnp.dot(k_ref[0:LANES, :], qv_ref[m], preferred_element_type=F32))
    lax.fori_loop(0, n_chunks // CHUNKS_PER_TRIP, lagged_trip, 0)

    @pl.when(jnp.logical_not(jnp.max(gap_ref[...]) <= MAX_LAG_LOG2))
    def _():
        acc_ref[...] = jnp.zeros_like(acc_ref)
        m_ref[...] = jnp.full_like(m_ref, NEG_BIG)
        lax.fori_loop(0, n_chunks // CHUNKS_PER_TRIP, exact_trip, 0)

    for oi, out_ref in enumerate(out_refs):
        for m, (_, _, _, mo, half) in enumerate(maps):
            if mo == oi:
                acc = acc_ref[m]
                out_ref[half * GQA_HEAD_DIM:(half + 1) * GQA_HEAD_DIM, :] = (
                    acc[0:GQA_HEAD_DIM, :] * (1.0 / acc[GQA_HEAD_DIM:GQA_HEAD_DIM + 1, :])
                ).astype(BF16)


def _flash(qt3, k4, k_index, vt5, v_indices, maps, n_out, tq, name):
    bsz, qw, seq = qt3.shape
    tk = vt5.shape[-1]
    assert seq % tq == 0 and (seq // tk) % CHUNKS_PER_TRIP == 0, (seq, tq, tk)
    nqb = qw // LANES
    n_maps = len(maps)
    n_v = len(v_indices(0))
    per_kind = 2 * n_maps
    once = pl.Buffered(1)

    def kmap(b, j, qi):
        n, lb = k_index(b, j)
        return (n, b, 0, lb)

    kspec = pl.BlockSpec((None, None, seq, LANES), kmap, pipeline_mode=once)
    vspecs = [pl.BlockSpec((None, None, seq // tk, VALUE_ROWS, tk),
                           lambda b, j, qi, n=n: (v_indices(j)[n], b, 0, 0, 0), pipeline_mode=once)
              for n in range(n_v)]
    qspec = pl.BlockSpec((None, LANES, tq), lambda b, j, qi: (b, j, qi))
    return pl.pallas_call(
        functools.partial(_flash_kernel, maps=maps, n_v=n_v, n_out=n_out, tk=tk),
        grid=(bsz, nqb, seq // tq),
        in_specs=[qspec, kspec] + vspecs,
        out_specs=[qspec] * n_out,
        out_shape=[jax.ShapeDtypeStruct((bsz, qw, seq), BF16)] * n_out,
        scratch_shapes=([pltpu.VMEM((n_maps, VALUE_ROWS, tq), F32),
                         pltpu.VMEM((n_maps, SUBLANES, tq), F32),
                         pltpu.VMEM((n_maps, SUBLANES, tq), F32),
                         pltpu.VMEM((n_maps, LANES, tq), BF16)]
                        + [pltpu.VMEM((tk, tq), F32)] * per_kind
                        + [pltpu.VMEM((tk, tq), BF16)] * per_kind
                        + [pltpu.VMEM((SUBLANES, tq), F32)] * per_kind),
        compiler_params=_params(("parallel", "parallel", "parallel")),
        name=name,
    )(qt3, k4, *([vt5] * n_v))


GQA_MAPS = ((0, GQA_HEAD_DIM, 0, 0, 0), (GQA_HEAD_DIM, LANES, 0, 0, 1))
DIFF_MAPS = tuple((m * DIFF_HEAD_DIM, (m + 1) * DIFF_HEAD_DIM, m // 2, m % 2, m // 2)
                  for m in range(4))


def _outproj_kernel(x_ref, of_ref, or_ref, ob_ref, oc1_ref, oc2_ref, sg_ref, wo_ref,
                    hnw_ref, dnw_ref, lam_ref, postw_ref, g128_ref, g64_ref, out_ref, *, lam_init):
    sg = sg_ref[...].astype(F32)
    a = of_ref[...].astype(F32) + or_ref[...].astype(F32)
    ms = jnp.dot((a * a).astype(BF16), g128_ref[...], preferred_element_type=F32)
    mix_a = a * lax.rsqrt(ms + NORM_EPS) * hnw_ref[...] * sg[:, :HGRN_WIDTH]
    mix_b = ob_ref[...].astype(F32).T * sg[:, HGRN_WIDTH:HGRN_WIDTH + GQA_WIDTH]

    lp = lam_ref[...]
    lam = (jnp.exp(jnp.sum(lp[0:1] * lp[1:2], axis=-1, keepdims=True))
           - jnp.exp(jnp.sum(lp[2:3] * lp[3:4], axis=-1, keepdims=True)) + lam_init)
    c = (oc1_ref[...].astype(F32) - lam * oc2_ref[...].astype(F32)).T
    ms = jnp.dot((c * c).astype(BF16), g64_ref[...], preferred_element_type=F32)
    mix_c = (c * lax.rsqrt(ms + NORM_EPS) * dnw_ref[...] * (1.0 - lam_init)
             * sg[:, HGRN_WIDTH + GQA_WIDTH:])

    y = (jnp.dot(mix_a.astype(BF16), wo_ref[0:HGRN_WIDTH, :], preferred_element_type=F32)
         + jnp.dot(mix_b.astype(BF16), wo_ref[HGRN_WIDTH:HGRN_WIDTH + GQA_WIDTH, :],
                   preferred_element_type=F32)
         + jnp.dot(mix_c.astype(BF16), wo_ref[HGRN_WIDTH + GQA_WIDTH:, :],
                   preferred_element_type=F32))
    out_ref[...] = x_ref[...] + (y * lax.rsqrt(jnp.mean(y * y, axis=-1, keepdims=True) + NORM_EPS)
                                 * postw_ref[...])


def _outproj(x2, of2, or2, obt, oc1t, oc2t, sg, wo_bf, hnw, dnw, lam_p, postw, g128, g64,
             lam_init, tm):
    t = x2.shape[0]
    tiles_per_seq = obt.shape[2] // tm
    tok = lambda w: pl.BlockSpec((tm, w), lambda i: (i, 0))
    att = pl.BlockSpec((None, GQA_WIDTH, tm), lambda i: (i // tiles_per_seq, 0, i % tiles_per_seq))
    return pl.pallas_call(
        functools.partial(_outproj_kernel, lam_init=lam_init),
        grid=(t // tm,),
        in_specs=[tok(D_MODEL), tok(HGRN_WIDTH), tok(HGRN_WIDTH), att, att, att, tok(MIX_WIDTH),
                  _const_spec((MIX_WIDTH, D_MODEL)),
                  _const_spec((1, HGRN_WIDTH)), _const_spec((1, DIFF_WIDTH)),
                  _const_spec((4, DIFF_HEAD_DIM)), _const_spec((1, D_MODEL)),
                  _const_spec((HGRN_WIDTH, HGRN_WIDTH)), _const_spec((DIFF_WIDTH, DIFF_WIDTH))],
        out_specs=tok(D_MODEL),
        out_shape=jax.ShapeDtypeStruct((t, D_MODEL), F32),
        compiler_params=_params(("parallel",)),
        name="outproj",
    )(x2, of2, or2, obt, oc1t, oc2t, sg, wo_bf, hnw, dnw, lam_p, postw, g128, g64)


def _group_mean_matrix(width, group):
    idx = jnp.arange(width) // group
    return ((idx[:, None] == idx[None, :]).astype(F32) / group).astype(BF16)


def _rope_tables(seq_len):
    half = ROPE_DIM // 2
    inv = jnp.power(ROPE_THETA, -jnp.arange(0, ROPE_DIM, 2, dtype=F32) / ROPE_DIM)
    pos = jnp.arange(seq_len, dtype=F32)
    rows = seq_len // GRID_W
    row_pos = jnp.repeat(jnp.arange(rows, dtype=F32), GRID_W)
    col_pos = jnp.tile(jnp.arange(GRID_W, dtype=F32), rows)
    sign = jnp.concatenate([-jnp.ones((half,), F32), jnp.ones((half,), F32)])

    def tab(p):
        ang = p[:, None] * inv[None, :]
        ang = jnp.concatenate([ang, ang], axis=-1)
        return jnp.cos(ang), jnp.sin(ang) * sign

    cr, sr = tab(row_pos)
    cc, sc = tab(col_pos)
    c1, s1 = tab(pos)
    rep_b = GQA_WIDTH // (2 * ROPE_DIM)
    rep_c = DIFF_WIDTH // ROPE_DIM
    return (jnp.tile(jnp.concatenate([cr, cc], axis=-1), (1, rep_b)),
            jnp.tile(jnp.concatenate([sr, sc], axis=-1), (1, rep_b)),
            jnp.tile(c1, (1, rep_c)), jnp.tile(s1, (1, rep_c)))


def _chunk_tri(tb):
    r = jnp.arange(2 * tb)
    same = (r[:, None] // HGRN_CHUNK) == (r[None, :] // HGRN_CHUNK)
    tri = jnp.where(r[:, None] < tb, r[None, :] <= r[:, None], r[None, :] >= r[:, None])
    return (same & tri).astype(BF16)


def _trunk(x, pre_norm_w, w_in_bf, hgrn_lb, hgrn_norm_w, gqa_q_norm_w, gqa_k_norm_w,
           diff_lambda, diff_norm_w, w_out_bf, post_norm_w, *, tabs, g64, g128, tri,
           tm, tb, tq):
    bsz, seq, _ = x.shape
    t = bsz * seq
    x2 = x.reshape(t, D_MODEL)
    for layer in range(DEPTH):
        qnw = jnp.tile(gqa_q_norm_w[layer], GQA_HEADS)[None, :]
        knw = jnp.tile(gqa_k_norm_w[layer], GQA_KV_HEADS)[None, :]
        pa, qbt, kdup, vbt, qct, kc, vct, sg = _inproj(
            x2, bsz, seq, pre_norm_w[layer][None, :], w_in_bf[layer], tabs, qnw, knw, g64, tm)

        o_f, o_r = _hgrn(pa.reshape(bsz, seq, A_COLS), hgrn_lb, tri, layer, tb)

        (obt,) = _flash(qbt, kdup.reshape(GQA_KV_HEADS, bsz, seq, LANES), lambda b, j: (j, 0),
                        vbt, lambda j: (j,), GQA_MAPS, 1, tq, "flash_gqa")

        oc1t, oc2t = _flash(qct, kc.reshape(1, bsz, seq, -1), lambda b, j: (0, j),
                            vct, lambda j: (2 * j, 2 * j + 1), DIFF_MAPS, 2, tq, "flash_diff")

        lam_init = 0.8 - 0.6 * math.exp(-0.3 * layer)
        x2 = _outproj(x2, o_f.reshape(t, -1), o_r.reshape(t, -1), obt, oc1t, oc2t, sg,
                      w_out_bf[layer], hgrn_norm_w[layer][None, :],
                      jnp.tile(diff_norm_w[layer], DIFF_HEADS)[None, :],
                      diff_lambda[layer], post_norm_w[layer][None, :], g128, g64, lam_init, tm)
    return x2.reshape(bsz, seq, D_MODEL)


def kernel(x_prompt, x_sample, pre_norm_w, w_in, hgrn_lb, hgrn_norm_w, gqa_q_norm_w, gqa_k_norm_w,
           diff_lambda, diff_norm_w, w_out, post_norm_w):
    w_in_bf = w_in.astype(BF16)
    w_out_bf = w_out.astype(BF16)
    longest = max(x_prompt.shape[1], x_sample.shape[1])
    run = functools.partial(_trunk, pre_norm_w=pre_norm_w, w_in_bf=w_in_bf, hgrn_lb=hgrn_lb,
                            hgrn_norm_w=hgrn_norm_w, gqa_q_norm_w=gqa_q_norm_w,
                            gqa_k_norm_w=gqa_k_norm_w, diff_lambda=diff_lambda,
                            diff_norm_w=diff_norm_w, w_out_bf=w_out_bf, post_norm_w=post_norm_w,
                            tabs=_rope_tables(longest),
                            g64=_group_mean_matrix(GQA_WIDTH, GQA_HEAD_DIM),
                            g128=_group_mean_matrix(HGRN_WIDTH, HGRN_DIM),
                            tri=_chunk_tri(TILES["hgrn_block"]),
                            tm=TILES["proj_rows"], tb=TILES["hgrn_block"],
                            tq=TILES["attn_queries"])
    return (run(x_prompt), run(x_sample))
```

```python
import functools
import math

import jax
import jax.numpy as jnp
from jax import lax
from jax.experimental import pallas as pl
from jax.experimental.pallas import tpu as pltpu

F32 = jnp.float32
BF16 = jnp.bfloat16

D_MODEL = 1024
DEPTH = 2
GRID_W = 64
HGRN_HEADS = 4
HGRN_DIM = 128
HGRN_WIDTH = HGRN_HEADS * HGRN_DIM
HGRN_CHUNK = 32
HGRN_HEADS_PER_STEP = 4
GQA_HEADS = 4
GQA_KV_HEADS = 2
GQA_HEAD_DIM = 64
GQA_WIDTH = GQA_HEADS * GQA_HEAD_DIM
GQA_KV_WIDTH = GQA_KV_HEADS * GQA_HEAD_DIM
DIFF_HEADS = 4
DIFF_HEAD_DIM = 32
DIFF_WIDTH = DIFF_HEADS * 2 * DIFF_HEAD_DIM
MIX_WIDTH = HGRN_WIDTH + GQA_WIDTH + DIFF_WIDTH
ROPE_THETA = 10000.0
ROPE_DIM = 32
NORM_EPS = 1e-6

A_COLS = 4 * HGRN_WIDTH
OFF_AG = A_COLS
OFF_B = OFF_AG + HGRN_WIDTH
OFF_BG = OFF_B + GQA_WIDTH + 2 * GQA_KV_WIDTH
OFF_C = OFF_BG + GQA_WIDTH
IN_COLS = OFF_C + 4 * DIFF_WIDTH

LANES = 128
SUBLANES = 8
VMEM_LIMIT = 56 * 1024 * 1024
NEG_BIG = -1e30
LOG2E = math.log2(math.e)

VALUE_ROWS = 80
COL_BLOCK = 128
CHUNKS_PER_TRIP = 4
PV_LAG = 2
P_SLOTS = 4
MAX_LAG_LOG2 = 100.0

TILES = {"proj_rows": 512, "hgrn_block": 256, "attn_queries": 1024}

NT_DIMS = (((1,), (1,)), ((), ()))
TN_DIMS = (((0,), (0,)), ((), ()))


def _params(sem):
    return pltpu.CompilerParams(dimension_semantics=sem, vmem_limit_bytes=VMEM_LIMIT)


def _const_spec(shape):
    nd = len(shape)
    return pl.BlockSpec(shape, lambda *_: (0,) * nd)


def _rot_half16(x):
    outs = []
    for s in range(x.shape[1] // LANES):
        xs = x[:, s * LANES:(s + 1) * LANES]
        up = pltpu.roll(xs, ROPE_DIM // 2, 1)
        dn = pltpu.roll(xs, LANES - ROPE_DIM // 2, 1)
        lane = lax.broadcasted_iota(jnp.int32, xs.shape, 1)
        outs.append(jnp.where((lane & (ROPE_DIM - 1)) < ROPE_DIM // 2, dn, up))
    return outs[0] if len(outs) == 1 else jnp.concatenate(outs, axis=1)


def _silu(x):
    return x * (1.0 / (1.0 + jnp.exp(-x)))


def _inproj_kernel(x_ref, pw_ref, w_ref, cosb_ref, sinb_ref, cosc_ref, sinc_ref,
                   qnw_ref, knw_ref, g64_ref,
                   pa_ref, qbt_ref, kd_ref, vbt_ref, qct_ref, kc_ref, vct_ref, sg_ref):
    x = x_ref[...]
    h = x * lax.rsqrt(jnp.mean(x * x, axis=-1, keepdims=True) + NORM_EPS) * pw_ref[...]
    hb = h.astype(BF16)

    def proj(lo, hi):
        return jnp.dot(hb, w_ref[:, lo:hi], preferred_element_type=F32)

    pa_ref[...] = proj(0, A_COLS)
    sg_ref[:, 0:HGRN_WIDTH] = _silu(proj(OFF_AG, OFF_B)).astype(BF16)

    def head_rms(t, w, g):
        ms = jnp.dot((t * t).astype(BF16), g, preferred_element_type=F32)
        return t * lax.rsqrt(ms + NORM_EPS) * w

    def rope(t, cos, sin):
        return t * cos + _rot_half16(t) * sin

    lane = lax.broadcasted_iota(jnp.int32, (x.shape[0], LANES), 1)
    low = lane < GQA_HEAD_DIM

    def store_values(v, vt_ref):
        for s in range(v.shape[1] // LANES):
            slab = v[:, s * LANES:(s + 1) * LANES]
            for half, src in enumerate((slab, pltpu.roll(slab, GQA_HEAD_DIM, 1))):
                aug = jnp.where(low, src, jnp.where(lane == GQA_HEAD_DIM, 1.0, 0.0))
                vt_ref[2 * s + half] = aug.T[0:VALUE_ROWS, :].astype(BF16)

    g64 = g64_ref[...]
    cosb = cosb_ref[...]
    sinb = sinb_ref[...]
    bq = head_rms(proj(OFF_B, OFF_B + GQA_WIDTH), qnw_ref[...], g64)
    qbt_ref[...] = (rope(bq, cosb, sinb) * (LOG2E / math.sqrt(GQA_HEAD_DIM))).T.astype(BF16)
    off_k = OFF_B + GQA_WIDTH
    bk = head_rms(proj(off_k, off_k + GQA_KV_WIDTH), knw_ref[...],
                  g64[:GQA_KV_WIDTH, :GQA_KV_WIDTH])
    k = rope(bk, cosb[:, :GQA_KV_WIDTH], sinb[:, :GQA_KV_WIDTH])
    k_swapped = pltpu.roll(k, GQA_HEAD_DIM, 1)
    kd_ref[0] = jnp.where(low, k, k_swapped).astype(BF16)
    kd_ref[1] = jnp.where(low, k_swapped, k).astype(BF16)
    off_v = off_k + GQA_KV_WIDTH
    store_values(proj(off_v, OFF_BG), vbt_ref)
    sg_ref[:, HGRN_WIDTH:HGRN_WIDTH + GQA_WIDTH] = _silu(proj(OFF_BG, OFF_C)).astype(BF16)

    cosc = cosc_ref[...]
    sinc = sinc_ref[...]
    cq = proj(OFF_C, OFF_C + DIFF_WIDTH)
    qct_ref[...] = (rope(cq, cosc, sinc) * (LOG2E / math.sqrt(DIFF_HEAD_DIM))).T.astype(BF16)
    ck = proj(OFF_C + DIFF_WIDTH, OFF_C + 2 * DIFF_WIDTH)
    kc_ref[...] = rope(ck, cosc, sinc).astype(BF16)
    store_values(proj(OFF_C + 2 * DIFF_WIDTH, OFF_C + 3 * DIFF_WIDTH), vct_ref)
    sg_ref[:, HGRN_WIDTH + GQA_WIDTH:] = _silu(proj(OFF_C + 3 * DIFF_WIDTH, IN_COLS)).astype(BF16)


def _inproj(x2, bsz, seq_len, pre_w, w_in_bf, tabs, qnw, knw, g64, tm):
    t = x2.shape[0]
    tiles_per_seq = seq_len // tm
    tok = lambda w: pl.BlockSpec((tm, w), lambda i: (i, 0))
    tab = pl.BlockSpec((tm, GQA_WIDTH), lambda i: (i % tiles_per_seq, 0))
    qt_spec = pl.BlockSpec((None, GQA_WIDTH, tm),
                           lambda i: (i // tiles_per_seq, 0, i % tiles_per_seq))
    qt_shape = jax.ShapeDtypeStruct((bsz, GQA_WIDTH, seq_len), BF16)

    def vt(heads):
        return (pl.BlockSpec((heads, None, None, VALUE_ROWS, tm),
                             lambda i: (0, i // tiles_per_seq, i % tiles_per_seq, 0, 0)),
                jax.ShapeDtypeStruct((heads, bsz, tiles_per_seq, VALUE_ROWS, tm), BF16))

    outs = [(tok(A_COLS), jax.ShapeDtypeStruct((t, A_COLS), F32)),
            (qt_spec, qt_shape),
            (pl.BlockSpec((GQA_KV_HEADS, tm, LANES), lambda i: (0, i, 0)),
             jax.ShapeDtypeStruct((GQA_KV_HEADS, t, LANES), BF16)),
            vt(GQA_KV_HEADS),
            (qt_spec, qt_shape),
            (tok(DIFF_WIDTH), jax.ShapeDtypeStruct((t, DIFF_WIDTH), BF16)),
            vt(DIFF_HEADS),
            (tok(MIX_WIDTH), jax.ShapeDtypeStruct((t, MIX_WIDTH), BF16))]
    return pl.pallas_call(
        _inproj_kernel,
        grid=(t // tm,),
        in_specs=[tok(D_MODEL), _const_spec((1, D_MODEL)), _const_spec((D_MODEL, IN_COLS)),
                  tab, tab, tab, tab,
                  _const_spec((1, GQA_WIDTH)), _const_spec((1, GQA_KV_WIDTH)),
                  _const_spec((GQA_WIDTH, GQA_WIDTH))],
        out_specs=[o[0] for o in outs],
        out_shape=[o[1] for o in outs],
        compiler_params=_params(("parallel",)),
        name="inproj",
    )(x2, pre_w, w_in_bf, *tabs, qnw, knw, g64)


def _log1p(x):
    return jnp.log(1.0 + x)


def _hgrn_prepare(q, xf, v, lb, tri):
    tb = q.shape[0] // 2
    nc = tb // HGRN_CHUNK

    def rows(t):
        return jnp.concatenate([jnp.broadcast_to(t[0:1], (tb, HGRN_DIM)),
                                jnp.broadcast_to(t[1:2], (tb, HGRN_DIM))], axis=0)

    e = jnp.exp(-jnp.abs(xf))
    log_sig = jnp.minimum(xf, 0.0) - jnp.log(1.0 + e)
    c = rows(_log1p(-lb)) + log_sig
    a = rows(jnp.log(lb))
    g = jnp.maximum(a, c) + _log1p(jnp.exp(-jnp.abs(a - c)))
    k = rows(1.0 - lb) * jnp.where(xf > 0.0, e, 1.0) * (1.0 / (1.0 + e))
    qs = _silu(q)

    g_hi = g.astype(BF16)
    g_lo = (g - g_hi.astype(F32)).astype(BF16)
    b2 = jnp.dot(tri, jnp.concatenate([g_hi, g_lo], axis=1), preferred_element_type=F32)
    b = b2[:, :HGRN_DIM] + b2[:, HGRN_DIM:]

    mid = (HGRN_CHUNK // 2, HGRN_CHUNK // 2 - 1)
    last = (HGRN_CHUNK - 1, 0)
    chunk_rows = [slice(r0, r0 + HGRN_CHUNK) for r0 in range(0, 2 * tb, HGRN_CHUNK)]
    b_mid, from_start, to_end = [], [], []
    for ci, sl in enumerate(chunk_rows):
        d = ci // nc
        mid_row = b[sl.start + mid[d]:sl.start + mid[d] + 1, :]
        last_row = b[sl.start + last[d]:sl.start + last[d] + 1, :]
        b_mid.append(jnp.broadcast_to(mid_row, (HGRN_CHUNK, HGRN_DIM)))
        from_start.append(jnp.exp(mid_row))
        to_end.append(jnp.exp(last_row - mid_row))
    b_mid = jnp.concatenate(b_mid, axis=0)
    spread = lambda rows_: jnp.concatenate(
        [jnp.broadcast_to(r, (HGRN_CHUNK, HGRN_DIM)) for r in rows_], axis=0)

    q_up = qs * jnp.exp(b - b_mid)
    k_dn = k * jnp.exp(b_mid - b)
    qm = q_up.astype(BF16)
    km = k_dn.astype(BF16)
    kp = (k_dn * spread(to_end)).astype(BF16)
    qd = (q_up * spread(from_start)).astype(BF16)
    vb = v.astype(BF16)

    scores = lax.dot_general(qm, km, NT_DIMS, preferred_element_type=F32)
    scores = jnp.where(tri > 0, scores, 0.0).astype(BF16)
    o_intra = jnp.dot(scores, vb, preferred_element_type=F32)

    chunks = []
    for ci, sl in enumerate(chunk_rows):
        dec = from_start[ci] * to_end[ci]
        inc = lax.dot_general(vb[sl], kp[sl], TN_DIMS, preferred_element_type=F32)
        chunks.append((qd[sl], dec, inc))
    return o_intra, chunks


def _hgrn_scan(prepared, state_refs):
    heads = len(prepared)
    nc = len(prepared[0][1]) // 2
    states = [[state_refs[d][h] for h in range(heads)] for d in range(2)]
    o_inter = [[None] * (2 * nc) for _ in range(heads)]
    for step in range(nc):
        for h in range(heads):
            for d in range(2):
                ci = d * nc + (step if d == 0 else nc - 1 - step)
                qd, dec, inc = prepared[h][1][ci]
                o_inter[h][ci] = lax.dot_general(qd, states[d][h].astype(BF16), NT_DIMS,
                                                 preferred_element_type=F32)
                states[d][h] = states[d][h] * dec + inc
    for d in range(2):
        for h in range(heads):
            state_refs[d][h] = states[d][h]
    return [prepared[h][0] + jnp.concatenate(o_inter[h], axis=0) for h in range(heads)]


def _hgrn_kernel(qf_ref, xf_ref, vf_ref, qr_ref, xr_ref, vr_ref, lbp_ref, tri_ref,
                 of_ref, or_ref, sf_ref, sr_ref, *, layer):
    @pl.when(pl.program_id(2) == 0)
    def _():
        sf_ref[...] = jnp.zeros_like(sf_ref)
        sr_ref[...] = jnp.zeros_like(sr_ref)

    rows = [lbp_ref[l] for l in range(DEPTH)]
    top = functools.reduce(jnp.maximum, rows)
    e = [jnp.exp(r - top) for r in rows]
    den = functools.reduce(lambda u, w: u + w, e)
    lb = jnp.zeros(rows[0].shape, F32)
    for l in range(1, layer + 1):
        lb = lb + e[l] / den

    tb = qf_ref.shape[0]
    heads = qf_ref.shape[1] // HGRN_DIM
    tri = tri_ref[...]
    prepared = []
    for h in range(heads):
        lanes = slice(h * HGRN_DIM, (h + 1) * HGRN_DIM)
        stack = lambda f_ref, r_ref: jnp.concatenate([f_ref[:, lanes], r_ref[:, lanes]], axis=0)
        prepared.append(_hgrn_prepare(stack(qf_ref, qr_ref), stack(xf_ref, xr_ref),
                                      stack(vf_ref, vr_ref), lb[:, lanes], tri))
    outs = _hgrn_scan(prepared, (sf_ref, sr_ref))
    for h, out in enumerate(outs):
        lanes = slice(h * HGRN_DIM, (h + 1) * HGRN_DIM)
        of_ref[:, lanes] = out[:tb].astype(BF16)
        or_ref[:, lanes] = out[tb:].astype(BF16)


def _hgrn(pa3, hgrn_lb, tri, layer, tb):
    bsz, seq, _ = pa3.shape
    nb = seq // tb
    groups = HGRN_HEADS // HGRN_HEADS_PER_STEP
    width = HGRN_HEADS_PER_STEP * HGRN_DIM

    def fwd(col):
        return pl.BlockSpec((None, tb, width), lambda b, h, i: (b, i, col * groups + h))

    def rev(col):
        return pl.BlockSpec((None, tb, width), lambda b, h, i: (b, nb - 1 - i, col * groups + h))

    out_f = pl.BlockSpec((None, tb, width), lambda b, h, i: (b, i, h))
    out_r = pl.BlockSpec((None, tb, width), lambda b, h, i: (b, nb - 1 - i, h))
    shape = jax.ShapeDtypeStruct((bsz, seq, HGRN_WIDTH), BF16)
    state = pltpu.VMEM((HGRN_HEADS_PER_STEP, HGRN_DIM, HGRN_DIM), F32)
    return pl.pallas_call(
        functools.partial(_hgrn_kernel, layer=layer),
        grid=(bsz, groups, nb),
        in_specs=[fwd(0), fwd(1), fwd(3), rev(0), rev(2), rev(3),
                  pl.BlockSpec((DEPTH, 2, width), lambda b, h, i: (0, 0, h)),
                  _const_spec((2 * tb, 2 * tb))],
        out_specs=[out_f, out_r],
        out_shape=[shape, shape],
        scratch_shapes=[state, state],
        compiler_params=_params(("parallel", "parallel", "arbitrary")),
        name="hgrn2",
    )(pa3, pa3, pa3, pa3, pa3, pa3, hgrn_lb, tri)


def _flash_kernel(*refs, maps, n_v, n_out, tk):
    n_maps = len(maps)
    qt_ref, k_ref = refs[0], refs[1]
    v_refs = refs[2:2 + n_v]
    out_refs = refs[2 + n_v:2 + n_v + n_out]
    scratch = refs[2 + n_v + n_out:]
    acc_ref, m_ref, gap_ref, qv_ref = scratch[:4]
    s_ref = scratch[4:4 + n_maps]
    p_ref, al_ref = (
        [scratch[4 + n_maps + (kind * P_SLOTS + slot) * n_maps:
                 4 + n_maps + (kind * P_SLOTS + slot + 1) * n_maps] for slot in range(P_SLOTS)]
        for kind in range(2))
    tq = qt_ref.shape[1]
    n_chunks = k_ref.shape[0] // tk
    sub = m_ref.shape[1]

    def scores(ci, m):
        off = pl.multiple_of(ci * tk, tk)
        return jnp.dot(k_ref[pl.ds(off, tk), :], qv_ref[m], preferred_element_type=F32)

    def column_max(s):
        top = jnp.max(s.reshape(s.shape[0] // sub, sub, tq), axis=0)
        return jnp.broadcast_to(jnp.max(top, axis=0, keepdims=True), (sub, tq))

    def lagged_softmax(ci, slot):
        for m in range(n_maps):
            shift = m_ref[m]
            s = scores(ci, m)
            p_ref[slot][m][...] = jnp.exp2(s - shift[0:1, :]).astype(BF16)
            top = column_max(s)
            gap_ref[m] = jnp.maximum(gap_ref[m], top - shift)
            m_next = jnp.maximum(shift, top)
            m_ref[m] = m_next
            al_ref[slot][m][...] = jnp.exp2(shift - m_next)

    def lagged_pv(ci, slot):
        for m, (_, _, vi, _, _) in enumerate(maps):
            acc_ref[m] = (acc_ref[m] + jnp.dot(v_refs[vi][ci], p_ref[slot][m][...],
                                               preferred_element_type=F32)
                          ) * al_ref[slot][m][0:1, :]

    def lagged_trip(t, carry):
        first = t * CHUNKS_PER_TRIP
        for u in range(CHUNKS_PER_TRIP):
            lagged_softmax(first + u, u % P_SLOTS)
            lagged_pv(jnp.maximum(first + u - PV_LAG, 0), (u - PV_LAG) % P_SLOTS)
        return carry

    def exact_chunk(ci, slot):
        for m, (_, _, vi, _, _) in enumerate(maps):
            s_ref[m][...] = scores(ci, m)
            m_prev = m_ref[m]
            m_next = jnp.maximum(m_prev, column_max(s_ref[m][...]))
            m_ref[m] = m_next
            for c0 in range(0, tq, COL_BLOCK):
                cols = slice(c0, c0 + COL_BLOCK)
                p_ref[slot][m][:, cols] = jnp.exp2(
                    s_ref[m][:, cols] - m_next[0:1, cols]).astype(BF16)
            acc_ref[m] = acc_ref[m] * jnp.exp2(m_prev - m_next)[0:1, :] + jnp.dot(
                v_refs[vi][ci], p_ref[slot][m][...], preferred_element_type=F32)

    def exact_trip(t, carry):
        for u in range(CHUNKS_PER_TRIP):
            exact_chunk(t * CHUNKS_PER_TRIP + u, u % 2)
        return carry

    qt = qt_ref[...]
    row = lax.broadcasted_iota(jnp.int32, qt.shape, 0)
    for m, (lo, hi, _, _, _) in enumerate(maps):
        qv_ref[m] = jnp.where((row >= lo) & (row < hi), qt, jnp.zeros_like(qt))

    acc_ref[...] = jnp.zeros_like(acc_ref)
    gap_ref[...] = jnp.zeros_like(gap_ref)
    for m in range(n_maps):
        m_ref[m] = column_max(jnp.dot(k_ref[0:LANES, :], qv_ref[m], preferred_element_type=F32))
        for slot in range(P_SLOTS - PV_LAG, P_SLOTS):
            p_ref[slot][m][...] = jnp.zeros(p_ref[slot][m].shape, BF16)
            al_ref[slot][m][...] = jnp.ones(al_ref[slot][m].shape, F32)
    lax.fori_loop(0, n_chunks // CHUNKS_PER_TRIP, lagged_trip, 0)
    for ci in range(n_chunks - PV_LAG, n_chunks):
        lagged_pv(ci, ci % P_SLOTS)

    @pl.when(jnp.logical_not(jnp.max(gap_ref[...]) <= MAX_LAG_LOG2))
    def _():
        acc_ref[...] = jnp.zeros_like(acc_ref)
        m_ref[...] = jnp.full_like(m_ref, NEG_BIG)
        lax.fori_loop(0, n_chunks // CHUNKS_PER_TRIP, exact_trip, 0)

    for oi, out_ref in enumerate(out_refs):
        for m, (_, _, _, mo, half) in enumerate(maps):
            if mo == oi:
                acc = acc_ref[m]
                out_ref[half * GQA_HEAD_DIM:(half + 1) * GQA_HEAD_DIM, :] = (
                    acc[0:GQA_HEAD_DIM, :] * (1.0 / acc[GQA_HEAD_DIM:GQA_HEAD_DIM + 1, :])
                ).astype(BF16)


def _flash(qt3, k4, k_index, vt5, v_indices, maps, n_out, tq, name):
    bsz, qw, seq = qt3.shape
    tk = vt5.shape[-1]
    assert seq % tq == 0 and (seq // tk) % CHUNKS_PER_TRIP == 0, (seq, tq, tk)
    assert CHUNKS_PER_TRIP % P_SLOTS == 0 and PV_LAG + 2 <= P_SLOTS
    nqb = qw // LANES
    n_maps = len(maps)
    n_v = len(v_indices(0))
    once = pl.Buffered(1)

    def kmap(b, j, qi):
        n, lb = k_index(b, j)
        return (n, b, 0, lb)

    kspec = pl.BlockSpec((None, None, seq, LANES), kmap, pipeline_mode=once)
    vspecs = [pl.BlockSpec((None, None, seq // tk, VALUE_ROWS, tk),
                           lambda b, j, qi, n=n: (v_indices(j)[n], b, 0, 0, 0), pipeline_mode=once)
              for n in range(n_v)]
    qspec = pl.BlockSpec((None, LANES, tq), lambda b, j, qi: (b, j, qi))
    return pl.pallas_call(
        functools.partial(_flash_kernel, maps=maps, n_v=n_v, n_out=n_out, tk=tk),
        grid=(bsz, nqb, seq // tq),
        in_specs=[qspec, kspec] + vspecs,
        out_specs=[qspec] * n_out,
        out_shape=[jax.ShapeDtypeStruct((bsz, qw, seq), BF16)] * n_out,
        scratch_shapes=([pltpu.VMEM((n_maps, VALUE_ROWS, tq), F32),
                         pltpu.VMEM((n_maps, SUBLANES, tq), F32),
                         pltpu.VMEM((n_maps, SUBLANES, tq), F32),
                         pltpu.VMEM((n_maps, LANES, tq), BF16)]
                        + [pltpu.VMEM((tk, tq), F32)] * n_maps
                        + [pltpu.VMEM((tk, tq), BF16)] * (P_SLOTS * n_maps)
                        + [pltpu.VMEM((SUBLANES, tq), F32)] * (P_SLOTS * n_maps)),
        compiler_params=_params(("parallel", "parallel", "parallel")),
        name=name,
    )(qt3, k4, *([vt5] * n_v))


GQA_MAPS = ((0, GQA_HEAD_DIM, 0, 0, 0), (GQA_HEAD_DIM, LANES, 0, 0, 1))
DIFF_MAPS = tuple((m * DIFF_HEAD_DIM, (m + 1) * DIFF_HEAD_DIM, m // 2, m % 2, m // 2)
                  for m in range(4))


def _outproj_kernel(x_ref, of_ref, or_ref, ob_ref, oc1_ref, oc2_ref, sg_ref, wo_ref,
                    hnw_ref, dnw_ref, lam_ref, postw_ref, g128_ref, g64_ref, out_ref, *, lam_init):
    sg = sg_ref[...].astype(F32)
    a = of_ref[...].astype(F32) + or_ref[...].astype(F32)
    ms = jnp.dot((a * a).astype(BF16), g128_ref[...], preferred_element_type=F32)
    mix_a = a * lax.rsqrt(ms + NORM_EPS) * hnw_ref[...] * sg[:, :HGRN_WIDTH]
    mix_b = ob_ref[...].astype(F32).T * sg[:, HGRN_WIDTH:HGRN_WIDTH + GQA_WIDTH]

    lp = lam_ref[...]
    lam = (jnp.exp(jnp.sum(lp[0:1] * lp[1:2], axis=-1, keepdims=True))
           - jnp.exp(jnp.sum(lp[2:3] * lp[3:4], axis=-1, keepdims=True)) + lam_init)
    c = (oc1_ref[...].astype(F32) - lam * oc2_ref[...].astype(F32)).T
    ms = jnp.dot((c * c).astype(BF16), g64_ref[...], preferred_element_type=F32)
    mix_c = (c * lax.rsqrt(ms + NORM_EPS) * dnw_ref[...] * (1.0 - lam_init)
             * sg[:, HGRN_WIDTH + GQA_WIDTH:])

    y = (jnp.dot(mix_a.astype(BF16), wo_ref[0:HGRN_WIDTH, :], preferred_element_type=F32)
         + jnp.dot(mix_b.astype(BF16), wo_ref[HGRN_WIDTH:HGRN_WIDTH + GQA_WIDTH, :],
                   preferred_element_type=F32)
         + jnp.dot(mix_c.astype(BF16), wo_ref[HGRN_WIDTH + GQA_WIDTH:, :],
                   preferred_element_type=F32))
    out_ref[...] = x_ref[...] + (y * lax.rsqrt(jnp.mean(y * y, axis=-1, keepdims=True) + NORM_EPS)
                                 * postw_ref[...])


def _outproj(x2, of2, or2, obt, oc1t, oc2t, sg, wo_bf, hnw, dnw, lam_p, postw, g128, g64,
             lam_init, tm):
    t = x2.shape[0]
    tiles_per_seq = obt.shape[2] // tm
    tok = lambda w: pl.BlockSpec((tm, w), lambda i: (i, 0))
    att = pl.BlockSpec((None, GQA_WIDTH, tm), lambda i: (i // tiles_per_seq, 0, i % tiles_per_seq))
    return pl.pallas_call(
        functools.partial(_outproj_kernel, lam_init=lam_init),
        grid=(t // tm,),
        in_specs=[tok(D_MODEL), tok(HGRN_WIDTH), tok(HGRN_WIDTH), att, att, att, tok(MIX_WIDTH),
                  _const_spec((MIX_WIDTH, D_MODEL)),
                  _const_spec((1, HGRN_WIDTH)), _const_spec((1, DIFF_WIDTH)),
                  _const_spec((4, DIFF_HEAD_DIM)), _const_spec((1, D_MODEL)),
                  _const_spec((HGRN_WIDTH, HGRN_WIDTH)), _const_spec((DIFF_WIDTH, DIFF_WIDTH))],
        out_specs=tok(D_MODEL),
        out_shape=jax.ShapeDtypeStruct((t, D_MODEL), F32),
        compiler_params=_params(("parallel",)),
        name="outproj",
    )(x2, of2, or2, obt, oc1t, oc2t, sg, wo_bf, hnw, dnw, lam_p, postw, g128, g64)


def _group_mean_matrix(width, group):
    idx = jnp.arange(width) // group
    return ((idx[:, None] == idx[None, :]).astype(F32) / group).astype(BF16)


def _rope_tables(seq_len):
    half = ROPE_DIM // 2
    inv = jnp.power(ROPE_THETA, -jnp.arange(0, ROPE_DIM, 2, dtype=F32) / ROPE_DIM)
    pos = jnp.arange(seq_len, dtype=F32)
    rows = seq_len // GRID_W
    row_pos = jnp.repeat(jnp.arange(rows, dtype=F32), GRID_W)
    col_pos = jnp.tile(jnp.arange(GRID_W, dtype=F32), rows)
    sign = jnp.concatenate([-jnp.ones((half,), F32), jnp.ones((half,), F32)])

    def tab(p):
        ang = p[:, None] * inv[None, :]
        ang = jnp.concatenate([ang, ang], axis=-1)
        return jnp.cos(ang), jnp.sin(ang) * sign

    cr, sr = tab(row_pos)
    cc, sc = tab(col_pos)
    c1, s1 = tab(pos)
    rep_b = GQA_WIDTH // (2 * ROPE_DIM)
    rep_c = DIFF_WIDTH // ROPE_DIM
    return (jnp.tile(jnp.concatenate([cr, cc], axis=-1), (1, rep_b)),
            jnp.tile(jnp.concatenate([sr, sc], axis=-1), (1, rep_b)),
            jnp.tile(c1, (1, rep_c)), jnp.tile(s1, (1, rep_c)))


def _chunk_tri(tb):
    r = jnp.arange(2 * tb)
    same = (r[:, None] // HGRN_CHUNK) == (r[None, :] // HGRN_CHUNK)
    tri = jnp.where(r[:, None] < tb, r[None, :] <= r[:, None], r[None, :] >= r[:, None])
    return (same & tri).astype(BF16)


def _trunk(x, pre_norm_w, w_in_bf, hgrn_lb, hgrn_norm_w, gqa_q_norm_w, gqa_k_norm_w,
           diff_lambda, diff_norm_w, w_out_bf, post_norm_w, *, tabs, g64, g128, tri,
           tm, tb, tq):
    bsz, seq, _ = x.shape
    t = bsz * seq
    x2 = x.reshape(t, D_MODEL)
    for layer in range(DEPTH):
        qnw = jnp.tile(gqa_q_norm_w[layer], GQA_HEADS)[None, :]
        knw = jnp.tile(gqa_k_norm_w[layer], GQA_KV_HEADS)[None, :]
        pa, qbt, kdup, vbt, qct, kc, vct, sg = _inproj(
            x2, bsz, seq, pre_norm_w[layer][None, :], w_in_bf[layer], tabs, qnw, knw, g64, tm)

        o_f, o_r = _hgrn(pa.reshape(bsz, seq, A_COLS), hgrn_lb, tri, layer, tb)

        (obt,) = _flash(qbt, kdup.reshape(GQA_KV_HEADS, bsz, seq, LANES), lambda b, j: (j, 0),
                        vbt, lambda j: (j,), GQA_MAPS, 1, tq, "flash_gqa")

        oc1t, oc2t = _flash(qct, kc.reshape(1, bsz, seq, -1), lambda b, j: (0, j),
                            vct, lambda j: (2 * j, 2 * j + 1), DIFF_MAPS, 2, tq, "flash_diff")

        lam_init = 0.8 - 0.6 * math.exp(-0.3 * layer)
        x2 = _outproj(x2, o_f.reshape(t, -1), o_r.reshape(t, -1), obt, oc1t, oc2t, sg,
                      w_out_bf[layer], hgrn_norm_w[layer][None, :],
                      jnp.tile(diff_norm_w[layer], DIFF_HEADS)[None, :],
                      diff_lambda[layer], post_norm_w[layer][None, :], g128, g64, lam_init, tm)
    return x2.reshape(bsz, seq, D_MODEL)


def kernel(x_prompt, x_sample, pre_norm_w, w_in, hgrn_lb, hgrn_norm_w, gqa_q_norm_w, gqa_k_norm_w,
           diff_lambda, diff_norm_w, w_out, post_norm_w):
    w_in_bf = w_in.astype(BF16)
    w_out_bf = w_out.astype(BF16)
    longest = max(x_prompt.shape[1], x_sample.shape[1])
    run = functools.partial(_trunk, pre_norm_w=pre_norm_w, w_in_bf=w_in_bf, hgrn_lb=hgrn_lb,
                            hgrn_norm_w=hgrn_norm_w, gqa_q_norm_w=gqa_q_norm_w,
                            gqa_k_norm_w=gqa_k_norm_w, diff_lambda=diff_lambda,
                            diff_norm_w=diff_norm_w, w_out_bf=w_out_bf, post_norm_w=post_norm_w,
                            tabs=_rope_tables(longest),
                            g64=_group_mean_matrix(GQA_WIDTH, GQA_HEAD_DIM),
                            g128=_group_mean_matrix(HGRN_WIDTH, HGRN_DIM),
                            tri=_chunk_tri(TILES["hgrn_block"]),
                            tm=TILES["proj_rows"], tb=TILES["hgrn_block"],
                            tq=TILES["attn_queries"])
    return (run(x_prompt), run(x_sample))
```

```python
import functools
import math

import jax
import jax.numpy as jnp
from jax import lax
from jax.experimental import pallas as pl
from jax.experimental.pallas import tpu as pltpu

F32 = jnp.float32
BF16 = jnp.bfloat16

D_MODEL = 1024
DEPTH = 2
GRID_W = 64
HGRN_HEADS = 4
HGRN_DIM = 128
HGRN_WIDTH = HGRN_HEADS * HGRN_DIM
HGRN_CHUNK = 32
HGRN_HEADS_PER_STEP = 4
GQA_HEADS = 4
GQA_KV_HEADS = 2
GQA_HEAD_DIM = 64
GQA_WIDTH = GQA_HEADS * GQA_HEAD_DIM
GQA_KV_WIDTH = GQA_KV_HEADS * GQA_HEAD_DIM
DIFF_HEADS = 4
DIFF_HEAD_DIM = 32
DIFF_WIDTH = DIFF_HEADS * 2 * DIFF_HEAD_DIM
MIX_WIDTH = HGRN_WIDTH + GQA_WIDTH + DIFF_WIDTH
ROPE_THETA = 10000.0
ROPE_DIM = 32
NORM_EPS = 1e-6

A_COLS = 4 * HGRN_WIDTH
OFF_AG = A_COLS
OFF_B = OFF_AG + HGRN_WIDTH
OFF_BG = OFF_B + GQA_WIDTH + 2 * GQA_KV_WIDTH
OFF_C = OFF_BG + GQA_WIDTH
IN_COLS = OFF_C + 4 * DIFF_WIDTH

LANES = 128
SUBLANES = 8
VMEM_LIMIT = 56 * 1024 * 1024
NEG_BIG = -1e30
LOG2E = math.log2(math.e)

VALUE_ROWS = 80
COL_BLOCK = 128
CHUNKS_PER_TRIP = 4
MAX_LAG_LOG2 = 100.0

TILES = {"proj_rows": 512, "hgrn_block": 256, "attn_queries": 1024}

NT_DIMS = (((1,), (1,)), ((), ()))
TN_DIMS = (((0,), (0,)), ((), ()))


def _params(sem):
    return pltpu.CompilerParams(dimension_semantics=sem, vmem_limit_bytes=VMEM_LIMIT)


def _const_spec(shape):
    nd = len(shape)
    return pl.BlockSpec(shape, lambda *_: (0,) * nd)


def _rot_half16(x):
    outs = []
    for s in range(x.shape[1] // LANES):
        xs = x[:, s * LANES:(s + 1) * LANES]
        up = pltpu.roll(xs, ROPE_DIM // 2, 1)
        dn = pltpu.roll(xs, LANES - ROPE_DIM // 2, 1)
        lane = lax.broadcasted_iota(jnp.int32, xs.shape, 1)
        outs.append(jnp.where((lane & (ROPE_DIM - 1)) < ROPE_DIM // 2, dn, up))
    return outs[0] if len(outs) == 1 else jnp.concatenate(outs, axis=1)


def _silu(x):
    return x * (1.0 / (1.0 + jnp.exp(-x)))


def _inproj_kernel(x_ref, pw_ref, w_ref, cosb_ref, sinb_ref, cosc_ref, sinc_ref,
                   qnw_ref, knw_ref, g64_ref,
                   pa_ref, qbt_ref, kd_ref, vbt_ref, qct_ref, kc_ref, vct_ref, sg_ref):
    x = x_ref[...]
    h = x * lax.rsqrt(jnp.mean(x * x, axis=-1, keepdims=True) + NORM_EPS) * pw_ref[...]
    hb = h.astype(BF16)

    def proj(lo, hi):
        return jnp.dot(hb, w_ref[:, lo:hi], preferred_element_type=F32)

    pa_ref[...] = proj(0, A_COLS)
    sg_ref[:, 0:HGRN_WIDTH] = _silu(proj(OFF_AG, OFF_B)).astype(BF16)

    def head_rms(t, w, g):
        ms = jnp.dot((t * t).astype(BF16), g, preferred_element_type=F32)
        return t * lax.rsqrt(ms + NORM_EPS) * w

    def rope(t, cos, sin):
        return t * cos + _rot_half16(t) * sin

    lane = lax.broadcasted_iota(jnp.int32, (x.shape[0], LANES), 1)
    low = lane < GQA_HEAD_DIM

    def store_values(v, vt_ref):
        for s in range(v.shape[1] // LANES):
            slab = v[:, s * LANES:(s + 1) * LANES]
            for half, src in enumerate((slab, pltpu.roll(slab, GQA_HEAD_DIM, 1))):
                aug = jnp.where(low, src, jnp.where(lane == GQA_HEAD_DIM, 1.0, 0.0))
                vt_ref[2 * s + half] = aug.T[0:VALUE_ROWS, :].astype(BF16)

    g64 = g64_ref[...]
    cosb = cosb_ref[...]
    sinb = sinb_ref[...]
    bq = head_rms(proj(OFF_B, OFF_B + GQA_WIDTH), qnw_ref[...], g64)
    qbt_ref[...] = (rope(bq, cosb, sinb) * (LOG2E / math.sqrt(GQA_HEAD_DIM))).T.astype(BF16)
    off_k = OFF_B + GQA_WIDTH
    bk = head_rms(proj(off_k, off_k + GQA_KV_WIDTH), knw_ref[...],
                  g64[:GQA_KV_WIDTH, :GQA_KV_WIDTH])
    k = rope(bk, cosb[:, :GQA_KV_WIDTH], sinb[:, :GQA_KV_WIDTH])
    k_swapped = pltpu.roll(k, GQA_HEAD_DIM, 1)
    kd_ref[0] = jnp.where(low, k, k_swapped).astype(BF16)
    kd_ref[1] = jnp.where(low, k_swapped, k).astype(BF16)
    off_v = off_k + GQA_KV_WIDTH
    store_values(proj(off_v, OFF_BG), vbt_ref)
    sg_ref[:, HGRN_WIDTH:HGRN_WIDTH + GQA_WIDTH] = _silu(proj(OFF_BG, OFF_C)).astype(BF16)

    cosc = cosc_ref[...]
    sinc = sinc_ref[...]
    cq = proj(OFF_C, OFF_C + DIFF_WIDTH)
    qct_ref[...] = (rope(cq, cosc, sinc) * (LOG2E / math.sqrt(DIFF_HEAD_DIM))).T.astype(BF16)
    ck = proj(OFF_C + DIFF_WIDTH, OFF_C + 2 * DIFF_WIDTH)
    kc_ref[...] = rope(ck, cosc, sinc).astype(BF16)
    store_values(proj(OFF_C + 2 * DIFF_WIDTH, OFF_C + 3 * DIFF_WIDTH), vct_ref)
    sg_ref[:, HGRN_WIDTH + GQA_WIDTH:] = _silu(proj(OFF_C + 3 * DIFF_WIDTH, IN_COLS)).astype(BF16)


def _inproj(x2, bsz, seq_len, pre_w, w_in_bf, tabs, qnw, knw, g64, tm):
    t = x2.shape[0]
    tiles_per_seq = seq_len // tm
    tok = lambda w: pl.BlockSpec((tm, w), lambda i: (i, 0))
    tab = pl.BlockSpec((tm, GQA_WIDTH), lambda i: (i % tiles_per_seq, 0))
    qt_spec = pl.BlockSpec((None, GQA_WIDTH, tm),
                           lambda i: (i // tiles_per_seq, 0, i % tiles_per_seq))
    qt_shape = jax.ShapeDtypeStruct((bsz, GQA_WIDTH, seq_len), BF16)

    def vt(heads):
        return (pl.BlockSpec((heads, None, None, VALUE_ROWS, tm),
                             lambda i: (0, i // tiles_per_seq, i % tiles_per_seq, 0, 0)),
                jax.ShapeDtypeStruct((heads, bsz, tiles_per_seq, VALUE_ROWS, tm), BF16))

    outs = [(tok(A_COLS), jax.ShapeDtypeStruct((t, A_COLS), F32)),
            (qt_spec, qt_shape),
            (pl.BlockSpec((GQA_KV_HEADS, tm, LANES), lambda i: (0, i, 0)),
             jax.ShapeDtypeStruct((GQA_KV_HEADS, t, LANES), BF16)),
            vt(GQA_KV_HEADS),
            (qt_spec, qt_shape),
            (tok(DIFF_WIDTH), jax.ShapeDtypeStruct((t, DIFF_WIDTH), BF16)),
            vt(DIFF_HEADS),
            (tok(MIX_WIDTH), jax.ShapeDtypeStruct((t, MIX_WIDTH), BF16))]
    return pl.pallas_call(
        _inproj_kernel,
        grid=(t // tm,),
        in_specs=[tok(D_MODEL), _const_spec((1, D_MODEL)), _const_spec((D_MODEL, IN_COLS)),
                  tab, tab, tab, tab,
                  _const_spec((1, GQA_WIDTH)), _const_spec((1, GQA_KV_WIDTH)),
                  _const_spec((GQA_WIDTH, GQA_WIDTH))],
        out_specs=[o[0] for o in outs],
        out_shape=[o[1] for o in outs],
        compiler_params=_params(("parallel",)),
        name="inproj",
    )(x2, pre_w, w_in_bf, *tabs, qnw, knw, g64)


def _log1p(x):
    return jnp.log(1.0 + x)


def _hgrn_prepare(q, xf, v, lb, tri):
    tb = q.shape[0] // 2
    nc = tb // HGRN_CHUNK

    def rows(t):
        return jnp.concatenate([jnp.broadcast_to(t[0:1], (tb, HGRN_DIM)),
                                jnp.broadcast_to(t[1:2], (tb, HGRN_DIM))], axis=0)

    e = jnp.exp(-jnp.abs(xf))
    log_sig = jnp.minimum(xf, 0.0) - jnp.log(1.0 + e)
    c = rows(_log1p(-lb)) + log_sig
    a = rows(jnp.log(lb))
    g = jnp.maximum(a, c) + _log1p(jnp.exp(-jnp.abs(a - c)))
    k = rows(1.0 - lb) * jnp.where(xf > 0.0, e, 1.0) * (1.0 / (1.0 + e))
    qs = _silu(q)

    g_hi = g.astype(BF16)
    g_lo = (g - g_hi.astype(F32)).astype(BF16)
    halves = (slice(0, tb), slice(tb, 2 * tb))
    g2 = jnp.concatenate([g_hi, g_lo], axis=1)
    b2 = jnp.concatenate([jnp.dot(tri[h, h], g2[h], preferred_element_type=F32)
                          for h in halves], axis=0)
    b = b2[:, :HGRN_DIM] + b2[:, HGRN_DIM:]

    mid = (HGRN_CHUNK // 2, HGRN_CHUNK // 2 - 1)
    last = (HGRN_CHUNK - 1, 0)
    chunk_rows = [slice(r0, r0 + HGRN_CHUNK) for r0 in range(0, 2 * tb, HGRN_CHUNK)]
    b_mid, from_start, to_end = [], [], []
    for ci, sl in enumerate(chunk_rows):
        d = ci // nc
        mid_row = b[sl.start + mid[d]:sl.start + mid[d] + 1, :]
        last_row = b[sl.start + last[d]:sl.start + last[d] + 1, :]
        b_mid.append(jnp.broadcast_to(mid_row, (HGRN_CHUNK, HGRN_DIM)))
        from_start.append(jnp.exp(mid_row))
        to_end.append(jnp.exp(last_row - mid_row))
    b_mid = jnp.concatenate(b_mid, axis=0)
    spread = lambda rows_: jnp.concatenate(
        [jnp.broadcast_to(r, (HGRN_CHUNK, HGRN_DIM)) for r in rows_], axis=0)

    q_up = qs * jnp.exp(b - b_mid)
    k_dn = k * jnp.exp(b_mid - b)
    qm = q_up.astype(BF16)
    km = k_dn.astype(BF16)
    kp = (k_dn * spread(to_end)).astype(BF16)
    qd = (q_up * spread(from_start)).astype(BF16)
    vb = v.astype(BF16)

    o_intra = []
    for h in halves:
        scores = lax.dot_general(qm[h], km[h], NT_DIMS, preferred_element_type=F32)
        scores = jnp.where(tri[h, h] > 0, scores, 0.0).astype(BF16)
        o_intra.append(jnp.dot(scores, vb[h], preferred_element_type=F32))
    o_intra = jnp.concatenate(o_intra, axis=0)

    chunks = []
    for ci, sl in enumerate(chunk_rows):
        dec = from_start[ci] * to_end[ci]
        inc = lax.dot_general(vb[sl], kp[sl], TN_DIMS, preferred_element_type=F32)
        chunks.append((qd[sl], dec, inc))
    return o_intra, chunks


def _hgrn_scan(prepared, state_refs):
    heads = len(prepared)
    nc = len(prepared[0][1]) // 2
    states = [[state_refs[d][h] for h in range(heads)] for d in range(2)]
    o_inter = [[None] * (2 * nc) for _ in range(heads)]
    for step in range(nc):
        for h in range(heads):
            for d in range(2):
                ci = d * nc + (step if d == 0 else nc - 1 - step)
                qd, dec, inc = prepared[h][1][ci]
                o_inter[h][ci] = lax.dot_general(qd, states[d][h].astype(BF16), NT_DIMS,
                                                 preferred_element_type=F32)
                states[d][h] = states[d][h] * dec + inc
    for d in range(2):
        for h in range(heads):
            state_refs[d][h] = states[d][h]
    return [prepared[h][0] + jnp.concatenate(o_inter[h], axis=0) for h in range(heads)]


def _hgrn_kernel(qf_ref, xf_ref, vf_ref, qr_ref, xr_ref, vr_ref, lbp_ref, tri_ref,
                 of_ref, or_ref, sf_ref, sr_ref, *, layer):
    @pl.when(pl.program_id(2) == 0)
    def _():
        sf_ref[...] = jnp.zeros_like(sf_ref)
        sr_ref[...] = jnp.zeros_like(sr_ref)

    rows = [lbp_ref[l] for l in range(DEPTH)]
    top = functools.reduce(jnp.maximum, rows)
    e = [jnp.exp(r - top) for r in rows]
    den = functools.reduce(lambda u, w: u + w, e)
    lb = jnp.zeros(rows[0].shape, F32)
    for l in range(1, layer + 1):
        lb = lb + e[l] / den

    tb = qf_ref.shape[0]
    heads = qf_ref.shape[1] // HGRN_DIM
    tri = tri_ref[...]
    prepared = []
    for h in range(heads):
        lanes = slice(h * HGRN_DIM, (h + 1) * HGRN_DIM)
        stack = lambda f_ref, r_ref: jnp.concatenate([f_ref[:, lanes], r_ref[:, lanes]], axis=0)
        prepared.append(_hgrn_prepare(stack(qf_ref, qr_ref), stack(xf_ref, xr_ref),
                                      stack(vf_ref, vr_ref), lb[:, lanes], tri))
    outs = _hgrn_scan(prepared, (sf_ref, sr_ref))
    for h, out in enumerate(outs):
        lanes = slice(h * HGRN_DIM, (h + 1) * HGRN_DIM)
        of_ref[:, lanes] = out[:tb].astype(BF16)
        or_ref[:, lanes] = out[tb:].astype(BF16)


def _hgrn(pa3, hgrn_lb, tri, layer, tb):
    bsz, seq, _ = pa3.shape
    nb = seq // tb
    groups = HGRN_HEADS // HGRN_HEADS_PER_STEP
    width = HGRN_HEADS_PER_STEP * HGRN_DIM

    def fwd(col):
        return pl.BlockSpec((None, tb, width), lambda b, h, i: (b, i, col * groups + h))

    def rev(col):
        return pl.BlockSpec((None, tb, width), lambda b, h, i: (b, nb - 1 - i, col * groups + h))

    out_f = pl.BlockSpec((None, tb, width), lambda b, h, i: (b, i, h))
    out_r = pl.BlockSpec((None, tb, width), lambda b, h, i: (b, nb - 1 - i, h))
    shape = jax.ShapeDtypeStruct((bsz, seq, HGRN_WIDTH), BF16)
    state = pltpu.VMEM((HGRN_HEADS_PER_STEP, HGRN_DIM, HGRN_DIM), F32)
    return pl.pallas_call(
        functools.partial(_hgrn_kernel, layer=layer),
        grid=(bsz, groups, nb),
        in_specs=[fwd(0), fwd(1), fwd(3), rev(0), rev(2), rev(3),
                  pl.BlockSpec((DEPTH, 2, width), lambda b, h, i: (0, 0, h)),
                  _const_spec((2 * tb, 2 * tb))],
        out_specs=[out_f, out_r],
        out_shape=[shape, shape],
        scratch_shapes=[state, state],
        compiler_params=_params(("parallel", "parallel", "arbitrary")),
        name="hgrn2",
    )(pa3, pa3, pa3, pa3, pa3, pa3, hgrn_lb, tri)


def _flash_kernel(*refs, maps, n_v, n_out, tk):
    n_maps = len(maps)
    qt_ref, k_ref = refs[0], refs[1]
    v_refs = refs[2:2 + n_v]
    out_refs = refs[2 + n_v:2 + n_v + n_out]
    scratch = refs[2 + n_v + n_out:]
    acc_ref, m_ref, gap_ref, qv_ref = scratch[:4]
    per_kind = 2 * n_maps
    s_ref, p_ref, al_ref = (
        [scratch[4 + kind * per_kind + slot * n_maps:4 + kind * per_kind + (slot + 1) * n_maps]
         for slot in range(2)] for kind in range(3))
    tq = qt_ref.shape[1]
    n_chunks = k_ref.shape[0] // tk
    sub = m_ref.shape[1]

    def scores(ci, m):
        off = pl.multiple_of(ci * tk, tk)
        return jnp.dot(k_ref[pl.ds(off, tk), :], qv_ref[m], preferred_element_type=F32)

    def column_max(s):
        top = jnp.max(s.reshape(s.shape[0] // sub, sub, tq), axis=0)
        return jnp.broadcast_to(jnp.max(top, axis=0, keepdims=True), (sub, tq))

    def lagged_softmax(ci, slot):
        for m in range(n_maps):
            shift = m_ref[m]
            s = scores(ci, m)
            p_ref[slot][m][...] = jnp.exp2(s - shift[0:1, :]).astype(BF16)
            top = column_max(s)
            gap_ref[m] = jnp.maximum(gap_ref[m], top - shift)
            m_next = jnp.maximum(shift, top)
            m_ref[m] = m_next
            al_ref[slot][m][...] = jnp.exp2(shift - m_next)

    def lagged_pv(ci, slot):
        for m, (_, _, vi, _, _) in enumerate(maps):
            acc_ref[m] = (acc_ref[m] + jnp.dot(v_refs[vi][ci], p_ref[slot][m][...],
                                               preferred_element_type=F32)
                          ) * al_ref[slot][m][0:1, :]

    def lagged_trip(t, carry):
        first = t * CHUNKS_PER_TRIP
        lagged_softmax(first, 0)
        for u in range(1, CHUNKS_PER_TRIP):
            lagged_softmax(first + u, u % 2)
            lagged_pv(first + u - 1, (u - 1) % 2)
        lagged_pv(first + CHUNKS_PER_TRIP - 1, (CHUNKS_PER_TRIP - 1) % 2)
        return carry

    def exact_chunk(ci, slot):
        for m in range(n_maps):
            s_ref[slot][m][...] = scores(ci, m)
        for m, (_, _, vi, _, _) in enumerate(maps):
            m_prev = m_ref[m]
            m_next = jnp.maximum(m_prev, column_max(s_ref[slot][m][...]))
            m_ref[m] = m_next
            for c0 in range(0, tq, COL_BLOCK):
                cols = slice(c0, c0 + COL_BLOCK)
                p_ref[slot][m][:, cols] = jnp.exp2(
                    s_ref[slot][m][:, cols] - m_next[0:1, cols]).astype(BF16)
            acc_ref[m] = acc_ref[m] * jnp.exp2(m_prev - m_next)[0:1, :] + jnp.dot(
                v_refs[vi][ci], p_ref[slot][m][...], preferred_element_type=F32)

    def exact_trip(t, carry):
        for u in range(CHUNKS_PER_TRIP):
            exact_chunk(t * CHUNKS_PER_TRIP + u, u % 2)
        return carry

    qt = qt_ref[...]
    row = lax.broadcasted_iota(jnp.int32, qt.shape, 0)
    for m, (lo, hi, _, _, _) in enumerate(maps):
        qv_ref[m] = jnp.where((row >= lo) & (row < hi), qt, jnp.zeros_like(qt))

    acc_ref[...] = jnp.zeros_like(acc_ref)
    gap_ref[...] = jnp.zeros_like(gap_ref)
    for m in range(n_maps):
        m_ref[m] = column_max(jnp.dot(k_ref[0:LANES, :], qv_ref[m], preferred_element_type=F32))
    lax.fori_loop(0, n_chunks // CHUNKS_PER_TRIP, lagged_trip, 0)

    @pl.when(jnp.logical_not(jnp.max(gap_ref[...]) <= MAX_LAG_LOG2))
    def _():
        acc_ref[...] = jnp.zeros_like(acc_ref)
        m_ref[...] = jnp.full_like(m_ref, NEG_BIG)
        lax.fori_loop(0, n_chunks // CHUNKS_PER_TRIP, exact_trip, 0)

    for oi, out_ref in enumerate(out_refs):
        for m, (_, _, _, mo, half) in enumerate(maps):
            if mo == oi:
                acc = acc_ref[m]
                out_ref[half * GQA_HEAD_DIM:(half + 1) * GQA_HEAD_DIM, :] = (
                    acc[0:GQA_HEAD_DIM, :] * (1.0 / acc[GQA_HEAD_DIM:GQA_HEAD_DIM + 1, :])
                ).astype(BF16)


def _flash(qt3, k4, k_index, vt5, v_indices, maps, n_out, tq, name):
    bsz, qw, seq = qt3.shape
    tk = vt5.shape[-1]
    assert seq % tq == 0 and (seq // tk) % CHUNKS_PER_TRIP == 0, (seq, tq, tk)
    nqb = qw // LANES
    n_maps = len(maps)
    n_v = len(v_indices(0))
    per_kind = 2 * n_maps
    once = pl.Buffered(1)

    def kmap(b, j, qi):
        n, lb = k_index(b, j)
        return (n, b, 0, lb)

    kspec = pl.BlockSpec((None, None, seq, LANES), kmap, pipeline_mode=once)
    vspecs = [pl.BlockSpec((None, None, seq // tk, VALUE_ROWS, tk),
                           lambda b, j, qi, n=n: (v_indices(j)[n], b, 0, 0, 0), pipeline_mode=once)
              for n in range(n_v)]
    qspec = pl.BlockSpec((None, LANES, tq), lambda b, j, qi: (b, j, qi))
    return pl.pallas_call(
        functools.partial(_flash_kernel, maps=maps, n_v=n_v, n_out=n_out, tk=tk),
        grid=(bsz, nqb, seq // tq),
        in_specs=[qspec, kspec] + vspecs,
        out_specs=[qspec] * n_out,
        out_shape=[jax.ShapeDtypeStruct((bsz, qw, seq), BF16)] * n_out,
        scratch_shapes=([pltpu.VMEM((n_maps, VALUE_ROWS, tq), F32),
                         pltpu.VMEM((n_maps, SUBLANES, tq), F32),
                         pltpu.VMEM((n_maps, SUBLANES, tq), F32),
                         pltpu.VMEM((n_maps, LANES, tq), BF16)]
                        + [pltpu.VMEM((tk, tq), F32)] * per_kind
                        + [pltpu.VMEM((tk, tq), BF16)] * per_kind
                        + [pltpu.VMEM((SUBLANES, tq), F32)] * per_kind),
        compiler_params=_params(("parallel", "parallel", "parallel")),
        name=name,
    )(qt3, k4, *([vt5] * n_v))


GQA_MAPS = ((0, GQA_HEAD_DIM, 0, 0, 0), (GQA_HEAD_DIM, LANES, 0, 0, 1))
DIFF_MAPS = tuple((m * DIFF_HEAD_DIM, (m + 1) * DIFF_HEAD_DIM, m // 2, m % 2, m // 2)
                  for m in range(4))


def _outproj_kernel(x_ref, of_ref, or_ref, ob_ref, oc1_ref, oc2_ref, sg_ref, wo_ref,
                    hnw_ref, dnw_ref, lam_ref, postw_ref, g128_ref, g64_ref, out_ref, *, lam_init):
    sg = sg_ref[...].astype(F32)
    a = of_ref[...].astype(F32) + or_ref[...].astype(F32)
    ms = jnp.dot((a * a).astype(BF16), g128_ref[...], preferred_element_type=F32)
    mix_a = a * lax.rsqrt(ms + NORM_EPS) * hnw_ref[...] * sg[:, :HGRN_WIDTH]
    mix_b = ob_ref[...].astype(F32).T * sg[:, HGRN_WIDTH:HGRN_WIDTH + GQA_WIDTH]

    lp = lam_ref[...]
    lam = (jnp.exp(jnp.sum(lp[0:1] * lp[1:2], axis=-1, keepdims=True))
           - jnp.exp(jnp.sum(lp[2:3] * lp[3:4], axis=-1, keepdims=True)) + lam_init)
    c = (oc1_ref[...].astype(F32) - lam * oc2_ref[...].astype(F32)).T
    ms = jnp.dot((c * c).astype(BF16), g64_ref[...], preferred_element_type=F32)
    mix_c = (c * lax.rsqrt(ms + NORM_EPS) * dnw_ref[...] * (1.0 - lam_init)
             * sg[:, HGRN_WIDTH + GQA_WIDTH:])

    y = (jnp.dot(mix_a.astype(BF16), wo_ref[0:HGRN_WIDTH, :], preferred_element_type=F32)
         + jnp.dot(mix_b.astype(BF16), wo_ref[HGRN_WIDTH:HGRN_WIDTH + GQA_WIDTH, :],
                   preferred_element_type=F32)
         + jnp.dot(mix_c.astype(BF16), wo_ref[HGRN_WIDTH + GQA_WIDTH:, :],
                   preferred_element_type=F32))
    out_ref[...] = x_ref[...] + (y * lax.rsqrt(jnp.mean(y * y, axis=-1, keepdims=True) + NORM_EPS)
                                 * postw_ref[...])


def _outproj(x2, of2, or2, obt, oc1t, oc2t, sg, wo_bf, hnw, dnw, lam_p, postw, g128, g64,
             lam_init, tm):
    t = x2.shape[0]
    tiles_per_seq = obt.shape[2] // tm
    tok = lambda w: pl.BlockSpec((tm, w), lambda i: (i, 0))
    att = pl.BlockSpec((None, GQA_WIDTH, tm), lambda i: (i // tiles_per_seq, 0, i % tiles_per_seq))
    return pl.pallas_call(
        functools.partial(_outproj_kernel, lam_init=lam_init),
        grid=(t // tm,),
        in_specs=[tok(D_MODEL), tok(HGRN_WIDTH), tok(HGRN_WIDTH), att, att, att, tok(MIX_WIDTH),
                  _const_spec((MIX_WIDTH, D_MODEL)),
                  _const_spec((1, HGRN_WIDTH)), _const_spec((1, DIFF_WIDTH)),
                  _const_spec((4, DIFF_HEAD_DIM)), _const_spec((1, D_MODEL)),
                  _const_spec((HGRN_WIDTH, HGRN_WIDTH)), _const_spec((DIFF_WIDTH, DIFF_WIDTH))],
        out_specs=tok(D_MODEL),
        out_shape=jax.ShapeDtypeStruct((t, D_MODEL), F32),
        compiler_params=_params(("parallel",)),
        name="outproj",
    )(x2, of2, or2, obt, oc1t, oc2t, sg, wo_bf, hnw, dnw, lam_p, postw, g128, g64)


def _group_mean_matrix(width, group):
    idx = jnp.arange(width) // group
    return ((idx[:, None] == idx[None, :]).astype(F32) / group).astype(BF16)


def _rope_tables(seq_len):
    half = ROPE_DIM // 2
    inv = jnp.power(ROPE_THETA, -jnp.arange(0, ROPE_DIM, 2, dtype=F32) / ROPE_DIM)
    pos = jnp.arange(seq_len, dtype=F32)
    rows = seq_len // GRID_W
    row_pos = jnp.repeat(jnp.arange(rows, dtype=F32), GRID_W)
    col_pos = jnp.tile(jnp.arange(GRID_W, dtype=F32), rows)
    sign = jnp.concatenate([-jnp.ones((half,), F32), jnp.ones((half,), F32)])

    def tab(p):
        ang = p[:, None] * inv[None, :]
        ang = jnp.concatenate([ang, ang], axis=-1)
        return jnp.cos(ang), jnp.sin(ang) * sign

    cr, sr = tab(row_pos)
    cc, sc = tab(col_pos)
    c1, s1 = tab(pos)
    rep_b = GQA_WIDTH // (2 * ROPE_DIM)
    rep_c = DIFF_WIDTH // ROPE_DIM
    return (jnp.tile(jnp.concatenate([cr, cc], axis=-1), (1, rep_b)),
            jnp.tile(jnp.concatenate([sr, sc], axis=-1), (1, rep_b)),
            jnp.tile(c1, (1, rep_c)), jnp.tile(s1, (1, rep_c)))


def _chunk_tri(tb):
    r = jnp.arange(2 * tb)
    same = (r[:, None] // HGRN_CHUNK) == (r[None, :] // HGRN_CHUNK)
    tri = jnp.where(r[:, None] < tb, r[None, :] <= r[:, None], r[None, :] >= r[:, None])
    return (same & tri).astype(BF16)


def _trunk(x, pre_norm_w, w_in_bf, hgrn_lb, hgrn_norm_w, gqa_q_norm_w, gqa_k_norm_w,
           diff_lambda, diff_norm_w, w_out_bf, post_norm_w, *, tabs, g64, g128, tri,
           tm, tb, tq):
    bsz, seq, _ = x.shape
    t = bsz * seq
    x2 = x.reshape(t, D_MODEL)
    for layer in range(DEPTH):
        qnw = jnp.tile(gqa_q_norm_w[layer], GQA_HEADS)[None, :]
        knw = jnp.tile(gqa_k_norm_w[layer], GQA_KV_HEADS)[None, :]
        pa, qbt, kdup, vbt, qct, kc, vct, sg = _inproj(
            x2, bsz, seq, pre_norm_w[layer][None, :], w_in_bf[layer], tabs, qnw, knw, g64, tm)

        o_f, o_r = _hgrn(pa.reshape(bsz, seq, A_COLS), hgrn_lb, tri, layer, tb)

        (obt,) = _flash(qbt, kdup.reshape(GQA_KV_HEADS, bsz, seq, LANES), lambda b, j: (j, 0),
                        vbt, lambda j: (j,), GQA_MAPS, 1, tq, "flash_gqa")

        oc1t, oc2t = _flash(qct, kc.reshape(1, bsz, seq, -1), lambda b, j: (0, j),
                            vct, lambda j: (2 * j, 2 * j + 1), DIFF_MAPS, 2, tq, "flash_diff")

        lam_init = 0.8 - 0.6 * math.exp(-0.3 * layer)
        x2 = _outproj(x2, o_f.reshape(t, -1), o_r.reshape(t, -1), obt, oc1t, oc2t, sg,
                      w_out_bf[layer], hgrn_norm_w[layer][None, :],
                      jnp.tile(diff_norm_w[layer], DIFF_HEADS)[None, :],
                      diff_lambda[layer], post_norm_w[layer][None, :], g128, g64, lam_init, tm)
    return x2.reshape(bsz, seq, D_MODEL)


def kernel(x_prompt, x_sample, pre_norm_w, w_in, hgrn_lb, hgrn_norm_w, gqa_q_norm_w, gqa_k_norm_w,
           diff_lambda, diff_norm_w, w_out, post_norm_w):
    w_in_bf = w_in.astype(BF16)
    w_out_bf = w_out.astype(BF16)
    longest = max(x_prompt.shape[1], x_sample.shape[1])
    run = functools.partial(_trunk, pre_norm_w=pre_norm_w, w_in_bf=w_in_bf, hgrn_lb=hgrn_lb,
                            hgrn_norm_w=hgrn_norm_w, gqa_q_norm_w=gqa_q_norm_w,
                            gqa_k_norm_w=gqa_k_norm_w, diff_lambda=diff_lambda,
                            diff_norm_w=diff_norm_w, w_out_bf=w_out_bf, post_norm_w=post_norm_w,
                            tabs=_rope_tables(longest),
                            g64=_group_mean_matrix(GQA_WIDTH, GQA_HEAD_DIM),
                            g128=_group_mean_matrix(HGRN_WIDTH, HGRN_DIM),
                            tri=_chunk_tri(TILES["hgrn_block"]),
                            tm=TILES["proj_rows"], tb=TILES["hgrn_block"],
                            tq=TILES["attn_queries"])
    return (run(x_prompt), run(x_sample))
```

```python
import functools
import math

import jax
import jax.numpy as jnp
from jax import lax
from jax.experimental import pallas as pl
from jax.experimental.pallas import tpu as pltpu

F32 = jnp.float32
BF16 = jnp.bfloat16

D_MODEL = 1024
DEPTH = 2
GRID_W = 64
HGRN_HEADS = 4
HGRN_DIM = 128
HGRN_WIDTH = HGRN_HEADS * HGRN_DIM
HGRN_CHUNK = 32
HGRN_HEADS_PER_STEP = 4
GQA_HEADS = 4
GQA_KV_HEADS = 2
GQA_HEAD_DIM = 64
GQA_WIDTH = GQA_HEADS * GQA_HEAD_DIM
GQA_KV_WIDTH = GQA_KV_HEADS * GQA_HEAD_DIM
DIFF_HEADS = 4
DIFF_HEAD_DIM = 32
DIFF_WIDTH = DIFF_HEADS * 2 * DIFF_HEAD_DIM
MIX_WIDTH = HGRN_WIDTH + GQA_WIDTH + DIFF_WIDTH
ROPE_THETA = 10000.0
ROPE_DIM = 32
NORM_EPS = 1e-6

A_COLS = 4 * HGRN_WIDTH
OFF_AG = A_COLS
OFF_B = OFF_AG + HGRN_WIDTH
OFF_BG = OFF_B + GQA_WIDTH + 2 * GQA_KV_WIDTH
OFF_C = OFF_BG + GQA_WIDTH
IN_COLS = OFF_C + 4 * DIFF_WIDTH

LANES = 128
SUBLANES = 8
VMEM_LIMIT = 56 * 1024 * 1024
NEG_BIG = -1e30
LOG2E = math.log2(math.e)

VALUE_ROWS = 80
COL_BLOCK = 128
CHUNKS_PER_TRIP = 4
MAX_LAG_LOG2 = 100.0

TILES = {"proj_rows": 512, "hgrn_block": 256, "attn_queries": 1024}

NT_DIMS = (((1,), (1,)), ((), ()))
TN_DIMS = (((0,), (0,)), ((), ()))


def _params(sem):
    return pltpu.CompilerParams(dimension_semantics=sem, vmem_limit_bytes=VMEM_LIMIT)


def _const_spec(shape):
    nd = len(shape)
    return pl.BlockSpec(shape, lambda *_: (0,) * nd)


def _rot_half16(x):
    outs = []
    for s in range(x.shape[1] // LANES):
        xs = x[:, s * LANES:(s + 1) * LANES]
        up = pltpu.roll(xs, ROPE_DIM // 2, 1)
        dn = pltpu.roll(xs, LANES - ROPE_DIM // 2, 1)
        lane = lax.broadcasted_iota(jnp.int32, xs.shape, 1)
        outs.append(jnp.where((lane & (ROPE_DIM - 1)) < ROPE_DIM // 2, dn, up))
    return outs[0] if len(outs) == 1 else jnp.concatenate(outs, axis=1)


def _silu(x):
    return x * (1.0 / (1.0 + jnp.exp(-x)))


def _inproj_kernel(x_ref, pw_ref, w_ref, cosb_ref, sinb_ref, cosc_ref, sinc_ref,
                   qnw_ref, knw_ref, g64_ref,
                   pa_ref, qbt_ref, kd_ref, vbt_ref, qct_ref, kc_ref, vct_ref, sg_ref):
    x = x_ref[...]
    h = x * lax.rsqrt(jnp.mean(x * x, axis=-1, keepdims=True) + NORM_EPS) * pw_ref[...]
    hb = h.astype(BF16)

    def proj(lo, hi):
        return jnp.dot(hb, w_ref[:, lo:hi], preferred_element_type=F32)

    pa_ref[...] = proj(0, A_COLS)
    sg_ref[:, 0:HGRN_WIDTH] = _silu(proj(OFF_AG, OFF_B)).astype(BF16)

    def head_rms(t, w, g):
        ms = jnp.dot((t * t).astype(BF16), g, preferred_element_type=F32)
        return t * lax.rsqrt(ms + NORM_EPS) * w

    def rope(t, cos, sin):
        return t * cos + _rot_half16(t) * sin

    lane = lax.broadcasted_iota(jnp.int32, (x.shape[0], LANES), 1)
    low = lane < GQA_HEAD_DIM

    def store_values(v, vt_ref):
        for s in range(v.shape[1] // LANES):
            slab = v[:, s * LANES:(s + 1) * LANES]
            for half, src in enumerate((slab, pltpu.roll(slab, GQA_HEAD_DIM, 1))):
                aug = jnp.where(low, src, jnp.where(lane == GQA_HEAD_DIM, 1.0, 0.0))
                vt_ref[2 * s + half] = aug.T[0:VALUE_ROWS, :].astype(BF16)

    g64 = g64_ref[...]
    cosb = cosb_ref[...]
    sinb = sinb_ref[...]
    bq = head_rms(proj(OFF_B, OFF_B + GQA_WIDTH), qnw_ref[...], g64)
    qbt_ref[...] = (rope(bq, cosb, sinb) * (LOG2E / math.sqrt(GQA_HEAD_DIM))).T.astype(BF16)
    off_k = OFF_B + GQA_WIDTH
    bk = head_rms(proj(off_k, off_k + GQA_KV_WIDTH), knw_ref[...],
                  g64[:GQA_KV_WIDTH, :GQA_KV_WIDTH])
    k = rope(bk, cosb[:, :GQA_KV_WIDTH], sinb[:, :GQA_KV_WIDTH])
    k_swapped = pltpu.roll(k, GQA_HEAD_DIM, 1)
    kd_ref[0] = jnp.where(low, k, k_swapped).astype(BF16)
    kd_ref[1] = jnp.where(low, k_swapped, k).astype(BF16)
    off_v = off_k + GQA_KV_WIDTH
    store_values(proj(off_v, OFF_BG), vbt_ref)
    sg_ref[:, HGRN_WIDTH:HGRN_WIDTH + GQA_WIDTH] = _silu(proj(OFF_BG, OFF_C)).astype(BF16)

    cosc = cosc_ref[...]
    sinc = sinc_ref[...]
    cq = proj(OFF_C, OFF_C + DIFF_WIDTH)
    qct_ref[...] = (rope(cq, cosc, sinc) * (LOG2E / math.sqrt(DIFF_HEAD_DIM))).T.astype(BF16)
    ck = proj(OFF_C + DIFF_WIDTH, OFF_C + 2 * DIFF_WIDTH)
    kc_ref[...] = rope(ck, cosc, sinc).astype(BF16)
    store_values(proj(OFF_C + 2 * DIFF_WIDTH, OFF_C + 3 * DIFF_WIDTH), vct_ref)
    sg_ref[:, HGRN_WIDTH + GQA_WIDTH:] = _silu(proj(OFF_C + 3 * DIFF_WIDTH, IN_COLS)).astype(BF16)


def _inproj(x2, bsz, seq_len, pre_w, w_in_bf, tabs, qnw, knw, g64, tm):
    t = x2.shape[0]
    tiles_per_seq = seq_len // tm
    tok = lambda w: pl.BlockSpec((tm, w), lambda i: (i, 0))
    tab = pl.BlockSpec((tm, GQA_WIDTH), lambda i: (i % tiles_per_seq, 0))
    qt_spec = pl.BlockSpec((None, GQA_WIDTH, tm),
                           lambda i: (i // tiles_per_seq, 0, i % tiles_per_seq))
    qt_shape = jax.ShapeDtypeStruct((bsz, GQA_WIDTH, seq_len), BF16)

    def vt(heads):
        return (pl.BlockSpec((heads, None, None, VALUE_ROWS, tm),
                             lambda i: (0, i // tiles_per_seq, i % tiles_per_seq, 0, 0)),
                jax.ShapeDtypeStruct((heads, bsz, tiles_per_seq, VALUE_ROWS, tm), BF16))

    outs = [(tok(A_COLS), jax.ShapeDtypeStruct((t, A_COLS), F32)),
            (qt_spec, qt_shape),
            (pl.BlockSpec((GQA_KV_HEADS, tm, LANES), lambda i: (0, i, 0)),
             jax.ShapeDtypeStruct((GQA_KV_HEADS, t, LANES), BF16)),
            vt(GQA_KV_HEADS),
            (qt_spec, qt_shape),
            (tok(DIFF_WIDTH), jax.ShapeDtypeStruct((t, DIFF_WIDTH), BF16)),
            vt(DIFF_HEADS),
            (tok(MIX_WIDTH), jax.ShapeDtypeStruct((t, MIX_WIDTH), BF16))]
    return pl.pallas_call(
        _inproj_kernel,
        grid=(t // tm,),
        in_specs=[tok(D_MODEL), _const_spec((1, D_MODEL)), _const_spec((D_MODEL, IN_COLS)),
                  tab, tab, tab, tab,
                  _const_spec((1, GQA_WIDTH)), _const_spec((1, GQA_KV_WIDTH)),
                  _const_spec((GQA_WIDTH, GQA_WIDTH))],
        out_specs=[o[0] for o in outs],
        out_shape=[o[1] for o in outs],
        compiler_params=_params(("parallel",)),
        name="inproj",
    )(x2, pre_w, w_in_bf, *tabs, qnw, knw, g64)


def _log1p(x):
    return jnp.log(1.0 + x)


def _hgrn_prepare(q, xf, v, lb, tri):
    tb = q.shape[0] // 2
    nc = tb // HGRN_CHUNK

    def rows(t):
        return jnp.concatenate([jnp.broadcast_to(t[0:1], (tb, HGRN_DIM)),
                                jnp.broadcast_to(t[1:2], (tb, HGRN_DIM))], axis=0)

    e = jnp.exp(-jnp.abs(xf))
    log_sig = jnp.minimum(xf, 0.0) - jnp.log(1.0 + e)
    c = rows(_log1p(-lb)) + log_sig
    a = rows(jnp.log(lb))
    g = jnp.maximum(a, c) + _log1p(jnp.exp(-jnp.abs(a - c)))
    k = rows(1.0 - lb) * jnp.where(xf > 0.0, e, 1.0) * (1.0 / (1.0 + e))
    qs = _silu(q)

    g_hi = g.astype(BF16)
    g_lo = (g - g_hi.astype(F32)).astype(BF16)
    halves = (slice(0, tb), slice(tb, 2 * tb))
    g2 = jnp.concatenate([g_hi, g_lo], axis=1)
    b2 = jnp.concatenate([jnp.dot(tri[h, h], g2[h], preferred_element_type=F32)
                          for h in halves], axis=0)
    b = b2[:, :HGRN_DIM] + b2[:, HGRN_DIM:]

    mid = (HGRN_CHUNK // 2, HGRN_CHUNK // 2 - 1)
    last = (HGRN_CHUNK - 1, 0)
    chunk_rows = [slice(r0, r0 + HGRN_CHUNK) for r0 in range(0, 2 * tb, HGRN_CHUNK)]
    b_mid, from_start, to_end = [], [], []
    for ci, sl in enumerate(chunk_rows):
        d = ci // nc
        mid_row = b[sl.start + mid[d]:sl.start + mid[d] + 1, :]
        last_row = b[sl.start + last[d]:sl.start + last[d] + 1, :]
        b_mid.append(jnp.broadcast_to(mid_row, (HGRN_CHUNK, HGRN_DIM)))
        from_start.append(jnp.exp(mid_row))
        to_end.append(jnp.exp(last_row - mid_row))
    b_mid = jnp.concatenate(b_mid, axis=0)
    spread = lambda rows_: jnp.concatenate(
        [jnp.broadcast_to(r, (HGRN_CHUNK, HGRN_DIM)) for r in rows_], axis=0)

    q_up = qs * jnp.exp(b - b_mid)
    k_dn = k * jnp.exp(b_mid - b)
    qm = q_up.astype(BF16)
    km = k_dn.astype(BF16)
    kp = (k_dn * spread(to_end)).astype(BF16)
    qd = (q_up * spread(from_start)).astype(BF16)
    vb = v.astype(BF16)

    o_intra = []
    for h in halves:
        scores = lax.dot_general(qm[h], km[h], NT_DIMS, preferred_element_type=F32)
        scores = jnp.where(tri[h, h] > 0, scores, 0.0).astype(BF16)
        o_intra.append(jnp.dot(scores, vb[h], preferred_element_type=F32))
    o_intra = jnp.concatenate(o_intra, axis=0)

    chunks = []
    for ci, sl in enumerate(chunk_rows):
        dec = from_start[ci] * to_end[ci]
        inc = lax.dot_general(vb[sl], kp[sl], TN_DIMS, preferred_element_type=F32)
        chunks.append((qd[sl], dec, inc))
    return o_intra, chunks


def _hgrn_scan(prepared, state_refs):
    heads = len(prepared)
    nc = len(prepared[0][1]) // 2
    states = [[state_refs[d][h] for h in range(heads)] for d in range(2)]
    o_inter = [[None] * (2 * nc) for _ in range(heads)]
    for step in range(nc):
        for h in range(heads):
            for d in range(2):
                ci = d * nc + (step if d == 0 else nc - 1 - step)
                qd, dec, inc = prepared[h][1][ci]
                o_inter[h][ci] = jnp.dot(qd, states[d][h].T.astype(BF16),
                                         preferred_element_type=F32)
                states[d][h] = states[d][h] * dec + inc
    for d in range(2):
        for h in range(heads):
            state_refs[d][h] = states[d][h]
    return [prepared[h][0] + jnp.concatenate(o_inter[h], axis=0) for h in range(heads)]


def _hgrn_kernel(qf_ref, xf_ref, vf_ref, qr_ref, xr_ref, vr_ref, lbp_ref, tri_ref,
                 of_ref, or_ref, sf_ref, sr_ref, *, layer):
    @pl.when(pl.program_id(2) == 0)
    def _():
        sf_ref[...] = jnp.zeros_like(sf_ref)
        sr_ref[...] = jnp.zeros_like(sr_ref)

    rows = [lbp_ref[l] for l in range(DEPTH)]
    top = functools.reduce(jnp.maximum, rows)
    e = [jnp.exp(r - top) for r in rows]
    den = functools.reduce(lambda u, w: u + w, e)
    lb = jnp.zeros(rows[0].shape, F32)
    for l in range(1, layer + 1):
        lb = lb + e[l] / den

    tb = qf_ref.shape[0]
    heads = qf_ref.shape[1] // HGRN_DIM
    tri = tri_ref[...]
    prepared = []
    for h in range(heads):
        lanes = slice(h * HGRN_DIM, (h + 1) * HGRN_DIM)
        stack = lambda f_ref, r_ref: jnp.concatenate([f_ref[:, lanes], r_ref[:, lanes]], axis=0)
        prepared.append(_hgrn_prepare(stack(qf_ref, qr_ref), stack(xf_ref, xr_ref),
                                      stack(vf_ref, vr_ref), lb[:, lanes], tri))
    outs = _hgrn_scan(prepared, (sf_ref, sr_ref))
    for h, out in enumerate(outs):
        lanes = slice(h * HGRN_DIM, (h + 1) * HGRN_DIM)
        of_ref[:, lanes] = out[:tb].astype(BF16)
        or_ref[:, lanes] = out[tb:].astype(BF16)


def _hgrn(pa3, hgrn_lb, tri, layer, tb):
    bsz, seq, _ = pa3.shape
    nb = seq // tb
    groups = HGRN_HEADS // HGRN_HEADS_PER_STEP
    width = HGRN_HEADS_PER_STEP * HGRN_DIM

    def fwd(col):
        return pl.BlockSpec((None, tb, width), lambda b, h, i: (b, i, col * groups + h))

    def rev(col):
        return pl.BlockSpec((None, tb, width), lambda b, h, i: (b, nb - 1 - i, col * groups + h))

    out_f = pl.BlockSpec((None, tb, width), lambda b, h, i: (b, i, h))
    out_r = pl.BlockSpec((None, tb, width), lambda b, h, i: (b, nb - 1 - i, h))
    shape = jax.ShapeDtypeStruct((bsz, seq, HGRN_WIDTH), BF16)
    state = pltpu.VMEM((HGRN_HEADS_PER_STEP, HGRN_DIM, HGRN_DIM), F32)
    return pl.pallas_call(
        functools.partial(_hgrn_kernel, layer=layer),
        grid=(bsz, groups, nb),
        in_specs=[fwd(0), fwd(1), fwd(3), rev(0), rev(2), rev(3),
                  pl.BlockSpec((DEPTH, 2, width), lambda b, h, i: (0, 0, h)),
                  _const_spec((2 * tb, 2 * tb))],
        out_specs=[out_f, out_r],
        out_shape=[shape, shape],
        scratch_shapes=[state, state],
        compiler_params=_params(("parallel", "parallel", "arbitrary")),
        name="hgrn2",
    )(pa3, pa3, pa3, pa3, pa3, pa3, hgrn_lb, tri)


def _flash_kernel(*refs, maps, n_v, n_out, tk):
    n_maps = len(maps)
    qt_ref, k_ref = refs[0], refs[1]
    v_refs = refs[2:2 + n_v]
    out_refs = refs[2 + n_v:2 + n_v + n_out]
    scratch = refs[2 + n_v + n_out:]
    acc_ref, m_ref, gap_ref, qv_ref = scratch[:4]
    per_kind = 2 * n_maps
    s_ref, p_ref, al_ref = (
        [scratch[4 + kind * per_kind + slot * n_maps:4 + kind * per_kind + (slot + 1) * n_maps]
         for slot in range(2)] for kind in range(3))
    tq = qt_ref.shape[1]
    n_chunks = k_ref.shape[0] // tk
    sub = m_ref.shape[1]

    def scores(ci, m):
        off = pl.multiple_of(ci * tk, tk)
        return jnp.dot(k_ref[pl.ds(off, tk), :], qv_ref[m], preferred_element_type=F32)

    def column_max(s):
        top = jnp.max(s.reshape(s.shape[0] // sub, sub, tq), axis=0)
        return jnp.broadcast_to(jnp.max(top, axis=0, keepdims=True), (sub, tq))

    def lagged_softmax(ci, slot):
        for m in range(n_maps):
            shift = m_ref[m]
            s = scores(ci, m)
            p_ref[slot][m][...] = jnp.exp2(s - shift[0:1, :]).astype(BF16)
            top = column_max(s)
            gap_ref[m] = jnp.maximum(gap_ref[m], top - shift)
            m_next = jnp.maximum(shift, top)
            m_ref[m] = m_next
            al_ref[slot][m][...] = jnp.exp2(shift - m_next)

    def lagged_pv(ci, slot):
        for m, (_, _, vi, _, _) in enumerate(maps):
            acc_ref[m] = (acc_ref[m] + jnp.dot(v_refs[vi][ci], p_ref[slot][m][...],
                                               preferred_element_type=F32)
                          ) * al_ref[slot][m][0:1, :]

    def lagged_trip(t, carry):
        first = t * CHUNKS_PER_TRIP
        lagged_softmax(first, 0)
        for u in range(1, CHUNKS_PER_TRIP):
            lagged_softmax(first + u, u % 2)
            lagged_pv(first + u - 1, (u - 1) % 2)
        lagged_pv(first + CHUNKS_PER_TRIP - 1, (CHUNKS_PER_TRIP - 1) % 2)
        return carry

    def exact_chunk(ci, slot):
        for m in range(n_maps):
            s_ref[slot][m][...] = scores(ci, m)
        for m, (_, _, vi, _, _) in enumerate(maps):
            m_prev = m_ref[m]
            m_next = jnp.maximum(m_prev, column_max(s_ref[slot][m][...]))
            m_ref[m] = m_next
            for c0 in range(0, tq, COL_BLOCK):
                cols = slice(c0, c0 + COL_BLOCK)
                p_ref[slot][m][:, cols] = jnp.exp2(
                    s_ref[slot][m][:, cols] - m_next[0:1, cols]).astype(BF16)
            acc_ref[m] = acc_ref[m] * jnp.exp2(m_prev - m_next)[0:1, :] + jnp.dot(
                v_refs[vi][ci], p_ref[slot][m][...], preferred_element_type=F32)

    def exact_trip(t, carry):
        for u in range(CHUNKS_PER_TRIP):
            exact_chunk(t * CHUNKS_PER_TRIP + u, u % 2)
        return carry

    qt = qt_ref[...]
    row = lax.broadcasted_iota(jnp.int32, qt.shape, 0)
    for m, (lo, hi, _, _, _) in enumerate(maps):
        qv_ref[m] = jnp.where((row >= lo) & (row < hi), qt, jnp.zeros_like(qt))

    acc_ref[...] = jnp.zeros_like(acc_ref)
    gap_ref[...] = jnp.zeros_like(gap_ref)
    for m in range(n_maps):
        m_ref[m] = column_max(jnp.dot(k_ref[0:LANES, :], qv_ref[m], preferred_element_type=F32))
    lax.fori_loop(0, n_chunks // CHUNKS_PER_TRIP, lagged_trip, 0)

    @pl.when(jnp.logical_not(jnp.max(gap_ref[...]) <= MAX_LAG_LOG2))
    def _():
        acc_ref[...] = jnp.zeros_like(acc_ref)
        m_ref[...] = jnp.full_like(m_ref, NEG_BIG)
        lax.fori_loop(0, n_chunks // CHUNKS_PER_TRIP, exact_trip, 0)

    for oi, out_ref in enumerate(out_refs):
        for m, (_, _, _, mo, half) in enumerate(maps):
            if mo == oi:
                acc = acc_ref[m]
                out_ref[half * GQA_HEAD_DIM:(half + 1) * GQA_HEAD_DIM, :] = (
                    acc[0:GQA_HEAD_DIM, :] * (1.0 / acc[GQA_HEAD_DIM:GQA_HEAD_DIM + 1, :])
                ).astype(BF16)


def _flash(qt3, k4, k_index, vt5, v_indices, maps, n_out, tq, name):
    bsz, qw, seq = qt3.shape
    tk = vt5.shape[-1]
    assert seq % tq == 0 and (seq // tk) % CHUNKS_PER_TRIP == 0, (seq, tq, tk)
    nqb = qw // LANES
    n_maps = len(maps)
    n_v = len(v_indices(0))
    per_kind = 2 * n_maps
    once = pl.Buffered(1)

    def kmap(b, j, qi):
        n, lb = k_index(b, j)
        return (n, b, 0, lb)

    kspec = pl.BlockSpec((None, None, seq, LANES), kmap, pipeline_mode=once)
    vspecs = [pl.BlockSpec((None, None, seq // tk, VALUE_ROWS, tk),
                           lambda b, j, qi, n=n: (v_indices(j)[n], b, 0, 0, 0), pipeline_mode=once)
              for n in range(n_v)]
    qspec = pl.BlockSpec((None, LANES, tq), lambda b, j, qi: (b, j, qi))
    return pl.pallas_call(
        functools.partial(_flash_kernel, maps=maps, n_v=n_v, n_out=n_out, tk=tk),
        grid=(bsz, nqb, seq // tq),
        in_specs=[qspec, kspec] + vspecs,
        out_specs=[qspec] * n_out,
        out_shape=[jax.ShapeDtypeStruct((bsz, qw, seq), BF16)] * n_out,
        scratch_shapes=([pltpu.VMEM((n_maps, VALUE_ROWS, tq), F32),
                         pltpu.VMEM((n_maps, SUBLANES, tq), F32),
                         pltpu.VMEM((n_maps, SUBLANES, tq), F32),
                         pltpu.VMEM((n_maps, LANES, tq), BF16)]
                        + [pltpu.VMEM((tk, tq), F32)] * per_kind
                        + [pltpu.VMEM((tk, tq), BF16)] * per_kind
                        + [pltpu.VMEM((SUBLANES, tq), F32)] * per_kind),
        compiler_params=_params(("parallel", "parallel", "parallel")),
        name=name,
    )(qt3, k4, *([vt5] * n_v))


GQA_MAPS = ((0, GQA_HEAD_DIM, 0, 0, 0), (GQA_HEAD_DIM, LANES, 0, 0, 1))
DIFF_MAPS = tuple((m * DIFF_HEAD_DIM, (m + 1) * DIFF_HEAD_DIM, m // 2, m % 2, m // 2)
                  for m in range(4))


def _outproj_kernel(x_ref, of_ref, or_ref, ob_ref, oc1_ref, oc2_ref, sg_ref, wo_ref,
                    hnw_ref, dnw_ref, lam_ref, postw_ref, g128_ref, g64_ref, out_ref, *, lam_init):
    sg = sg_ref[...].astype(F32)
    a = of_ref[...].astype(F32) + or_ref[...].astype(F32)
    ms = jnp.dot((a * a).astype(BF16), g128_ref[...], preferred_element_type=F32)
    mix_a = a * lax.rsqrt(ms + NORM_EPS) * hnw_ref[...] * sg[:, :HGRN_WIDTH]
    mix_b = ob_ref[...].astype(F32).T * sg[:, HGRN_WIDTH:HGRN_WIDTH + GQA_WIDTH]

    lp = lam_ref[...]
    lam = (jnp.exp(jnp.sum(lp[0:1] * lp[1:2], axis=-1, keepdims=True))
           - jnp.exp(jnp.sum(lp[2:3] * lp[3:4], axis=-1, keepdims=True)) + lam_init)
    c = (oc1_ref[...].astype(F32) - lam * oc2_ref[...].astype(F32)).T
    ms = jnp.dot((c * c).astype(BF16), g64_ref[...], preferred_element_type=F32)
    mix_c = (c * lax.rsqrt(ms + NORM_EPS) * dnw_ref[...] * (1.0 - lam_init)
             * sg[:, HGRN_WIDTH + GQA_WIDTH:])

    y = (jnp.dot(mix_a.astype(BF16), wo_ref[0:HGRN_WIDTH, :], preferred_element_type=F32)
         + jnp.dot(mix_b.astype(BF16), wo_ref[HGRN_WIDTH:HGRN_WIDTH + GQA_WIDTH, :],
                   preferred_element_type=F32)
         + jnp.dot(mix_c.astype(BF16), wo_ref[HGRN_WIDTH + GQA_WIDTH:, :],
                   preferred_element_type=F32))
    out_ref[...] = x_ref[...] + (y * lax.rsqrt(jnp.mean(y * y, axis=-1, keepdims=True) + NORM_EPS)
                                 * postw_ref[...])


def _outproj(x2, of2, or2, obt, oc1t, oc2t, sg, wo_bf, hnw, dnw, lam_p, postw, g128, g64,
             lam_init, tm):
    t = x2.shape[0]
    tiles_per_seq = obt.shape[2] // tm
    tok = lambda w: pl.BlockSpec((tm, w), lambda i: (i, 0))
    att = pl.BlockSpec((None, GQA_WIDTH, tm), lambda i: (i // tiles_per_seq, 0, i % tiles_per_seq))
    return pl.pallas_call(
        functools.partial(_outproj_kernel, lam_init=lam_init),
        grid=(t // tm,),
        in_specs=[tok(D_MODEL), tok(HGRN_WIDTH), tok(HGRN_WIDTH), att, att, att, tok(MIX_WIDTH),
                  _const_spec((MIX_WIDTH, D_MODEL)),
                  _const_spec((1, HGRN_WIDTH)), _const_spec((1, DIFF_WIDTH)),
                  _const_spec((4, DIFF_HEAD_DIM)), _const_spec((1, D_MODEL)),
                  _const_spec((HGRN_WIDTH, HGRN_WIDTH)), _const_spec((DIFF_WIDTH, DIFF_WIDTH))],
        out_specs=tok(D_MODEL),
        out_shape=jax.ShapeDtypeStruct((t, D_MODEL), F32),
        compiler_params=_params(("parallel",)),
        name="outproj",
    )(x2, of2, or2, obt, oc1t, oc2t, sg, wo_bf, hnw, dnw, lam_p, postw, g128, g64)


def _group_mean_matrix(width, group):
    idx = jnp.arange(width) // group
    return ((idx[:, None] == idx[None, :]).astype(F32) / group).astype(BF16)


def _rope_tables(seq_len):
    half = ROPE_DIM // 2
    inv = jnp.power(ROPE_THETA, -jnp.arange(0, ROPE_DIM, 2, dtype=F32) / ROPE_DIM)
    pos = jnp.arange(seq_len, dtype=F32)
    rows = seq_len // GRID_W
    row_pos = jnp.repeat(jnp.arange(rows, dtype=F32), GRID_W)
    col_pos = jnp.tile(jnp.arange(GRID_W, dtype=F32), rows)
    sign = jnp.concatenate([-jnp.ones((half,), F32), jnp.ones((half,), F32)])

    def tab(p):
        ang = p[:, None] * inv[None, :]
        ang = jnp.concatenate([ang, ang], axis=-1)
        return jnp.cos(ang), jnp.sin(ang) * sign

    cr, sr = tab(row_pos)
    cc, sc = tab(col_pos)
    c1, s1 = tab(pos)
    rep_b = GQA_WIDTH // (2 * ROPE_DIM)
    rep_c = DIFF_WIDTH // ROPE_DIM
    return (jnp.tile(jnp.concatenate([cr, cc], axis=-1), (1, rep_b)),
            jnp.tile(jnp.concatenate([sr, sc], axis=-1), (1, rep_b)),
            jnp.tile(c1, (1, rep_c)), jnp.tile(s1, (1, rep_c)))


def _chunk_tri(tb):
    r = jnp.arange(2 * tb)
    same = (r[:, None] // HGRN_CHUNK) == (r[None, :] // HGRN_CHUNK)
    tri = jnp.where(r[:, None] < tb, r[None, :] <= r[:, None], r[None, :] >= r[:, None])
    return (same & tri).astype(BF16)


def _trunk(x, pre_norm_w, w_in_bf, hgrn_lb, hgrn_norm_w, gqa_q_norm_w, gqa_k_norm_w,
           diff_lambda, diff_norm_w, w_out_bf, post_norm_w, *, tabs, g64, g128, tri,
           tm, tb, tq):
    bsz, seq, _ = x.shape
    t = bsz * seq
    x2 = x.reshape(t, D_MODEL)
    for layer in range(DEPTH):
        qnw = jnp.tile(gqa_q_norm_w[layer], GQA_HEADS)[None, :]
        knw = jnp.tile(gqa_k_norm_w[layer], GQA_KV_HEADS)[None, :]
        pa, qbt, kdup, vbt, qct, kc, vct, sg = _inproj(
            x2, bsz, seq, pre_norm_w[layer][None, :], w_in_bf[layer], tabs, qnw, knw, g64, tm)

        o_f, o_r = _hgrn(pa.reshape(bsz, seq, A_COLS), hgrn_lb, tri, layer, tb)

        (obt,) = _flash(qbt, kdup.reshape(GQA_KV_HEADS, bsz, seq, LANES), lambda b, j: (j, 0),
                        vbt, lambda j: (j,), GQA_MAPS, 1, tq, "flash_gqa")

        oc1t, oc2t = _flash(qct, kc.reshape(1, bsz, seq, -1), lambda b, j: (0, j),
                            vct, lambda j: (2 * j, 2 * j + 1), DIFF_MAPS, 2, tq, "flash_diff")

        lam_init = 0.8 - 0.6 * math.exp(-0.3 * layer)
        x2 = _outproj(x2, o_f.reshape(t, -1), o_r.reshape(t, -1), obt, oc1t, oc2t, sg,
                      w_out_bf[layer], hgrn_norm_w[layer][None, :],
                      jnp.tile(diff_norm_w[layer], DIFF_HEADS)[None, :],
                      diff_lambda[layer], post_norm_w[layer][None, :], g128, g64, lam_init, tm)
    return x2.reshape(bsz, seq, D_MODEL)


def kernel(x_prompt, x_sample, pre_norm_w, w_in, hgrn_lb, hgrn_norm_w, gqa_q_norm_w, gqa_k_norm_w,
           diff_lambda, diff_norm_w, w_out, post_norm_w):
    w_in_bf = w_in.astype(BF16)
    w_out_bf = w_out.astype(BF16)
    longest = max(x_prompt.shape[1], x_sample.shape[1])
    run = functools.partial(_trunk, pre_norm_w=pre_norm_w, w_in_bf=w_in_bf, hgrn_lb=hgrn_lb,
                            hgrn_norm_w=hgrn_norm_w, gqa_q_norm_w=gqa_q_norm_w,
                            gqa_k_norm_w=gqa_k_norm_w, diff_lambda=diff_lambda,
                            diff_norm_w=diff_norm_w, w_out_bf=w_out_bf, post_norm_w=post_norm_w,
                            tabs=_rope_tables(longest),
                            g64=_group_mean_matrix(GQA_WIDTH, GQA_HEAD_DIM),
                            g128=_group_mean_matrix(HGRN_WIDTH, HGRN_DIM),
                            tri=_chunk_tri(TILES["hgrn_block"]),
                            tm=TILES["proj_rows"], tb=TILES["hgrn_block"],
                            tq=TILES["attn_queries"])
    return (run(x_prompt), run(x_sample))
```

```python
import functools
import math

import jax
import jax.numpy as jnp
from jax import lax
from jax.experimental import pallas as pl
from jax.experimental.pallas import tpu as pltpu

F32 = jnp.float32
BF16 = jnp.bfloat16

D_MODEL = 1024
DEPTH = 2
GRID_W = 64
HGRN_HEADS = 4
HGRN_DIM = 128
HGRN_WIDTH = HGRN_HEADS * HGRN_DIM
HGRN_CHUNK = 32
HGRN_HEADS_PER_STEP = 4
GQA_HEADS = 4
GQA_KV_HEADS = 2
GQA_HEAD_DIM = 64
GQA_WIDTH = GQA_HEADS * GQA_HEAD_DIM
GQA_KV_WIDTH = GQA_KV_HEADS * GQA_HEAD_DIM
DIFF_HEADS = 4
DIFF_HEAD_DIM = 32
DIFF_WIDTH = DIFF_HEADS * 2 * DIFF_HEAD_DIM
MIX_WIDTH = HGRN_WIDTH + GQA_WIDTH + DIFF_WIDTH
ROPE_THETA = 10000.0
ROPE_DIM = 32
NORM_EPS = 1e-6

A_COLS = 4 * HGRN_WIDTH
OFF_AG = A_COLS
OFF_B = OFF_AG + HGRN_WIDTH
OFF_BG = OFF_B + GQA_WIDTH + 2 * GQA_KV_WIDTH
OFF_C = OFF_BG + GQA_WIDTH
IN_COLS = OFF_C + 4 * DIFF_WIDTH

LANES = 128
SUBLANES = 8
VMEM_LIMIT = 56 * 1024 * 1024
NEG_BIG = -1e30
LOG2E = math.log2(math.e)

VALUE_ROWS = 80
COL_BLOCK = 128
CHUNKS_PER_TRIP = 4
MAX_LAG_LOG2 = 100.0

TILES = {"proj_rows": 512, "hgrn_block": 256, "gqa_queries": 2048, "diff_queries": 1024}

NT_DIMS = (((1,), (1,)), ((), ()))
TN_DIMS = (((0,), (0,)), ((), ()))


def _params(sem):
    return pltpu.CompilerParams(dimension_semantics=sem, vmem_limit_bytes=VMEM_LIMIT)


def _const_spec(shape):
    nd = len(shape)
    return pl.BlockSpec(shape, lambda *_: (0,) * nd)


def _rot_half16(x):
    outs = []
    for s in range(x.shape[1] // LANES):
        xs = x[:, s * LANES:(s + 1) * LANES]
        up = pltpu.roll(xs, ROPE_DIM // 2, 1)
        dn = pltpu.roll(xs, LANES - ROPE_DIM // 2, 1)
        lane = lax.broadcasted_iota(jnp.int32, xs.shape, 1)
        outs.append(jnp.where((lane & (ROPE_DIM - 1)) < ROPE_DIM // 2, dn, up))
    return outs[0] if len(outs) == 1 else jnp.concatenate(outs, axis=1)


def _silu(x):
    return x * (1.0 / (1.0 + jnp.exp(-x)))


def _inproj_kernel(x_ref, pw_ref, w_ref, cosb_ref, sinb_ref, cosc_ref, sinc_ref,
                   qnw_ref, knw_ref, g64_ref,
                   pa_ref, qbt_ref, kd_ref, vbt_ref, qct_ref, kc_ref, vct_ref, sg_ref):
    x = x_ref[...]
    h = x * lax.rsqrt(jnp.mean(x * x, axis=-1, keepdims=True) + NORM_EPS) * pw_ref[...]
    hb = h.astype(BF16)

    def proj(lo, hi):
        return jnp.dot(hb, w_ref[:, lo:hi], preferred_element_type=F32)

    pa_ref[...] = proj(0, A_COLS)
    sg_ref[:, 0:HGRN_WIDTH] = _silu(proj(OFF_AG, OFF_B)).astype(BF16)

    def head_rms(t, w, g):
        ms = jnp.dot((t * t).astype(BF16), g, preferred_element_type=F32)
        return t * lax.rsqrt(ms + NORM_EPS) * w

    def rope(t, cos, sin):
        return t * cos + _rot_half16(t) * sin

    lane = lax.broadcasted_iota(jnp.int32, (x.shape[0], LANES), 1)
    low = lane < GQA_HEAD_DIM

    def store_values(v, vt_ref):
        for s in range(v.shape[1] // LANES):
            slab = v[:, s * LANES:(s + 1) * LANES]
            for half, src in enumerate((slab, pltpu.roll(slab, GQA_HEAD_DIM, 1))):
                aug = jnp.where(low, src, jnp.where(lane == GQA_HEAD_DIM, 1.0, 0.0))
                vt_ref[2 * s + half] = aug.T[0:VALUE_ROWS, :].astype(BF16)

    g64 = g64_ref[...]
    cosb = cosb_ref[...]
    sinb = sinb_ref[...]
    bq = head_rms(proj(OFF_B, OFF_B + GQA_WIDTH), qnw_ref[...], g64)
    qbt_ref[...] = (rope(bq, cosb, sinb) * (LOG2E / math.sqrt(GQA_HEAD_DIM))).T.astype(BF16)
    off_k = OFF_B + GQA_WIDTH
    bk = head_rms(proj(off_k, off_k + GQA_KV_WIDTH), knw_ref[...],
                  g64[:GQA_KV_WIDTH, :GQA_KV_WIDTH])
    k = rope(bk, cosb[:, :GQA_KV_WIDTH], sinb[:, :GQA_KV_WIDTH])
    k_swapped = pltpu.roll(k, GQA_HEAD_DIM, 1)
    kd_ref[0] = jnp.where(low, k, k_swapped).astype(BF16)
    kd_ref[1] = jnp.where(low, k_swapped, k).astype(BF16)
    off_v = off_k + GQA_KV_WIDTH
    store_values(proj(off_v, OFF_BG), vbt_ref)
    sg_ref[:, HGRN_WIDTH:HGRN_WIDTH + GQA_WIDTH] = _silu(proj(OFF_BG, OFF_C)).astype(BF16)

    cosc = cosc_ref[...]
    sinc = sinc_ref[...]
    cq = proj(OFF_C, OFF_C + DIFF_WIDTH)
    qct_ref[...] = (rope(cq, cosc, sinc) * (LOG2E / math.sqrt(DIFF_HEAD_DIM))).T.astype(BF16)
    ck = proj(OFF_C + DIFF_WIDTH, OFF_C + 2 * DIFF_WIDTH)
    kc_ref[...] = rope(ck, cosc, sinc).astype(BF16)
    store_values(proj(OFF_C + 2 * DIFF_WIDTH, OFF_C + 3 * DIFF_WIDTH), vct_ref)
    sg_ref[:, HGRN_WIDTH + GQA_WIDTH:] = _silu(proj(OFF_C + 3 * DIFF_WIDTH, IN_COLS)).astype(BF16)


def _inproj(x2, bsz, seq_len, pre_w, w_in_bf, tabs, qnw, knw, g64, tm):
    t = x2.shape[0]
    tiles_per_seq = seq_len // tm
    tok = lambda w: pl.BlockSpec((tm, w), lambda i: (i, 0))
    tab = pl.BlockSpec((tm, GQA_WIDTH), lambda i: (i % tiles_per_seq, 0))
    qt_spec = pl.BlockSpec((None, GQA_WIDTH, tm),
                           lambda i: (i // tiles_per_seq, 0, i % tiles_per_seq))
    qt_shape = jax.ShapeDtypeStruct((bsz, GQA_WIDTH, seq_len), BF16)

    def vt(heads):
        return (pl.BlockSpec((heads, None, None, VALUE_ROWS, tm),
                             lambda i: (0, i // tiles_per_seq, i % tiles_per_seq, 0, 0)),
                jax.ShapeDtypeStruct((heads, bsz, tiles_per_seq, VALUE_ROWS, tm), BF16))

    outs = [(tok(A_COLS), jax.ShapeDtypeStruct((t, A_COLS), F32)),
            (qt_spec, qt_shape),
            (pl.BlockSpec((GQA_KV_HEADS, tm, LANES), lambda i: (0, i, 0)),
             jax.ShapeDtypeStruct((GQA_KV_HEADS, t, LANES), BF16)),
            vt(GQA_KV_HEADS),
            (qt_spec, qt_shape),
            (tok(DIFF_WIDTH), jax.ShapeDtypeStruct((t, DIFF_WIDTH), BF16)),
            vt(DIFF_HEADS),
            (tok(MIX_WIDTH), jax.ShapeDtypeStruct((t, MIX_WIDTH), BF16))]
    return pl.pallas_call(
        _inproj_kernel,
        grid=(t // tm,),
        in_specs=[tok(D_MODEL), _const_spec((1, D_MODEL)), _const_spec((D_MODEL, IN_COLS)),
                  tab, tab, tab, tab,
                  _const_spec((1, GQA_WIDTH)), _const_spec((1, GQA_KV_WIDTH)),
                  _const_spec((GQA_WIDTH, GQA_WIDTH))],
        out_specs=[o[0] for o in outs],
        out_shape=[o[1] for o in outs],
        compiler_params=_params(("parallel",)),
        name="inproj",
    )(x2, pre_w, w_in_bf, *tabs, qnw, knw, g64)


def _log1p(x):
    return jnp.log(1.0 + x)


def _hgrn_prepare(q, xf, v, lb, tri):
    tb = q.shape[0] // 2
    nc = tb // HGRN_CHUNK

    def rows(t):
        return jnp.concatenate([jnp.broadcast_to(t[0:1], (tb, HGRN_DIM)),
                                jnp.broadcast_to(t[1:2], (tb, HGRN_DIM))], axis=0)

    e = jnp.exp(-jnp.abs(xf))
    log_sig = jnp.minimum(xf, 0.0) - jnp.log(1.0 + e)
    c = rows(_log1p(-lb)) + log_sig
    a = rows(jnp.log(lb))
    g = jnp.maximum(a, c) + _log1p(jnp.exp(-jnp.abs(a - c)))
    k = rows(1.0 - lb) * jnp.where(xf > 0.0, e, 1.0) * (1.0 / (1.0 + e))
    qs = _silu(q)

    g_hi = g.astype(BF16)
    g_lo = (g - g_hi.astype(F32)).astype(BF16)
    halves = (slice(0, tb), slice(tb, 2 * tb))
    g2 = jnp.concatenate([g_hi, g_lo], axis=1)
    b2 = jnp.concatenate([jnp.dot(tri[h, h], g2[h], preferred_element_type=F32)
                          for h in halves], axis=0)
    b = b2[:, :HGRN_DIM] + b2[:, HGRN_DIM:]

    mid = (HGRN_CHUNK // 2, HGRN_CHUNK // 2 - 1)
    last = (HGRN_CHUNK - 1, 0)
    chunk_rows = [slice(r0, r0 + HGRN_CHUNK) for r0 in range(0, 2 * tb, HGRN_CHUNK)]
    b_mid, from_start, to_end = [], [], []
    for ci, sl in enumerate(chunk_rows):
        d = ci // nc
        mid_row = b[sl.start + mid[d]:sl.start + mid[d] + 1, :]
        last_row = b[sl.start + last[d]:sl.start + last[d] + 1, :]
        b_mid.append(jnp.broadcast_to(mid_row, (HGRN_CHUNK, HGRN_DIM)))
        from_start.append(jnp.exp(mid_row))
        to_end.append(jnp.exp(last_row - mid_row))
    b_mid = jnp.concatenate(b_mid, axis=0)
    spread = lambda rows_: jnp.concatenate(
        [jnp.broadcast_to(r, (HGRN_CHUNK, HGRN_DIM)) for r in rows_], axis=0)

    q_up = qs * jnp.exp(b - b_mid)
    k_dn = k * jnp.exp(b_mid - b)
    qm = q_up.astype(BF16)
    km = k_dn.astype(BF16)
    kp = (k_dn * spread(to_end)).astype(BF16)
    qd = (q_up * spread(from_start)).astype(BF16)
    vb = v.astype(BF16)

    o_intra = []
    for h in halves:
        scores = lax.dot_general(qm[h], km[h], NT_DIMS, preferred_element_type=F32)
        scores = jnp.where(tri[h, h] > 0, scores, 0.0).astype(BF16)
        o_intra.append(jnp.dot(scores, vb[h], preferred_element_type=F32))
    o_intra = jnp.concatenate(o_intra, axis=0)

    chunks = []
    for ci, sl in enumerate(chunk_rows):
        dec = from_start[ci] * to_end[ci]
        inc = lax.dot_general(vb[sl], kp[sl], TN_DIMS, preferred_element_type=F32)
        chunks.append((qd[sl], dec, inc))
    return o_intra, chunks


def _hgrn_scan(prepared, state_refs):
    heads = len(prepared)
    nc = len(prepared[0][1]) // 2
    states = [[state_refs[d][h] for h in range(heads)] for d in range(2)]
    o_inter = [[None] * (2 * nc) for _ in range(heads)]
    for step in range(nc):
        for h in range(heads):
            for d in range(2):
                ci = d * nc + (step if d == 0 else nc - 1 - step)
                qd, dec, inc = prepared[h][1][ci]
                o_inter[h][ci] = jnp.dot(qd, states[d][h].T.astype(BF16),
                                         preferred_element_type=F32)
                states[d][h] = states[d][h] * dec + inc
    for d in range(2):
        for h in range(heads):
            state_refs[d][h] = states[d][h]
    return [prepared[h][0] + jnp.concatenate(o_inter[h], axis=0) for h in range(heads)]


def _hgrn_kernel(qf_ref, xf_ref, vf_ref, qr_ref, xr_ref, vr_ref, lbp_ref, tri_ref,
                 of_ref, or_ref, sf_ref, sr_ref, *, layer):
    @pl.when(pl.program_id(2) == 0)
    def _():
        sf_ref[...] = jnp.zeros_like(sf_ref)
        sr_ref[...] = jnp.zeros_like(sr_ref)

    rows = [lbp_ref[l] for l in range(DEPTH)]
    top = functools.reduce(jnp.maximum, rows)
    e = [jnp.exp(r - top) for r in rows]
    den = functools.reduce(lambda u, w: u + w, e)
    lb = jnp.zeros(rows[0].shape, F32)
    for l in range(1, layer + 1):
        lb = lb + e[l] / den

    tb = qf_ref.shape[0]
    heads = qf_ref.shape[1] // HGRN_DIM
    tri = tri_ref[...]
    prepared = []
    for h in range(heads):
        lanes = slice(h * HGRN_DIM, (h + 1) * HGRN_DIM)
        stack = lambda f_ref, r_ref: jnp.concatenate([f_ref[:, lanes], r_ref[:, lanes]], axis=0)
        prepared.append(_hgrn_prepare(stack(qf_ref, qr_ref), stack(xf_ref, xr_ref),
                                      stack(vf_ref, vr_ref), lb[:, lanes], tri))
    outs = _hgrn_scan(prepared, (sf_ref, sr_ref))
    for h, out in enumerate(outs):
        lanes = slice(h * HGRN_DIM, (h + 1) * HGRN_DIM)
        of_ref[:, lanes] = out[:tb].astype(BF16)
        or_ref[:, lanes] = out[tb:].astype(BF16)


def _hgrn(pa3, hgrn_lb, tri, layer, tb):
    bsz, seq, _ = pa3.shape
    nb = seq // tb
    groups = HGRN_HEADS // HGRN_HEADS_PER_STEP
    width = HGRN_HEADS_PER_STEP * HGRN_DIM

    def fwd(col):
        return pl.BlockSpec((None, tb, width), lambda b, h, i: (b, i, col * groups + h))

    def rev(col):
        return pl.BlockSpec((None, tb, width), lambda b, h, i: (b, nb - 1 - i, col * groups + h))

    out_f = pl.BlockSpec((None, tb, width), lambda b, h, i: (b, i, h))
    out_r = pl.BlockSpec((None, tb, width), lambda b, h, i: (b, nb - 1 - i, h))
    shape = jax.ShapeDtypeStruct((bsz, seq, HGRN_WIDTH), BF16)
    state = pltpu.VMEM((HGRN_HEADS_PER_STEP, HGRN_DIM, HGRN_DIM), F32)
    return pl.pallas_call(
        functools.partial(_hgrn_kernel, layer=layer),
        grid=(bsz, groups, nb),
        in_specs=[fwd(0), fwd(1), fwd(3), rev(0), rev(2), rev(3),
                  pl.BlockSpec((DEPTH, 2, width), lambda b, h, i: (0, 0, h)),
                  _const_spec((2 * tb, 2 * tb))],
        out_specs=[out_f, out_r],
        out_shape=[shape, shape],
        scratch_shapes=[state, state],
        compiler_params=_params(("parallel", "parallel", "arbitrary")),
        name="hgrn2",
    )(pa3, pa3, pa3, pa3, pa3, pa3, hgrn_lb, tri)


def _flash_kernel(*refs, maps, n_v, n_out, tk):
    n_maps = len(maps)
    qt_ref, k_ref = refs[0], refs[1]
    v_refs = refs[2:2 + n_v]
    out_refs = refs[2 + n_v:2 + n_v + n_out]
    scratch = refs[2 + n_v + n_out:]
    acc_ref, m_ref, gap_ref, qv_ref = scratch[:4]
    per_kind = 2 * n_maps
    s_ref, p_ref, al_ref = (
        [scratch[4 + kind * per_kind + slot * n_maps:4 + kind * per_kind + (slot + 1) * n_maps]
         for slot in range(2)] for kind in range(3))
    tq = qt_ref.shape[1]
    n_chunks = k_ref.shape[0] // tk
    sub = m_ref.shape[1]

    def scores(ci, m):
        off = pl.multiple_of(ci * tk, tk)
        return jnp.dot(k_ref[pl.ds(off, tk), :], qv_ref[m], preferred_element_type=F32)

    def column_max(s):
        top = jnp.max(s.reshape(s.shape[0] // sub, sub, tq), axis=0)
        return jnp.broadcast_to(jnp.max(top, axis=0, keepdims=True), (sub, tq))

    def lagged_softmax(ci, slot):
        for m in range(n_maps):
            shift = m_ref[m]
            s = scores(ci, m)
            p_ref[slot][m][...] = jnp.exp2(s - shift[0:1, :]).astype(BF16)
            top = column_max(s)
            gap_ref[m] = jnp.maximum(gap_ref[m], top - shift)
            m_next = jnp.maximum(shift, top)
            m_ref[m] = m_next
            al_ref[slot][m][...] = jnp.exp2(shift - m_next)

    def lagged_pv(ci, slot):
        for m, (_, _, vi, _, _) in enumerate(maps):
            acc_ref[m] = (acc_ref[m] + jnp.dot(v_refs[vi][ci], p_ref[slot][m][...],
                                               preferred_element_type=F32)
                          ) * al_ref[slot][m][0:1, :]

    def lagged_trip(t, carry):
        first = t * CHUNKS_PER_TRIP
        lagged_softmax(first, 0)
        for u in range(1, CHUNKS_PER_TRIP):
            lagged_softmax(first + u, u % 2)
            lagged_pv(first + u - 1, (u - 1) % 2)
        lagged_pv(first + CHUNKS_PER_TRIP - 1, (CHUNKS_PER_TRIP - 1) % 2)
        return carry

    def exact_chunk(ci, slot):
        for m in range(n_maps):
            s_ref[slot][m][...] = scores(ci, m)
        for m, (_, _, vi, _, _) in enumerate(maps):
            m_prev = m_ref[m]
            m_next = jnp.maximum(m_prev, column_max(s_ref[slot][m][...]))
            m_ref[m] = m_next
            for c0 in range(0, tq, COL_BLOCK):
                cols = slice(c0, c0 + COL_BLOCK)
                p_ref[slot][m][:, cols] = jnp.exp2(
                    s_ref[slot][m][:, cols] - m_next[0:1, cols]).astype(BF16)
            acc_ref[m] = acc_ref[m] * jnp.exp2(m_prev - m_next)[0:1, :] + jnp.dot(
                v_refs[vi][ci], p_ref[slot][m][...], preferred_element_type=F32)

    def exact_trip(t, carry):
        for u in range(CHUNKS_PER_TRIP):
            exact_chunk(t * CHUNKS_PER_TRIP + u, u % 2)
        return carry

    qt = qt_ref[...]
    row = lax.broadcasted_iota(jnp.int32, qt.shape, 0)
    for m, (lo, hi, _, _, _) in enumerate(maps):
        qv_ref[m] = jnp.where((row >= lo) & (row < hi), qt, jnp.zeros_like(qt))

    acc_ref[...] = jnp.zeros_like(acc_ref)
    gap_ref[...] = jnp.zeros_like(gap_ref)
    for m in range(n_maps):
        m_ref[m] = column_max(jnp.dot(k_ref[0:LANES, :], qv_ref[m], preferred_element_type=F32))
    lax.fori_loop(0, n_chunks // CHUNKS_PER_TRIP, lagged_trip, 0)

    @pl.when(jnp.logical_not(jnp.max(gap_ref[...]) <= MAX_LAG_LOG2))
    def _():
        acc_ref[...] = jnp.zeros_like(acc_ref)
        m_ref[...] = jnp.full_like(m_ref, NEG_BIG)
        lax.fori_loop(0, n_chunks // CHUNKS_PER_TRIP, exact_trip, 0)

    for oi, out_ref in enumerate(out_refs):
        for m, (_, _, _, mo, half) in enumerate(maps):
            if mo == oi:
                acc = acc_ref[m]
                out_ref[half * GQA_HEAD_DIM:(half + 1) * GQA_HEAD_DIM, :] = (
                    acc[0:GQA_HEAD_DIM, :] * (1.0 / acc[GQA_HEAD_DIM:GQA_HEAD_DIM + 1, :])
                ).astype(BF16)


def _flash(qt3, k4, k_index, vt5, v_indices, maps, n_out, tq, name):
    bsz, qw, seq = qt3.shape
    tk = vt5.shape[-1]
    assert seq % tq == 0 and (seq // tk) % CHUNKS_PER_TRIP == 0, (seq, tq, tk)
    nqb = qw // LANES
    n_maps = len(maps)
    n_v = len(v_indices(0))
    per_kind = 2 * n_maps
    once = pl.Buffered(1)

    def kmap(b, j, qi):
        n, lb = k_index(b, j)
        return (n, b, 0, lb)

    kspec = pl.BlockSpec((None, None, seq, LANES), kmap, pipeline_mode=once)
    vspecs = [pl.BlockSpec((None, None, seq // tk, VALUE_ROWS, tk),
                           lambda b, j, qi, n=n: (v_indices(j)[n], b, 0, 0, 0), pipeline_mode=once)
              for n in range(n_v)]
    qspec = pl.BlockSpec((None, LANES, tq), lambda b, j, qi: (b, j, qi))
    return pl.pallas_call(
        functools.partial(_flash_kernel, maps=maps, n_v=n_v, n_out=n_out, tk=tk),
        grid=(bsz, nqb, seq // tq),
        in_specs=[qspec, kspec] + vspecs,
        out_specs=[qspec] * n_out,
        out_shape=[jax.ShapeDtypeStruct((bsz, qw, seq), BF16)] * n_out,
        scratch_shapes=([pltpu.VMEM((n_maps, VALUE_ROWS, tq), F32),
                         pltpu.VMEM((n_maps, SUBLANES, tq), F32),
                         pltpu.VMEM((n_maps, SUBLANES, tq), F32),
                         pltpu.VMEM((n_maps, LANES, tq), BF16)]
                        + [pltpu.VMEM((tk, tq), F32)] * per_kind
                        + [pltpu.VMEM((tk, tq), BF16)] * per_kind
                        + [pltpu.VMEM((SUBLANES, tq), F32)] * per_kind),
        compiler_params=_params(("parallel", "parallel", "parallel")),
        name=name,
    )(qt3, k4, *([vt5] * n_v))


GQA_MAPS = ((0, GQA_HEAD_DIM, 0, 0, 0), (GQA_HEAD_DIM, LANES, 0, 0, 1))
DIFF_MAPS = tuple((m * DIFF_HEAD_DIM, (m + 1) * DIFF_HEAD_DIM, m // 2, m % 2, m // 2)
                  for m in range(4))


def _outproj_kernel(x_ref, of_ref, or_ref, ob_ref, oc1_ref, oc2_ref, sg_ref, wo_ref,
                    hnw_ref, dnw_ref, lam_ref, postw_ref, g128_ref, g64_ref, out_ref, *, lam_init):
    sg = sg_ref[...].astype(F32)
    a = of_ref[...].astype(F32) + or_ref[...].astype(F32)
    ms = jnp.dot((a * a).astype(BF16), g128_ref[...], preferred_element_type=F32)
    mix_a = a * lax.rsqrt(ms + NORM_EPS) * hnw_ref[...] * sg[:, :HGRN_WIDTH]
    mix_b = ob_ref[...].astype(F32).T * sg[:, HGRN_WIDTH:HGRN_WIDTH + GQA_WIDTH]

    lp = lam_ref[...]
    lam = (jnp.exp(jnp.sum(lp[0:1] * lp[1:2], axis=-1, keepdims=True))
           - jnp.exp(jnp.sum(lp[2:3] * lp[3:4], axis=-1, keepdims=True)) + lam_init)
    c = (oc1_ref[...].astype(F32) - lam * oc2_ref[...].astype(F32)).T
    ms = jnp.dot((c * c).astype(BF16), g64_ref[...], preferred_element_type=F32)
    mix_c = (c * lax.rsqrt(ms + NORM_EPS) * dnw_ref[...] * (1.0 - lam_init)
             * sg[:, HGRN_WIDTH + GQA_WIDTH:])

    y = (jnp.dot(mix_a.astype(BF16), wo_ref[0:HGRN_WIDTH, :], preferred_element_type=F32)
         + jnp.dot(mix_b.astype(BF16), wo_ref[HGRN_WIDTH:HGRN_WIDTH + GQA_WIDTH, :],
                   preferred_element_type=F32)
         + jnp.dot(mix_c.astype(BF16), wo_ref[HGRN_WIDTH + GQA_WIDTH:, :],
                   preferred_element_type=F32))
    out_ref[...] = x_ref[...] + (y * lax.rsqrt(jnp.mean(y * y, axis=-1, keepdims=True) + NORM_EPS)
                                 * postw_ref[...])


def _outproj(x2, of2, or2, obt, oc1t, oc2t, sg, wo_bf, hnw, dnw, lam_p, postw, g128, g64,
             lam_init, tm):
    t = x2.shape[0]
    tiles_per_seq = obt.shape[2] // tm
    tok = lambda w: pl.BlockSpec((tm, w), lambda i: (i, 0))
    att = pl.BlockSpec((None, GQA_WIDTH, tm), lambda i: (i // tiles_per_seq, 0, i % tiles_per_seq))
    return pl.pallas_call(
        functools.partial(_outproj_kernel, lam_init=lam_init),
        grid=(t // tm,),
        in_specs=[tok(D_MODEL), tok(HGRN_WIDTH), tok(HGRN_WIDTH), att, att, att, tok(MIX_WIDTH),
                  _const_spec((MIX_WIDTH, D_MODEL)),
                  _const_spec((1, HGRN_WIDTH)), _const_spec((1, DIFF_WIDTH)),
                  _const_spec((4, DIFF_HEAD_DIM)), _const_spec((1, D_MODEL)),
                  _const_spec((HGRN_WIDTH, HGRN_WIDTH)), _const_spec((DIFF_WIDTH, DIFF_WIDTH))],
        out_specs=tok(D_MODEL),
        out_shape=jax.ShapeDtypeStruct((t, D_MODEL), F32),
        compiler_params=_params(("parallel",)),
        name="outproj",
    )(x2, of2, or2, obt, oc1t, oc2t, sg, wo_bf, hnw, dnw, lam_p, postw, g128, g64)


def _group_mean_matrix(width, group):
    idx = jnp.arange(width) // group
    return ((idx[:, None] == idx[None, :]).astype(F32) / group).astype(BF16)


def _rope_tables(seq_len):
    half = ROPE_DIM // 2
    inv = jnp.power(ROPE_THETA, -jnp.arange(0, ROPE_DIM, 2, dtype=F32) / ROPE_DIM)
    pos = jnp.arange(seq_len, dtype=F32)
    rows = seq_len // GRID_W
    row_pos = jnp.repeat(jnp.arange(rows, dtype=F32), GRID_W)
    col_pos = jnp.tile(jnp.arange(GRID_W, dtype=F32), rows)
    sign = jnp.concatenate([-jnp.ones((half,), F32), jnp.ones((half,), F32)])

    def tab(p):
        ang = p[:, None] * inv[None, :]
        ang = jnp.concatenate([ang, ang], axis=-1)
        return jnp.cos(ang), jnp.sin(ang) * sign

    cr, sr = tab(row_pos)
    cc, sc = tab(col_pos)
    c1, s1 = tab(pos)
    rep_b = GQA_WIDTH // (2 * ROPE_DIM)
    rep_c = DIFF_WIDTH // ROPE_DIM
    return (jnp.tile(jnp.concatenate([cr, cc], axis=-1), (1, rep_b)),
            jnp.tile(jnp.concatenate([sr, sc], axis=-1), (1, rep_b)),
            jnp.tile(c1, (1, rep_c)), jnp.tile(s1, (1, rep_c)))


def _chunk_tri(tb):
    r = jnp.arange(2 * tb)
    same = (r[:, None] // HGRN_CHUNK) == (r[None, :] // HGRN_CHUNK)
    tri = jnp.where(r[:, None] < tb, r[None, :] <= r[:, None], r[None, :] >= r[:, None])
    return (same & tri).astype(BF16)


def _trunk(x, pre_norm_w, w_in_bf, hgrn_lb, hgrn_norm_w, gqa_q_norm_w, gqa_k_norm_w,
           diff_lambda, diff_norm_w, w_out_bf, post_norm_w, *, tabs, g64, g128, tri,
           tm, tb, tq_gqa, tq_diff):
    bsz, seq, _ = x.shape
    t = bsz * seq
    x2 = x.reshape(t, D_MODEL)
    for layer in range(DEPTH):
        qnw = jnp.tile(gqa_q_norm_w[layer], GQA_HEADS)[None, :]
        knw = jnp.tile(gqa_k_norm_w[layer], GQA_KV_HEADS)[None, :]
        pa, qbt, kdup, vbt, qct, kc, vct, sg = _inproj(
            x2, bsz, seq, pre_norm_w[layer][None, :], w_in_bf[layer], tabs, qnw, knw, g64, tm)

        o_f, o_r = _hgrn(pa.reshape(bsz, seq, A_COLS), hgrn_lb, tri, layer, tb)

        (obt,) = _flash(qbt, kdup.reshape(GQA_KV_HEADS, bsz, seq, LANES), lambda b, j: (j, 0),
                        vbt, lambda j: (j,), GQA_MAPS, 1, tq_gqa, "flash_gqa")

        oc1t, oc2t = _flash(qct, kc.reshape(1, bsz, seq, -1), lambda b, j: (0, j),
                            vct, lambda j: (2 * j, 2 * j + 1), DIFF_MAPS, 2, tq_diff, "flash_diff")

        lam_init = 0.8 - 0.6 * math.exp(-0.3 * layer)
        x2 = _outproj(x2, o_f.reshape(t, -1), o_r.reshape(t, -1), obt, oc1t, oc2t, sg,
                      w_out_bf[layer], hgrn_norm_w[layer][None, :],
                      jnp.tile(diff_norm_w[layer], DIFF_HEADS)[None, :],
                      diff_lambda[layer], post_norm_w[layer][None, :], g128, g64, lam_init, tm)
    return x2.reshape(bsz, seq, D_MODEL)


def kernel(x_prompt, x_sample, pre_norm_w, w_in, hgrn_lb, hgrn_norm_w, gqa_q_norm_w, gqa_k_norm_w,
           diff_lambda, diff_norm_w, w_out, post_norm_w):
    w_in_bf = w_in.astype(BF16)
    w_out_bf = w_out.astype(BF16)
    longest = max(x_prompt.shape[1], x_sample.shape[1])
    run = functools.partial(_trunk, pre_norm_w=pre_norm_w, w_in_bf=w_in_bf, hgrn_lb=hgrn_lb,
                            hgrn_norm_w=hgrn_norm_w, gqa_q_norm_w=gqa_q_norm_w,
                            gqa_k_norm_w=gqa_k_norm_w, diff_lambda=diff_lambda,
                            diff_norm_w=diff_norm_w, w_out_bf=w_out_bf, post_norm_w=post_norm_w,
                            tabs=_rope_tables(longest),
                            g64=_group_mean_matrix(GQA_WIDTH, GQA_HEAD_DIM),
                            g128=_group_mean_matrix(HGRN_WIDTH, HGRN_DIM),
                            tri=_chunk_tri(TILES["hgrn_block"]),
                            tm=TILES["proj_rows"], tb=TILES["hgrn_block"],
                            tq_gqa=TILES["gqa_queries"], tq_diff=TILES["diff_queries"])
    return (run(x_prompt), run(x_sample))
```

```python
import functools
import math

import jax
import jax.numpy as jnp
from jax import lax
from jax.experimental import pallas as pl
from jax.experimental.pallas import tpu as pltpu

F32 = jnp.float32
BF16 = jnp.bfloat16

D_MODEL = 1024
DEPTH = 2
GRID_W = 64
HGRN_HEADS = 4
HGRN_DIM = 128
HGRN_WIDTH = HGRN_HEADS * HGRN_DIM
HGRN_CHUNK = 32
HGRN_HEADS_PER_STEP = 4
GQA_HEADS = 4
GQA_KV_HEADS = 2
GQA_HEAD_DIM = 64
GQA_WIDTH = GQA_HEADS * GQA_HEAD_DIM
GQA_KV_WIDTH = GQA_KV_HEADS * GQA_HEAD_DIM
DIFF_HEADS = 4
DIFF_HEAD_DIM = 32
DIFF_WIDTH = DIFF_HEADS * 2 * DIFF_HEAD_DIM
MIX_WIDTH = HGRN_WIDTH + GQA_WIDTH + DIFF_WIDTH
ROPE_THETA = 10000.0
ROPE_DIM = 32
NORM_EPS = 1e-6

A_COLS = 4 * HGRN_WIDTH
OFF_AG = A_COLS
OFF_B = OFF_AG + HGRN_WIDTH
OFF_BG = OFF_B + GQA_WIDTH + 2 * GQA_KV_WIDTH
OFF_C = OFF_BG + GQA_WIDTH
IN_COLS = OFF_C + 4 * DIFF_WIDTH

LANES = 128
SUBLANES = 8
VMEM_LIMIT = 56 * 1024 * 1024
NEG_BIG = -1e30
LOG2E = math.log2(math.e)

VALUE_ROWS = 80
COL_BLOCK = 128
CHUNKS_PER_TRIP = 4
MAX_LAG_LOG2 = 100.0

TILES = {"proj_rows": 512, "hgrn_block": 256, "gqa_queries": 2048, "diff_queries": 1024}

NT_DIMS = (((1,), (1,)), ((), ()))
TN_DIMS = (((0,), (0,)), ((), ()))


def _params(sem):
    return pltpu.CompilerParams(dimension_semantics=sem, vmem_limit_bytes=VMEM_LIMIT)


def _const_spec(shape):
    nd = len(shape)
    return pl.BlockSpec(shape, lambda *_: (0,) * nd)


def _rot_half16(x):
    outs = []
    for s in range(x.shape[1] // LANES):
        xs = x[:, s * LANES:(s + 1) * LANES]
        up = pltpu.roll(xs, ROPE_DIM // 2, 1)
        dn = pltpu.roll(xs, LANES - ROPE_DIM // 2, 1)
        lane = lax.broadcasted_iota(jnp.int32, xs.shape, 1)
        outs.append(jnp.where((lane & (ROPE_DIM - 1)) < ROPE_DIM // 2, dn, up))
    return outs[0] if len(outs) == 1 else jnp.concatenate(outs, axis=1)


def _silu(x):
    return x * (1.0 / (1.0 + jnp.exp(-x)))


def _inproj_kernel(x_ref, pw_ref, w_ref, cosb_ref, sinb_ref, cosc_ref, sinc_ref,
                   qnw_ref, knw_ref, g64_ref,
                   pa_ref, qbt_ref, kd_ref, vbt_ref, qct_ref, kc_ref, vct_ref, sg_ref):
    x = x_ref[...]
    h = x * lax.rsqrt(jnp.mean(x * x, axis=-1, keepdims=True) + NORM_EPS) * pw_ref[...]
    hb = h.astype(BF16)

    def proj(lo, hi):
        return jnp.dot(hb, w_ref[:, lo:hi], preferred_element_type=F32)

    pa_ref[...] = proj(0, A_COLS)
    sg_ref[:, 0:HGRN_WIDTH] = _silu(proj(OFF_AG, OFF_B)).astype(BF16)

    def head_rms(t, w, g):
        ms = jnp.dot((t * t).astype(BF16), g, preferred_element_type=F32)
        return t * lax.rsqrt(ms + NORM_EPS) * w

    def rope(t, cos, sin):
        return t * cos + _rot_half16(t) * sin

    lane = lax.broadcasted_iota(jnp.int32, (x.shape[0], LANES), 1)
    low = lane < GQA_HEAD_DIM

    def store_values(v, vt_ref):
        for s in range(v.shape[1] // LANES):
            slab = v[:, s * LANES:(s + 1) * LANES]
            for half, src in enumerate((slab, pltpu.roll(slab, GQA_HEAD_DIM, 1))):
                aug = jnp.where(low, src, jnp.where(lane == GQA_HEAD_DIM, 1.0, 0.0))
                vt_ref[2 * s + half] = aug.T[0:VALUE_ROWS, :].astype(BF16)

    g64 = g64_ref[...]
    cosb = cosb_ref[...]
    sinb = sinb_ref[...]
    bq = head_rms(proj(OFF_B, OFF_B + GQA_WIDTH), qnw_ref[...], g64)
    qbt_ref[...] = (rope(bq, cosb, sinb) * (LOG2E / math.sqrt(GQA_HEAD_DIM))).T.astype(BF16)
    off_k = OFF_B + GQA_WIDTH
    bk = head_rms(proj(off_k, off_k + GQA_KV_WIDTH), knw_ref[...],
                  g64[:GQA_KV_WIDTH, :GQA_KV_WIDTH])
    k = rope(bk, cosb[:, :GQA_KV_WIDTH], sinb[:, :GQA_KV_WIDTH])
    k_swapped = pltpu.roll(k, GQA_HEAD_DIM, 1)
    kd_ref[0] = jnp.where(low, k, k_swapped).astype(BF16)
    kd_ref[1] = jnp.where(low, k_swapped, k).astype(BF16)
    off_v = off_k + GQA_KV_WIDTH
    store_values(proj(off_v, OFF_BG), vbt_ref)
    sg_ref[:, HGRN_WIDTH:HGRN_WIDTH + GQA_WIDTH] = _silu(proj(OFF_BG, OFF_C)).astype(BF16)

    cosc = cosc_ref[...]
    sinc = sinc_ref[...]
    cq = proj(OFF_C, OFF_C + DIFF_WIDTH)
    qct_ref[...] = (rope(cq, cosc, sinc) * (LOG2E / math.sqrt(DIFF_HEAD_DIM))).T.astype(BF16)
    ck = proj(OFF_C + DIFF_WIDTH, OFF_C + 2 * DIFF_WIDTH)
    kc_ref[...] = rope(ck, cosc, sinc).astype(BF16)
    store_values(proj(OFF_C + 2 * DIFF_WIDTH, OFF_C + 3 * DIFF_WIDTH), vct_ref)
    sg_ref[:, HGRN_WIDTH + GQA_WIDTH:] = _silu(proj(OFF_C + 3 * DIFF_WIDTH, IN_COLS)).astype(BF16)


def _inproj(x2, bsz, seq_len, pre_w, w_in_bf, tabs, qnw, knw, g64, tm):
    t = x2.shape[0]
    tiles_per_seq = seq_len // tm
    tok = lambda w: pl.BlockSpec((tm, w), lambda i: (i, 0))
    tab = pl.BlockSpec((tm, GQA_WIDTH), lambda i: (i % tiles_per_seq, 0))
    qt_spec = pl.BlockSpec((None, GQA_WIDTH, tm),
                           lambda i: (i // tiles_per_seq, 0, i % tiles_per_seq))
    qt_shape = jax.ShapeDtypeStruct((bsz, GQA_WIDTH, seq_len), BF16)

    def vt(heads):
        return (pl.BlockSpec((heads, None, None, VALUE_ROWS, tm),
                             lambda i: (0, i // tiles_per_seq, i % tiles_per_seq, 0, 0)),
                jax.ShapeDtypeStruct((heads, bsz, tiles_per_seq, VALUE_ROWS, tm), BF16))

    outs = [(tok(A_COLS), jax.ShapeDtypeStruct((t, A_COLS), F32)),
            (qt_spec, qt_shape),
            (pl.BlockSpec((GQA_KV_HEADS, tm, LANES), lambda i: (0, i, 0)),
             jax.ShapeDtypeStruct((GQA_KV_HEADS, t, LANES), BF16)),
            vt(GQA_KV_HEADS),
            (qt_spec, qt_shape),
            (tok(DIFF_WIDTH), jax.ShapeDtypeStruct((t, DIFF_WIDTH), BF16)),
            vt(DIFF_HEADS),
            (tok(MIX_WIDTH), jax.ShapeDtypeStruct((t, MIX_WIDTH), BF16))]
    return pl.pallas_call(
        _inproj_kernel,
        grid=(t // tm,),
        in_specs=[tok(D_MODEL), _const_spec((1, D_MODEL)), _const_spec((D_MODEL, IN_COLS)),
                  tab, tab, tab, tab,
                  _const_spec((1, GQA_WIDTH)), _const_spec((1, GQA_KV_WIDTH)),
                  _const_spec((GQA_WIDTH, GQA_WIDTH))],
        out_specs=[o[0] for o in outs],
        out_shape=[o[1] for o in outs],
        compiler_params=_params(("parallel",)),
        name="inproj",
    )(x2, pre_w, w_in_bf, *tabs, qnw, knw, g64)


def _log1p(x):
    return jnp.log(1.0 + x)


def _hgrn_prepare(q, xf, v, lb, tri):
    tb = q.shape[0] // 2
    nc = tb // HGRN_CHUNK

    def rows(t):
        return jnp.concatenate([jnp.broadcast_to(t[0:1], (tb, HGRN_DIM)),
                                jnp.broadcast_to(t[1:2], (tb, HGRN_DIM))], axis=0)

    e = jnp.exp(-jnp.abs(xf))
    log_sig = jnp.minimum(xf, 0.0) - jnp.log(1.0 + e)
    c = rows(_log1p(-lb)) + log_sig
    a = rows(jnp.log(lb))
    g = jnp.maximum(a, c) + _log1p(jnp.exp(-jnp.abs(a - c)))
    k = rows(1.0 - lb) * jnp.where(xf > 0.0, e, 1.0) * (1.0 / (1.0 + e))
    qs = _silu(q)

    g_hi = g.astype(BF16)
    g_lo = (g - g_hi.astype(F32)).astype(BF16)
    halves = (slice(0, tb), slice(tb, 2 * tb))
    g2 = jnp.concatenate([g_hi, g_lo], axis=1)
    b2 = jnp.concatenate([jnp.dot(tri[h, h], g2[h], preferred_element_type=F32)
                          for h in halves], axis=0)
    b = b2[:, :HGRN_DIM] + b2[:, HGRN_DIM:]

    mid = (HGRN_CHUNK // 2, HGRN_CHUNK // 2 - 1)
    last = (HGRN_CHUNK - 1, 0)
    chunk_rows = [slice(r0, r0 + HGRN_CHUNK) for r0 in range(0, 2 * tb, HGRN_CHUNK)]
    b_mid, from_start, to_end = [], [], []
    for ci, sl in enumerate(chunk_rows):
        d = ci // nc
        mid_row = b[sl.start + mid[d]:sl.start + mid[d] + 1, :]
        last_row = b[sl.start + last[d]:sl.start + last[d] + 1, :]
        b_mid.append(jnp.broadcast_to(mid_row, (HGRN_CHUNK, HGRN_DIM)))
        from_start.append(jnp.exp(mid_row))
        to_end.append(jnp.exp(last_row - mid_row))
    b_mid = jnp.concatenate(b_mid, axis=0)
    spread = lambda rows_: jnp.concatenate(
        [jnp.broadcast_to(r, (HGRN_CHUNK, HGRN_DIM)) for r in rows_], axis=0)

    q_up = qs * jnp.exp(b - b_mid)
    k_dn = k * jnp.exp(b_mid - b)
    qm = q_up.astype(BF16)
    km = k_dn.astype(BF16)
    kp = (k_dn * spread(to_end)).astype(BF16)
    qd = (q_up * spread(from_start)).astype(BF16)
    vb = v.astype(BF16)

    o_intra = []
    for h in halves:
        scores = lax.dot_general(qm[h], km[h], NT_DIMS, preferred_element_type=F32)
        scores = jnp.where(tri[h, h] > 0, scores, 0.0).astype(BF16)
        o_intra.append(jnp.dot(scores, vb[h], preferred_element_type=F32))
    o_intra = jnp.concatenate(o_intra, axis=0)

    chunks = []
    for ci, sl in enumerate(chunk_rows):
        dec = from_start[ci] * to_end[ci]
        inc = lax.dot_general(vb[sl], kp[sl], TN_DIMS, preferred_element_type=F32)
        chunks.append((qd[sl], dec, inc))
    return o_intra, chunks


def _hgrn_scan(prepared, state_refs):
    heads = len(prepared)
    nc = len(prepared[0][1]) // 2
    states = [[state_refs[d][h] for h in range(heads)] for d in range(2)]
    o_inter = [[None] * (2 * nc) for _ in range(heads)]
    for step in range(nc):
        for h in range(heads):
            for d in range(2):
                ci = d * nc + (step if d == 0 else nc - 1 - step)
                qd, dec, inc = prepared[h][1][ci]
                o_inter[h][ci] = jnp.dot(qd, states[d][h].T.astype(BF16),
                                         preferred_element_type=F32)
                states[d][h] = states[d][h] * dec + inc
    for d in range(2):
        for h in range(heads):
            state_refs[d][h] = states[d][h]
    return [prepared[h][0] + jnp.concatenate(o_inter[h], axis=0) for h in range(heads)]


def _hgrn_kernel(qf_ref, xf_ref, vf_ref, qr_ref, xr_ref, vr_ref, lbp_ref, tri_ref,
                 of_ref, or_ref, sf_ref, sr_ref, *, layer):
    @pl.when(pl.program_id(2) == 0)
    def _():
        sf_ref[...] = jnp.zeros_like(sf_ref)
        sr_ref[...] = jnp.zeros_like(sr_ref)

    rows = [lbp_ref[l] for l in range(DEPTH)]
    top = functools.reduce(jnp.maximum, rows)
    e = [jnp.exp(r - top) for r in rows]
    den = functools.reduce(lambda u, w: u + w, e)
    lb = jnp.zeros(rows[0].shape, F32)
    for l in range(1, layer + 1):
        lb = lb + e[l] / den

    tb = qf_ref.shape[0]
    heads = qf_ref.shape[1] // HGRN_DIM
    tri = tri_ref[...]
    prepared = []
    for h in range(heads):
        lanes = slice(h * HGRN_DIM, (h + 1) * HGRN_DIM)
        stack = lambda f_ref, r_ref: jnp.concatenate([f_ref[:, lanes], r_ref[:, lanes]], axis=0)
        prepared.append(_hgrn_prepare(stack(qf_ref, qr_ref), stack(xf_ref, xr_ref),
                                      stack(vf_ref, vr_ref), lb[:, lanes], tri))
    outs = _hgrn_scan(prepared, (sf_ref, sr_ref))
    for h, out in enumerate(outs):
        lanes = slice(h * HGRN_DIM, (h + 1) * HGRN_DIM)
        of_ref[:, lanes] = out[:tb].astype(BF16)
        or_ref[:, lanes] = out[tb:].astype(BF16)


def _hgrn(pa3, hgrn_lb, tri, layer, tb):
    bsz, seq, _ = pa3.shape
    nb = seq // tb
    groups = HGRN_HEADS // HGRN_HEADS_PER_STEP
    width = HGRN_HEADS_PER_STEP * HGRN_DIM

    def fwd(col):
        return pl.BlockSpec((None, tb, width), lambda b, h, i: (b, i, col * groups + h))

    def rev(col):
        return pl.BlockSpec((None, tb, width), lambda b, h, i: (b, nb - 1 - i, col * groups + h))

    out_f = pl.BlockSpec((None, tb, width), lambda b, h, i: (b, i, h))
    out_r = pl.BlockSpec((None, tb, width), lambda b, h, i: (b, nb - 1 - i, h))
    shape = jax.ShapeDtypeStruct((bsz, seq, HGRN_WIDTH), BF16)
    state = pltpu.VMEM((HGRN_HEADS_PER_STEP, HGRN_DIM, HGRN_DIM), F32)
    return pl.pallas_call(
        functools.partial(_hgrn_kernel, layer=layer),
        grid=(bsz, groups, nb),
        in_specs=[fwd(0), fwd(1), fwd(3), rev(0), rev(2), rev(3),
                  pl.BlockSpec((DEPTH, 2, width), lambda b, h, i: (0, 0, h)),
                  _const_spec((2 * tb, 2 * tb))],
        out_specs=[out_f, out_r],
        out_shape=[shape, shape],
        scratch_shapes=[state, state],
        compiler_params=_params(("parallel", "parallel", "arbitrary")),
        name="hgrn2",
    )(pa3, pa3, pa3, pa3, pa3, pa3, hgrn_lb, tri)


def _flash_kernel(*refs, maps, n_v, n_out, tk):
    n_maps = len(maps)
    qt_ref, k_ref = refs[0], refs[1]
    v_refs = refs[2:2 + n_v]
    out_refs = refs[2 + n_v:2 + n_v + n_out]
    scratch = refs[2 + n_v + n_out:]
    acc_ref, m_ref, gap_ref, qv_ref = scratch[:4]
    per_kind = 2 * n_maps
    s_ref, p_ref, al_ref = (
        [scratch[4 + kind * per_kind + slot * n_maps:4 + kind * per_kind + (slot + 1) * n_maps]
         for slot in range(2)] for kind in range(3))
    tq = qt_ref.shape[1]
    n_chunks = k_ref.shape[0] // tk
    sub = m_ref.shape[1]

    def scores(ci, m):
        off = pl.multiple_of(ci * tk, tk)
        return jnp.dot(k_ref[pl.ds(off, tk), :], qv_ref[m], preferred_element_type=F32)

    def column_max(s):
        top = jnp.max(s.reshape(s.shape[0] // sub, sub, tq), axis=0)
        return jnp.broadcast_to(jnp.max(top, axis=0, keepdims=True), (sub, tq))

    def lagged_softmax(ci, slot):
        for m in range(n_maps):
            shift = m_ref[m]
            s = scores(ci, m)
            p_ref[slot][m][...] = jnp.exp2(s - shift[0:1, :]).astype(BF16)
            top = column_max(s)
            gap_ref[m] = jnp.maximum(gap_ref[m], top - shift)
            m_next = jnp.maximum(shift, top)
            m_ref[m] = m_next
            al_ref[slot][m][...] = jnp.exp2(shift - m_next)

    def lagged_pv(ci, slot):
        for m, (_, _, vi, _, _) in enumerate(maps):
            acc_ref[m] = (acc_ref[m] + jnp.dot(v_refs[vi][ci], p_ref[slot][m][...],
                                               preferred_element_type=F32)
                          ) * al_ref[slot][m][0:1, :]

    def lagged_trip(t, carry):
        first = t * CHUNKS_PER_TRIP
        lagged_softmax(first, 0)
        for u in range(1, CHUNKS_PER_TRIP):
            lagged_softmax(first + u, u % 2)
            lagged_pv(first + u - 1, (u - 1) % 2)
        lagged_pv(first + CHUNKS_PER_TRIP - 1, (CHUNKS_PER_TRIP - 1) % 2)
        return carry

    def exact_chunk(ci, slot):
        for m in range(n_maps):
            s_ref[slot][m][...] = scores(ci, m)
        for m, (_, _, vi, _, _) in enumerate(maps):
            m_prev = m_ref[m]
            m_next = jnp.maximum(m_prev, column_max(s_ref[slot][m][...]))
            m_ref[m] = m_next
            for c0 in range(0, tq, COL_BLOCK):
                cols = slice(c0, c0 + COL_BLOCK)
                p_ref[slot][m][:, cols] = jnp.exp2(
                    s_ref[slot][m][:, cols] - m_next[0:1, cols]).astype(BF16)
            acc_ref[m] = acc_ref[m] * jnp.exp2(m_prev - m_next)[0:1, :] + jnp.dot(
                v_refs[vi][ci], p_ref[slot][m][...], preferred_element_type=F32)

    def exact_trip(t, carry):
        for u in range(CHUNKS_PER_TRIP):
            exact_chunk(t * CHUNKS_PER_TRIP + u, u % 2)
        return carry

    qt = qt_ref[...]
    row = lax.broadcasted_iota(jnp.int32, qt.shape, 0)
    for m, (lo, hi, _, _, _) in enumerate(maps):
        qv_ref[m] = jnp.where((row >= lo) & (row < hi), qt, jnp.zeros_like(qt))

    acc_ref[...] = jnp.zeros_like(acc_ref)
    gap_ref[...] = jnp.zeros_like(gap_ref)
    for m in range(n_maps):
        m_ref[m] = column_max(jnp.dot(k_ref[0:LANES, :], qv_ref[m], preferred_element_type=F32))
    lax.fori_loop(0, n_chunks // CHUNKS_PER_TRIP, lagged_trip, 0)

    @pl.when(jnp.logical_not(jnp.max(gap_ref[...]) <= MAX_LAG_LOG2))
    def _():
        acc_ref[...] = jnp.zeros_like(acc_ref)
        m_ref[...] = jnp.full_like(m_ref, NEG_BIG)
        lax.fori_loop(0, n_chunks // CHUNKS_PER_TRIP, exact_trip, 0)

    for oi, out_ref in enumerate(out_refs):
        for m, (_, _, _, mo, half) in enumerate(maps):
            if mo == oi:
                acc = acc_ref[m]
                out_ref[half * GQA_HEAD_DIM:(half + 1) * GQA_HEAD_DIM, :] = (
                    acc[0:GQA_HEAD_DIM, :] * (1.0 / acc[GQA_HEAD_DIM:GQA_HEAD_DIM + 1, :])
                ).astype(BF16)


def _flash(qt3, k4, k_index, vt5, v_indices, maps, n_out, tq, name):
    bsz, qw, seq = qt3.shape
    tk = vt5.shape[-1]
    assert seq % tq == 0 and (seq // tk) % CHUNKS_PER_TRIP == 0, (seq, tq, tk)
    nqb = qw // LANES
    n_maps = len(maps)
    n_v = len(v_indices(0))
    per_kind = 2 * n_maps
    once = pl.Buffered(2)

    def kmap(b, j, qi):
        n, lb = k_index(b, j)
        return (n, b, 0, lb)

    kspec = pl.BlockSpec((None, None, seq, LANES), kmap, pipeline_mode=once)
    vspecs = [pl.BlockSpec((None, None, seq // tk, VALUE_ROWS, tk),
                           lambda b, j, qi, n=n: (v_indices(j)[n], b, 0, 0, 0), pipeline_mode=once)
              for n in range(n_v)]
    qspec = pl.BlockSpec((None, LANES, tq), lambda b, j, qi: (b, j, qi))
    return pl.pallas_call(
        functools.partial(_flash_kernel, maps=maps, n_v=n_v, n_out=n_out, tk=tk),
        grid=(bsz, nqb, seq // tq),
        in_specs=[qspec, kspec] + vspecs,
        out_specs=[qspec] * n_out,
        out_shape=[jax.ShapeDtypeStruct((bsz, qw, seq), BF16)] * n_out,
        scratch_shapes=([pltpu.VMEM((n_maps, VALUE_ROWS, tq), F32),
                         pltpu.VMEM((n_maps, SUBLANES, tq), F32),
                         pltpu.VMEM((n_maps, SUBLANES, tq), F32),
                         pltpu.VMEM((n_maps, LANES, tq), BF16)]
                        + [pltpu.VMEM((tk, tq), F32)] * per_kind
                        + [pltpu.VMEM((tk, tq), BF16)] * per_kind
                        + [pltpu.VMEM((SUBLANES, tq), F32)] * per_kind),
        compiler_params=_params(("parallel", "parallel", "parallel")),
        name=name,
    )(qt3, k4, *([vt5] * n_v))


GQA_MAPS = ((0, GQA_HEAD_DIM, 0, 0, 0), (GQA_HEAD_DIM, LANES, 0, 0, 1))
DIFF_MAPS = tuple((m * DIFF_HEAD_DIM, (m + 1) * DIFF_HEAD_DIM, m // 2, m % 2, m // 2)
                  for m in range(4))


def _outproj_kernel(x_ref, of_ref, or_ref, ob_ref, oc1_ref, oc2_ref, sg_ref, wo_ref,
                    hnw_ref, dnw_ref, lam_ref, postw_ref, g128_ref, g64_ref, out_ref, *, lam_init):
    sg = sg_ref[...].astype(F32)
    a = of_ref[...].astype(F32) + or_ref[...].astype(F32)
    ms = jnp.dot((a * a).astype(BF16), g128_ref[...], preferred_element_type=F32)
    mix_a = a * lax.rsqrt(ms + NORM_EPS) * hnw_ref[...] * sg[:, :HGRN_WIDTH]
    mix_b = ob_ref[...].astype(F32).T * sg[:, HGRN_WIDTH:HGRN_WIDTH + GQA_WIDTH]

    lp = lam_ref[...]
    lam = (jnp.exp(jnp.sum(lp[0:1] * lp[1:2], axis=-1, keepdims=True))
           - jnp.exp(jnp.sum(lp[2:3] * lp[3:4], axis=-1, keepdims=True)) + lam_init)
    c = (oc1_ref[...].astype(F32) - lam * oc2_ref[...].astype(F32)).T
    ms = jnp.dot((c * c).astype(BF16), g64_ref[...], preferred_element_type=F32)
    mix_c = (c * lax.rsqrt(ms + NORM_EPS) * dnw_ref[...] * (1.0 - lam_init)
             * sg[:, HGRN_WIDTH + GQA_WIDTH:])

    y = (jnp.dot(mix_a.astype(BF16), wo_ref[0:HGRN_WIDTH, :], preferred_element_type=F32)
         + jnp.dot(mix_b.astype(BF16), wo_ref[HGRN_WIDTH:HGRN_WIDTH + GQA_WIDTH, :],
                   preferred_element_type=F32)
         + jnp.dot(mix_c.astype(BF16), wo_ref[HGRN_WIDTH + GQA_WIDTH:, :],
                   preferred_element_type=F32))
    out_ref[...] = x_ref[...] + (y * lax.rsqrt(jnp.mean(y * y, axis=-1, keepdims=True) + NORM_EPS)
                                 * postw_ref[...])


def _outproj(x2, of2, or2, obt, oc1t, oc2t, sg, wo_bf, hnw, dnw, lam_p, postw, g128, g64,
             lam_init, tm):
    t = x2.shape[0]
    tiles_per_seq = obt.shape[2] // tm
    tok = lambda w: pl.BlockSpec((tm, w), lambda i: (i, 0))
    att = pl.BlockSpec((None, GQA_WIDTH, tm), lambda i: (i // tiles_per_seq, 0, i % tiles_per_seq))
    return pl.pallas_call(
        functools.partial(_outproj_kernel, lam_init=lam_init),
        grid=(t // tm,),
        in_specs=[tok(D_MODEL), tok(HGRN_WIDTH), tok(HGRN_WIDTH), att, att, att, tok(MIX_WIDTH),
                  _const_spec((MIX_WIDTH, D_MODEL)),
                  _const_spec((1, HGRN_WIDTH)), _const_spec((1, DIFF_WIDTH)),
                  _const_spec((4, DIFF_HEAD_DIM)), _const_spec((1, D_MODEL)),
                  _const_spec((HGRN_WIDTH, HGRN_WIDTH)), _const_spec((DIFF_WIDTH, DIFF_WIDTH))],
        out_specs=tok(D_MODEL),
        out_shape=jax.ShapeDtypeStruct((t, D_MODEL), F32),
        compiler_params=_params(("parallel",)),
        name="outproj",
    )(x2, of2, or2, obt, oc1t, oc2t, sg, wo_bf, hnw, dnw, lam_p, postw, g128, g64)


def _group_mean_matrix(width, group):
    idx = jnp.arange(width) // group
    return ((idx[:, None] == idx[None, :]).astype(F32) / group).astype(BF16)


def _rope_tables(seq_len):
    half = ROPE_DIM // 2
    inv = jnp.power(ROPE_THETA, -jnp.arange(0, ROPE_DIM, 2, dtype=F32) / ROPE_DIM)
    pos = jnp.arange(seq_len, dtype=F32)
    rows = seq_len // GRID_W
    row_pos = jnp.repeat(jnp.arange(rows, dtype=F32), GRID_W)
    col_pos = jnp.tile(jnp.arange(GRID_W, dtype=F32), rows)
    sign = jnp.concatenate([-jnp.ones((half,), F32), jnp.ones((half,), F32)])

    def tab(p):
        ang = p[:, None] * inv[None, :]
        ang = jnp.concatenate([ang, ang], axis=-1)
        return jnp.cos(ang), jnp.sin(ang) * sign

    cr, sr = tab(row_pos)
    cc, sc = tab(col_pos)
    c1, s1 = tab(pos)
    rep_b = GQA_WIDTH // (2 * ROPE_DIM)
    rep_c = DIFF_WIDTH // ROPE_DIM
    return (jnp.tile(jnp.concatenate([cr, cc], axis=-1), (1, rep_b)),
            jnp.tile(jnp.concatenate([sr, sc], axis=-1), (1, rep_b)),
            jnp.tile(c1, (1, rep_c)), jnp.tile(s1, (1, rep_c)))


def _chunk_tri(tb):
    r = jnp.arange(2 * tb)
    same = (r[:, None] // HGRN_CHUNK) == (r[None, :] // HGRN_CHUNK)
    tri = jnp.where(r[:, None] < tb, r[None, :] <= r[:, None], r[None, :] >= r[:, None])
    return (same & tri).astype(BF16)


def _trunk(x, pre_norm_w, w_in_bf, hgrn_lb, hgrn_norm_w, gqa_q_norm_w, gqa_k_norm_w,
           diff_lambda, diff_norm_w, w_out_bf, post_norm_w, *, tabs, g64, g128, tri,
           tm, tb, tq_gqa, tq_diff):
    bsz, seq, _ = x.shape
    t = bsz * seq
    x2 = x.reshape(t, D_MODEL)
    for layer in range(DEPTH):
        qnw = jnp.tile(gqa_q_norm_w[layer], GQA_HEADS)[None, :]
        knw = jnp.tile(gqa_k_norm_w[layer], GQA_KV_HEADS)[None, :]
        pa, qbt, kdup, vbt, qct, kc, vct, sg = _inproj(
            x2, bsz, seq, pre_norm_w[layer][None, :], w_in_bf[layer], tabs, qnw, knw, g64, tm)

        o_f, o_r = _hgrn(pa.reshape(bsz, seq, A_COLS), hgrn_lb, tri, layer, tb)

        (obt,) = _flash(qbt, kdup.reshape(GQA_KV_HEADS, bsz, seq, LANES), lambda b, j: (j, 0),
                        vbt, lambda j: (j,), GQA_MAPS, 1, tq_gqa, "flash_gqa")

        oc1t, oc2t = _flash(qct, kc.reshape(1, bsz, seq, -1), lambda b, j: (0, j),
                            vct, lambda j: (2 * j, 2 * j + 1), DIFF_MAPS, 2, tq_diff, "flash_diff")

        lam_init = 0.8 - 0.6 * math.exp(-0.3 * layer)
        x2 = _outproj(x2, o_f.reshape(t, -1), o_r.reshape(t, -1), obt, oc1t, oc2t, sg,
                      w_out_bf[layer], hgrn_norm_w[layer][None, :],
                      jnp.tile(diff_norm_w[layer], DIFF_HEADS)[None, :],
                      diff_lambda[layer], post_norm_w[layer][None, :], g128, g64, lam_init, tm)
    return x2.reshape(bsz, seq, D_MODEL)


def kernel(x_prompt, x_sample, pre_norm_w, w_in, hgrn_lb, hgrn_norm_w, gqa_q_norm_w, gqa_k_norm_w,
           diff_lambda, diff_norm_w, w_out, post_norm_w):
    w_in_bf = w_in.astype(BF16)
    w_out_bf = w_out.astype(BF16)
    longest = max(x_prompt.shape[1], x_sample.shape[1])
    run = functools.partial(_trunk, pre_norm_w=pre_norm_w, w_in_bf=w_in_bf, hgrn_lb=hgrn_lb,
                            hgrn_norm_w=hgrn_norm_w, gqa_q_norm_w=gqa_q_norm_w,
                            gqa_k_norm_w=gqa_k_norm_w, diff_lambda=diff_lambda,
                            diff_norm_w=diff_norm_w, w_out_bf=w_out_bf, post_norm_w=post_norm_w,
                            tabs=_rope_tables(longest),
                            g64=_group_mean_matrix(GQA_WIDTH, GQA_HEAD_DIM),
                            g128=_group_mean_matrix(HGRN_WIDTH, HGRN_DIM),
                            tri=_chunk_tri(TILES["hgrn_block"]),
                            tm=TILES["proj_rows"], tb=TILES["hgrn_block"],
                            tq_gqa=TILES["gqa_queries"], tq_diff=TILES["diff_queries"])
    return (run(x_prompt), run(x_sample))
```
